```python
import numpy as np
import jax
import jax.numpy as jnp
from jax import lax

D_MODEL = 2048
BATCH = 8
SEQ = 2048
DEPTH = 2

CTX_LEN = 256
GRID_W = 64
MIX_WIDTH = D_MODEL
GROUP_WIDTH = MIX_WIDTH // 4
HEAD_DIM = 128
LRU_WIDTH = GROUP_WIDTH
LRU_BLOCKS = 4
LRU_BLOCK_W = LRU_WIDTH // LRU_BLOCKS
CONV_W = 4
CONV_PAD_LEFT = 2
RGLRU_C = 8.0
GLA_HEADS = 4
GLA_DV = GROUP_WIDTH // GLA_HEADS
GLA_DK = GLA_DV // 2
GLA_GATE_RANK = 16
GLA_TAU = 16.0
GLA_CHUNK = 64
ATT_HEADS = GROUP_WIDTH // HEAD_DIM
ATT_KV_HEADS = 2
ATT_GROUP = ATT_HEADS // ATT_KV_HEADS
Q_BLOCK = 128
ROPE_THETA = 10000.0
WIN_HEADS = GROUP_WIDTH // HEAD_DIM
WIN_KV_HEADS = 2
WIN_GROUP = WIN_HEADS // WIN_KV_HEADS
WINDOW = 128
N_EXPERTS = 64
N_EXPERT_GROUPS = 8
TOPK_GROUPS = 4
TOP_K = 8
EXPERT_HIDDEN = 512
SHARED_HIDDEN = 512
ROUTED_SCALE = 2.5
MOE_BLOCK = 128

RMS_EPS = 1e-6
NEG_INF = -1e30
IN_SPLITS = (LRU_WIDTH, LRU_WIDTH,
             GLA_HEADS * GLA_DK, GLA_HEADS * GLA_DK, GLA_HEADS * GLA_DV, GLA_HEADS * GLA_DV,
             GLA_GATE_RANK, GLA_GATE_RANK,
             ATT_HEADS * HEAD_DIM, ATT_KV_HEADS * HEAD_DIM, ATT_KV_HEADS * HEAD_DIM,
             WIN_HEADS * HEAD_DIM, WIN_KV_HEADS * HEAD_DIM, WIN_KV_HEADS * HEAD_DIM)
IN_COLS = sum(IN_SPLITS)
F32 = jnp.float32

kernel_name = 'hybrid_prefix_dit_block'


def rms_norm(x, g):
    xf = x.astype(F32)
    y = xf * lax.rsqrt(jnp.mean(xf * xf, axis=-1, keepdims=True) + RMS_EPS)
    return (y * g.astype(F32)).astype(x.dtype)


def modulate(x, shift, scale):
    return x * (1.0 + scale) + shift


def heads(t, n):
    return t.reshape(*t.shape[:-1], n, t.shape[-1] // n)


def swiglu(h, w_gate, w_up, w_down):
    return (jax.nn.silu(h @ w_gate) * (h @ w_up)) @ w_down


def axial_rope_tables(n_tok):
    rows = n_tok // GRID_W
    n_freq = HEAD_DIM // 4
    inv_freq = ROPE_THETA ** (-jnp.arange(n_freq, dtype=F32) / n_freq)
    row = jnp.repeat(jnp.arange(rows, dtype=F32), GRID_W)
    col = jnp.tile(jnp.arange(GRID_W, dtype=F32), rows)
    ang = jnp.stack([row, col], axis=-1)[:, :, None] * inv_freq
    return jnp.cos(ang), jnp.sin(ang)


def apply_axial_rope(x, cos, sin):
    b, n, h, d = x.shape
    xr = x.astype(F32).reshape(b, n, h, 2, 2, d // 4)
    x1, x2 = xr[..., 0, :], xr[..., 1, :]
    cs, sn = cos[None, :, None], sin[None, :, None]
    out = jnp.stack([x1 * cs - x2 * sn, x1 * sn + x2 * cs], axis=-2)
    return out.reshape(b, n, h, d).astype(x.dtype)


def centred_depthwise_conv(x, w, b):
    y = lax.conv_general_dilated(
        x, w[:, None, :].astype(x.dtype), window_strides=(1,),
        padding=[(CONV_PAD_LEFT, CONV_W - 1 - CONV_PAD_LEFT)],
        dimension_numbers=('NWC', 'WIO', 'NWC'), feature_group_count=x.shape[-1])
    return y + b.astype(x.dtype)


def linear_scan(a, b, h0):
    def combine(left, right):
        return left[0] * right[0], right[0] * left[1] + right[1]
    a_cum, b_cum = lax.associative_scan(combine, (a, b), axis=1)
    return a_cum * h0[:, None] + b_cum


def rglru_mixer(x_lat, g_lat, x_ctx, g_ctx, conv_w, conv_b, w_r, b_r, w_i, b_i, lam, need_ctx):
    n_ctx = x_ctx.shape[1]
    u = jnp.concatenate([centred_depthwise_conv(x_ctx, conv_w, conv_b),
                         centred_depthwise_conv(x_lat, conv_w, conv_b)], axis=1).astype(F32)
    u_blk = heads(u, LRU_BLOCKS)
    h_ctx_sum, h_lat_sum = 0.0, 0.0
    for d in range(2):
        r = jax.nn.sigmoid(jnp.einsum('bshi,hij->bshj', u_blk, w_r[d].astype(F32)).reshape(u.shape) + b_r[d])
        i = jax.nn.sigmoid(jnp.einsum('bshi,hij->bshj', u_blk, w_i[d].astype(F32)).reshape(u.shape) + b_i[d])
        log_a = -RGLRU_C * r * jax.nn.softplus(-lam[d].astype(F32))
        a = jnp.exp(log_a)
        bx = jnp.sqrt(-jnp.expm1(2.0 * log_a)) * (i * u)
        a_c, a_l, b_c, b_l = a[:, :n_ctx], a[:, n_ctx:], bx[:, :n_ctx], bx[:, n_ctx:]
        if d == 1:
            a_c, a_l, b_c, b_l = (jnp.flip(t, 1) for t in (a_c, a_l, b_c, b_l))
        h_c = linear_scan(a_c, b_c, jnp.zeros_like(b_c[:, 0]))
        h_l = linear_scan(a_l, b_l, h_c[:, -1])
        if d == 1:
            h_c, h_l = jnp.flip(h_c, 1), jnp.flip(h_l, 1)
        h_ctx_sum = h_ctx_sum + h_c
        h_lat_sum = h_lat_sum + h_l
    y_lat = (h_lat_sum * jax.nn.gelu(g_lat.astype(F32))).astype(x_lat.dtype)
    y_ctx = (h_ctx_sum * jax.nn.gelu(g_ctx.astype(F32))).astype(x_ctx.dtype) if need_ctx else None
    return y_lat, y_ctx


def gla_chunk_scan(q, k, v, log_a, s0):
    bsz, s, h, dk = q.shape
    dv = v.shape[-1]
    n_chunks = s // GLA_CHUNK

    def chunks(t):
        return jnp.swapaxes(t.reshape(bsz, n_chunks, GLA_CHUNK, *t.shape[2:]), 0, 1)

    lower = jnp.tril(jnp.ones((GLA_CHUNK, GLA_CHUNK), bool))[None, :, :, None, None]

    def step(state, inp):
        qc, kc, vc, ac = inp
        cum = jnp.cumsum(ac, axis=1)
        last = cum[:, -1]
        diff = cum[:, :, None] - cum[:, None, :]
        decay = jnp.exp(jnp.where(lower, diff, -jnp.inf))
        scores = jnp.einsum('btshd,bshd->bhts', qc[:, :, None] * decay, kc)
        o = (jnp.einsum('bhts,bshv->bthv', scores, vc)
             + jnp.einsum('bthd,bhdv->bthv', qc * jnp.exp(cum), state))
        state = (jnp.exp(last)[..., None] * state
                 + jnp.einsum('bshd,bshv->bhdv', kc * jnp.exp(last[:, None] - cum), vc))
        return state, o

    state, o = lax.scan(step, s0, (chunks(q), chunks(k), chunks(v), chunks(log_a)))
    return jnp.swapaxes(o, 0, 1).reshape(bsz, s, h, dv), state


def gla_mixer(p_lat, p_ctx, w2, b2, norm_g, need_ctx):
    def prep(p):
        q, k, v, g, z_f, z_b = (t.astype(F32) for t in p)
        log_a = [heads(jax.nn.log_sigmoid(z @ w2[d].astype(F32) + b2[d]) / GLA_TAU, GLA_HEADS)
                 for d, z in enumerate((z_f, z_b))]
        return heads(q, GLA_HEADS) * GLA_DK ** -0.5, heads(k, GLA_HEADS), heads(v, GLA_HEADS), g, log_a

    q_l, k_l, v_l, g_l, la_l = prep(p_lat)
    q_c, k_c, v_c, g_c, la_c = prep(p_ctx)
    s0 = jnp.zeros((q_l.shape[0], GLA_HEADS, GLA_DK, GLA_DV), F32)
    o_lat, o_ctx = 0.0, 0.0
    for d in range(2):
        seq_c = (q_c, k_c, v_c, la_c[d])
        seq_l = (q_l, k_l, v_l, la_l[d])
        if d == 1:
            seq_c = tuple(jnp.flip(t, 1) for t in seq_c)
            seq_l = tuple(jnp.flip(t, 1) for t in seq_l)
        oc, s_ctx = gla_chunk_scan(*seq_c, s0)
        ol, _ = gla_chunk_scan(*seq_l, s_ctx)
        if d == 1:
            oc, ol = jnp.flip(oc, 1), jnp.flip(ol, 1)
        o_lat = o_lat + ol
        o_ctx = o_ctx + oc

    def finish(o, g):
        return (rms_norm(o, norm_g) * jax.nn.silu(heads(g, GLA_HEADS))).reshape(g.shape)

    y_lat = finish(o_lat, g_l).astype(p_lat[0].dtype)
    y_ctx = finish(o_ctx, g_c).astype(p_ctx[0].dtype) if need_ctx else None
    return y_lat, y_ctx


def gqa_attend(q, k, v, mask=None, sink=None):
    s = jnp.einsum('...qkgd,...skd->...kgqs', q, k).astype(F32) * (q.shape[-1] ** -0.5)
    if mask is not None:
        s = jnp.where(mask, s, NEG_INF)
    if sink is not None:
        sink_col = jnp.broadcast_to(sink.astype(F32).reshape(k.shape[-2], -1)[:, :, None, None],
                                    s.shape[:-1] + (1,))
        p = jax.nn.softmax(jnp.concatenate([s, sink_col], axis=-1), axis=-1)[..., :-1]
    else:
        p = jax.nn.softmax(s, axis=-1)
    return jnp.einsum('...kgqs,...skd->...qkgd', p.astype(v.dtype), v)


def global_attention(p_lat, p_ctx, qk_g, cos, sin, need_ctx):
    q_l, k_l, v_l = p_lat
    q_c, k_c, v_c = p_ctx
    bsz, n = q_l.shape[:2]
    n_blocks = n // Q_BLOCK
    q_l = apply_axial_rope(rms_norm(heads(q_l, ATT_HEADS), qk_g[0]), cos, sin)
    k_l = apply_axial_rope(rms_norm(heads(k_l, ATT_KV_HEADS), qk_g[1]), cos, sin)
    k_c = rms_norm(heads(k_c, ATT_KV_HEADS), qk_g[1])
    v_c = heads(v_c, ATT_KV_HEADS)
    k_all = jnp.concatenate([k_c, k_l], axis=1)
    v_all = jnp.concatenate([v_c, heads(v_l, ATT_KV_HEADS)], axis=1)
    q_blocks = jnp.moveaxis(q_l.reshape(bsz, n_blocks, Q_BLOCK, ATT_KV_HEADS, ATT_GROUP, HEAD_DIM), 1, 0)
    o_blocks = lax.map(lambda qb: gqa_attend(qb, k_all, v_all), q_blocks)
    y_lat = jnp.moveaxis(o_blocks, 0, 1).reshape(bsz, n, ATT_HEADS * HEAD_DIM)
    y_ctx = None
    if need_ctx:
        q_c = rms_norm(heads(q_c, ATT_HEADS), qk_g[0])
        q_c = q_c.reshape(*q_c.shape[:2], ATT_KV_HEADS, ATT_GROUP, HEAD_DIM)
        y_ctx = gqa_attend(q_c, k_c, v_c).reshape(bsz, -1, ATT_HEADS * HEAD_DIM)
    return y_lat, y_ctx


def window_attention(p_lat, p_ctx, sink, cos, sin, need_ctx):
    q_l, k_l, v_l = p_lat
    q_c, k_c, v_c = p_ctx
    bsz, n = q_l.shape[:2]
    n_ctx = q_c.shape[1]
    n_blocks = n // WINDOW
    q_l = apply_axial_rope(heads(q_l, WIN_HEADS), cos, sin)
    k_l = apply_axial_rope(heads(k_l, WIN_KV_HEADS), cos, sin)
    v_l = heads(v_l, WIN_KV_HEADS)
    k_c = heads(k_c, WIN_KV_HEADS)
    v_c = heads(v_c, WIN_KV_HEADS)

    def band(t):
        tp = jnp.pad(t, ((0, 0), (WINDOW, WINDOW), (0, 0), (0, 0)))
        tp = tp.reshape(bsz, n_blocks + 2, WINDOW, *t.shape[2:])
        return jnp.concatenate([tp[:, :-2], tp[:, 1:-1], tp[:, 2:]], axis=2)

    def with_ctx(t_band, t_ctx):
        return jnp.concatenate([t_band, jnp.broadcast_to(t_ctx[:, None], (bsz, n_blocks) + t_ctx.shape[1:])], axis=2)

    k_all = with_ctx(band(k_l), k_c)
    v_all = with_ctx(band(v_l), v_c)
    q_pos = jnp.arange(n_blocks)[:, None] * WINDOW + jnp.arange(WINDOW)
    k_pos = (jnp.arange(n_blocks)[:, None] - 1) * WINDOW + jnp.arange(3 * WINDOW)
    kp = k_pos[:, None, :]
    in_band = (jnp.abs(q_pos[:, :, None] - kp) <= WINDOW) & (kp >= 0) & (kp < n)
    mask = jnp.concatenate([in_band, jnp.ones((n_blocks, WINDOW, n_ctx), bool)], axis=-1)[:, None, None]
    q_blocks = q_l.reshape(bsz, n_blocks, WINDOW, WIN_KV_HEADS, WIN_GROUP, HEAD_DIM)
    y_lat = gqa_attend(q_blocks, k_all, v_all, mask, sink).reshape(bsz, n, WIN_HEADS * HEAD_DIM)
    y_ctx = None
    if need_ctx:
        q_c = heads(q_c, WIN_HEADS).reshape(bsz, n_ctx, WIN_KV_HEADS, WIN_GROUP, HEAD_DIM)
        y_ctx = gqa_attend(q_c, k_c, v_c, sink=sink).reshape(bsz, n_ctx, WIN_HEADS * HEAD_DIM)
    return y_lat, y_ctx


def moe_ffn(h, router_w, router_bias, w_gate, w_up, w_down, s_gate, s_up, s_down):
    t, d = h.shape
    scores = jax.nn.sigmoid(h.astype(F32) @ router_w.astype(F32))
    biased = scores + router_bias.astype(F32)
    group_score = lax.top_k(biased.reshape(t, N_EXPERT_GROUPS, -1), 2)[0].sum(-1)
    _, top_groups = lax.top_k(group_score, TOPK_GROUPS)
    group_ok = jnp.any(top_groups[:, :, None] == jnp.arange(N_EXPERT_GROUPS), axis=1)
    eligible = jnp.repeat(group_ok, N_EXPERTS // N_EXPERT_GROUPS, axis=1)
    _, top_e = lax.top_k(jnp.where(eligible, biased, -jnp.inf), TOP_K)
    gate = jnp.take_along_axis(scores, top_e, axis=1)
    gate = gate / jnp.sum(gate, axis=-1, keepdims=True) * ROUTED_SCALE
    n_assign = t * TOP_K
    flat_e = top_e.reshape(-1)
    flat_tok = jnp.repeat(jnp.arange(t, dtype=jnp.int32), TOP_K)
    order = jnp.argsort(flat_e)
    e_sorted = flat_e[order]
    counts = jnp.bincount(flat_e, length=N_EXPERTS)
    padded = (counts + MOE_BLOCK - 1) // MOE_BLOCK * MOE_BLOCK
    pad_end = jnp.cumsum(padded)
    pad_start = pad_end - padded
    start = jnp.cumsum(counts) - counts
    dest = pad_start[e_sorted] + jnp.arange(n_assign, dtype=jnp.int32) - start[e_sorted]
    n_blocks = -(-(n_assign + N_EXPERTS * (MOE_BLOCK - 1)) // MOE_BLOCK)
    n_rows = n_blocks * MOE_BLOCK
    row_tok = jnp.full((n_rows,), t, dtype=jnp.int32).at[dest].set(flat_tok[order])
    row_w = jnp.zeros((n_rows,), F32).at[dest].set(gate.reshape(-1)[order])
    block_e = jnp.minimum(jnp.searchsorted(pad_end, jnp.arange(n_blocks, dtype=jnp.int32) * MOE_BLOCK, side='right'),
                          N_EXPERTS - 1)
    h_pad = jnp.concatenate([h, jnp.zeros((1, d), h.dtype)], axis=0)

    def expert_block(acc, inp):
        tok_b, w_b, e = inp
        y = swiglu(h_pad[tok_b], w_gate[e], w_up[e], w_down[e]).astype(F32) * w_b[:, None]
        return acc.at[tok_b].add(y), None

    acc, _ = lax.scan(expert_block, jnp.zeros((t + 1, d), F32),
                      (row_tok.reshape(n_blocks, MOE_BLOCK), row_w.reshape(n_blocks, MOE_BLOCK), block_e))
    return (acc[:t] + swiglu(h, s_gate, s_up, s_down).astype(F32)).astype(h.dtype)


def setup_inputs(seed: int = 0) -> dict:
    key = jax.random.key(seed)
    ks = iter(jax.random.split(key, 32))

    def nrm(shape, scale):
        return jax.random.normal(next(ks), shape, F32) * scale

    x = nrm((BATCH, SEQ, D_MODEL), 1.0)
    c = nrm((BATCH, D_MODEL), 1.0)
    ctx = nrm((BATCH, CTX_LEN, D_MODEL), 1.0)
    c_ctx = nrm((D_MODEL,), 1.0)
    mod_w = nrm((DEPTH, D_MODEL, 6 * D_MODEL), 0.5 * D_MODEL ** -0.5)
    mod_b = nrm((DEPTH, 6 * D_MODEL), 0.01)
    norm_g = 1.0 + nrm((DEPTH, 4, D_MODEL), 0.02)
    w_in = nrm((DEPTH, D_MODEL, IN_COLS), D_MODEL ** -0.5)
    w_out = nrm((DEPTH, MIX_WIDTH, D_MODEL), MIX_WIDTH ** -0.5)
    conv_w = nrm((DEPTH, CONV_W, LRU_WIDTH), CONV_W ** -0.5)
    conv_b = nrm((DEPTH, LRU_WIDTH), 0.01)
    lru_wr = nrm((DEPTH, 2, LRU_BLOCKS, LRU_BLOCK_W, LRU_BLOCK_W), LRU_BLOCK_W ** -0.5)
    lru_br = nrm((DEPTH, 2, LRU_WIDTH), 0.01)
    lru_wi = nrm((DEPTH, 2, LRU_BLOCKS, LRU_BLOCK_W, LRU_BLOCK_W), LRU_BLOCK_W ** -0.5)
    lru_bi = nrm((DEPTH, 2, LRU_WIDTH), 0.01)
    a_pow_c = jax.random.uniform(next(ks), (DEPTH, 2, LRU_WIDTH), F32, 0.9, 0.999)
    p = a_pow_c ** (1.0 / RGLRU_C)
    lru_lambda = jnp.log(p) - jnp.log1p(-p)
    gla_w2 = nrm((DEPTH, 2, GLA_GATE_RANK, GLA_HEADS * GLA_DK), GLA_GATE_RANK ** -0.5)
    gla_b2 = nrm((DEPTH, 2, GLA_HEADS * GLA_DK), 0.1)
    gla_norm_g = 1.0 + nrm((DEPTH, GLA_DV), 0.02)
    qk_norm_g = 1.0 + nrm((DEPTH, 2, HEAD_DIM), 0.02)
    win_sink = nrm((DEPTH, WIN_HEADS), 0.5)
    router_w = nrm((DEPTH, D_MODEL, N_EXPERTS), D_MODEL ** -0.5)
    router_bias = nrm((DEPTH, N_EXPERTS), 0.01)
    exp_w_gate = nrm((DEPTH, N_EXPERTS, D_MODEL, EXPERT_HIDDEN), D_MODEL ** -0.5)
    exp_w_up = nrm((DEPTH, N_EXPERTS, D_MODEL, EXPERT_HIDDEN), D_MODEL ** -0.5)
    exp_w_down = nrm((DEPTH, N_EXPERTS, EXPERT_HIDDEN, D_MODEL), EXPERT_HIDDEN ** -0.5)
    sh_w_gate = nrm((DEPTH, D_MODEL, SHARED_HIDDEN), D_MODEL ** -0.5)
    sh_w_up = nrm((DEPTH, D_MODEL, SHARED_HIDDEN), D_MODEL ** -0.5)
    sh_w_down = nrm((DEPTH, SHARED_HIDDEN, D_MODEL), SHARED_HIDDEN ** -0.5)
    return {'x': x, 'c': c, 'ctx': ctx, 'c_ctx': c_ctx, 'mod_w': mod_w, 'mod_b': mod_b,
            'norm_g': norm_g, 'w_in': w_in, 'w_out': w_out, 'conv_w': conv_w, 'conv_b': conv_b,
            'lru_wr': lru_wr, 'lru_br': lru_br, 'lru_wi': lru_wi, 'lru_bi': lru_bi,
            'lru_lambda': lru_lambda, 'gla_w2': gla_w2, 'gla_b2': gla_b2, 'gla_norm_g': gla_norm_g,
            'qk_norm_g': qk_norm_g, 'win_sink': win_sink, 'router_w': router_w,
            'router_bias': router_bias, 'exp_w_gate': exp_w_gate, 'exp_w_up': exp_w_up,
            'exp_w_down': exp_w_down, 'sh_w_gate': sh_w_gate, 'sh_w_up': sh_w_up,
            'sh_w_down': sh_w_down}


def reference(x, c, ctx, c_ctx, mod_w, mod_b, norm_g, w_in, w_out, conv_w, conv_b,
              lru_wr, lru_br, lru_wi, lru_bi, lru_lambda, gla_w2, gla_b2, gla_norm_g,
              qk_norm_g, win_sink, router_w, router_bias, exp_w_gate, exp_w_up, exp_w_down,
              sh_w_gate, sh_w_up, sh_w_down):
    bsz, n_lat, d = x.shape
    rope_cos, rope_sin = axial_rope_tables(n_lat)
    split_at = [int(v) for v in np.cumsum(IN_SPLITS)[:-1]]
    for layer in range(DEPTH):
        need_ctx = layer < DEPTH - 1
        m_lat = jnp.split((jax.nn.silu(c) @ mod_w[layer] + mod_b[layer])[:, None, :], 6, axis=-1)
        m_ctx = jnp.split(jax.nn.silu(c_ctx) @ mod_w[layer] + mod_b[layer], 6, axis=-1)
        g = norm_g[layer]
        h_lat = modulate(rms_norm(x, g[0]), m_lat[0], m_lat[1])
        h_ctx = modulate(rms_norm(ctx, g[0]), m_ctx[0], m_ctx[1])
        p_lat = jnp.split(h_lat @ w_in[layer], split_at, axis=-1)
        p_ctx = jnp.split(h_ctx @ w_in[layer], split_at, axis=-1)
        outs = [
            rglru_mixer(p_lat[0], p_lat[1], p_ctx[0], p_ctx[1], conv_w[layer], conv_b[layer],
                        lru_wr[layer], lru_br[layer], lru_wi[layer], lru_bi[layer], lru_lambda[layer], need_ctx),
            gla_mixer(p_lat[2:8], p_ctx[2:8], gla_w2[layer], gla_b2[layer], gla_norm_g[layer], need_ctx),
            global_attention(p_lat[8:11], p_ctx[8:11], qk_norm_g[layer], rope_cos, rope_sin, need_ctx),
            window_attention(p_lat[11:14], p_ctx[11:14], win_sink[layer], rope_cos, rope_sin, need_ctx),
        ]
        y_lat = jnp.concatenate([o[0] for o in outs], axis=-1) @ w_out[layer]
        x = x + m_lat[2] * rms_norm(y_lat, g[1])
        ffn_in = [modulate(rms_norm(x, g[2]), m_lat[3], m_lat[4]).reshape(-1, d)]
        if need_ctx:
            y_ctx = jnp.concatenate([o[1] for o in outs], axis=-1) @ w_out[layer]
            ctx = ctx + m_ctx[2] * rms_norm(y_ctx, g[1])
            ffn_in.append(modulate(rms_norm(ctx, g[2]), m_ctx[3], m_ctx[4]).reshape(-1, d))
        f = moe_ffn(jnp.concatenate(ffn_in, axis=0), router_w[layer], router_bias[layer],
                    exp_w_gate[layer], exp_w_up[layer], exp_w_down[layer],
                    sh_w_gate[layer], sh_w_up[layer], sh_w_down[layer])
        x = x + m_lat[5] * rms_norm(f[:bsz * n_lat].reshape(x.shape), g[3])
        if need_ctx:
            ctx = ctx + m_ctx[5] * rms_norm(f[bsz * n_lat:].reshape(ctx.shape), g[3])
    return x
```

```python
import functools

import numpy as np
import jax
import jax.numpy as jnp
from jax import lax
from jax.experimental import pallas as pl
from jax.experimental.pallas import tpu as pltpu

F32 = jnp.float32
BF16 = jnp.bfloat16
I32 = jnp.int32
U32 = jnp.uint32
HIGHEST = lax.Precision.HIGHEST
NT_DIMS = (((1,), (1,)), ((), ()))
TN_DIMS = (((0,), (0,)), ((), ()))

LANES = 128
SUBLANES = 8
VMEM_LIMIT_BYTES = 56 * 1024 * 1024

D_MODEL = 2048
GRID_W = 64
HEAD_DIM = 128
GROUP_WIDTH = D_MODEL // 4
LRU_BLOCKS = 4
CONV_W = 4
RGLRU_C = 8.0
GLA_HEADS = 4
GLA_DK = 64
GLA_DV = 128
GLA_RANK = 16
GLA_TAU = 16.0
GLA_CHUNK = 64
KV_HEADS = 2
WINDOW = 128
N_EXPERTS = 64
N_GROUPS = 8
TOPK_GROUPS = 4
TOP_K = 8
EXPERT_HIDDEN = 512
ROUTED_SCALE = 2.5
RMS_EPS = 1e-6
ROPE_THETA = 10000.0
NEG_INF = -1e30

W_A = 2 * GROUP_WIDTH
W_B_RAW = 2 * GLA_HEADS * GLA_DK + 2 * GLA_HEADS * GLA_DV + 2 * GLA_RANK
W_B = 1664
W_C = GROUP_WIDTH + 2 * KV_HEADS * HEAD_DIM
W_D = W_C
Z_COL_BLOCK = (2 * GLA_HEADS * GLA_DK + 2 * GLA_HEADS * GLA_DV) // LANES

TM = 256
MOE_BLK = 256
MOD_ROWS = 16


def _params(sem, vmem=VMEM_LIMIT_BYTES):
    return pltpu.CompilerParams(dimension_semantics=sem, vmem_limit_bytes=vmem)


def _rms(x, g):
    return x * lax.rsqrt(jnp.mean(x * x, axis=-1, keepdims=True) + RMS_EPS) * g


def _silu(x):
    return x * jax.nn.sigmoid(x)


def _pack_bf16_pairs(h):
    n = h.shape[1] // 2
    hb = h.astype(BF16).astype(F32)
    hi = lax.bitcast_convert_type(hb[:, :n], U32)
    lo = lax.bitcast_convert_type(hb[:, n:], U32)
    return hi | (lo >> 16)


def _unpack_bf16_pairs(p):
    hi = lax.bitcast_convert_type(p & jnp.uint32(0xFFFF0000), F32)
    lo = lax.bitcast_convert_type(p << 16, F32)
    return jnp.concatenate([hi, lo], axis=1).astype(BF16)


def _adaln_kernel(c_ref, w_ref, b_ref, o_ref):
    a = _silu(c_ref[...])
    o_ref[0] = jnp.dot(a, w_ref[0], preferred_element_type=F32, precision=HIGHEST) + b_ref[0]


def _adaln(cc, mod_w, mod_b):
    depth, d, n = mod_w.shape
    tn = 1024
    return pl.pallas_call(
        _adaln_kernel,
        grid=(depth, n // tn),
        in_specs=[pl.BlockSpec((MOD_ROWS, d), lambda l, j: (0, 0)),
                  pl.BlockSpec((1, d, tn), lambda l, j: (l, 0, j)),
                  pl.BlockSpec((1, 1, tn), lambda l, j: (l, 0, j))],
        out_specs=pl.BlockSpec((1, MOD_ROWS, tn), lambda l, j: (l, 0, j)),
        out_shape=jax.ShapeDtypeStruct((depth, MOD_ROWS, n), F32),
        compiler_params=_params(("arbitrary", "arbitrary")),
        name="adaln",
    )(cc, mod_w, mod_b.reshape(depth, 1, n))


class _Layout:
    def __init__(self, bsz, n_lat, n_ctx):
        self.bsz, self.n_lat, self.n_ctx = bsz, n_lat, n_ctx
        self.seq = n_ctx + n_lat
        self.rows = bsz * self.seq
        assert n_ctx % TM == 0 and n_lat % TM == 0
        self.tiles_b = self.seq // TM
        self.ctx_tiles = n_ctx // TM
        self.lat_tiles = n_lat // TM

    def n_tiles(self, lat_only):
        return self.bsz * (self.lat_tiles if lat_only else self.tiles_b)

    def tile(self, i, lat_only):
        if lat_only:
            return (i // self.lat_tiles) * self.tiles_b + self.ctx_tiles + i % self.lat_tiles
        return i

    def mod_row(self, i, lat_only):
        if lat_only:
            return i // self.lat_tiles
        return jnp.where(i % self.tiles_b < self.ctx_tiles, self.bsz, i // self.tiles_b)


def _mod_spec(lay, chunk, lat_only):
    return pl.BlockSpec((1, 1, D_MODEL), lambda i: (lay.mod_row(i, lat_only) * 6 + chunk, 0, 0))


def _inproj_kernel(x_ref, sh_ref, sc_ref, g_ref, w_ref, oa_ref, ob_ref, oc_ref, od_ref):
    h = _rms(x_ref[...], g_ref[...]) * (1.0 + sc_ref[0]) + sh_ref[0]
    hb = h.astype(BF16)
    c0 = 0
    for o_ref in (oa_ref, ob_ref, oc_ref, od_ref):
        width = o_ref.shape[1]
        o_ref[...] = jnp.dot(hb, w_ref[:, c0:c0 + width], preferred_element_type=F32)
        c0 += width


def _inproj(lay, xt, mods, g0, w_cat):
    widths = (W_A, W_B, W_C, W_D)
    row = lambda i: (i, 0)
    return pl.pallas_call(
        _inproj_kernel,
        grid=(lay.n_tiles(False),),
        in_specs=[pl.BlockSpec((TM, D_MODEL), row),
                  _mod_spec(lay, 0, False), _mod_spec(lay, 1, False),
                  pl.BlockSpec((1, D_MODEL), lambda i: (0, 0)),
                  pl.BlockSpec((D_MODEL, sum(widths)), lambda i: (0, 0), pipeline_mode=pl.Buffered(1))],
        out_specs=[pl.BlockSpec((TM, w), row) for w in widths],
        out_shape=[jax.ShapeDtypeStruct((lay.rows, w), F32) for w in widths],
        compiler_params=_params(("arbitrary",)),
        name="inproj",
    )(xt, mods, mods, g0, w_cat)


def _rglru_kernel(x_ref, g_ref, cw_ref, cb_ref, wr_ref, br_ref, wi_ref, bi_ref, lam_ref, o_ref,
                  xs_ref, a_ref, b_ref, h_ref, *, n_ctx):
    seq = x_ref.shape[0]
    pad = SUBLANES
    n_tiles = seq // SUBLANES
    ctx_tiles = n_ctx // SUBLANES
    row = lax.broadcasted_iota(I32, (seq, LANES), 0)

    xs_ref[0:pad, :] = jnp.zeros((pad, LANES), F32)
    xs_ref[pad + seq:2 * pad + seq, :] = jnp.zeros((pad, LANES), F32)
    xs_ref[pad:pad + seq, :] = x_ref[...]
    u = jnp.zeros((seq, LANES), F32) + cb_ref[...]
    for j in range(CONV_W):
        off = j - 2
        xsh = xs_ref[pad + off:pad + off + seq, :]
        if off < 0:
            xsh = jnp.where((row >= n_ctx) & (row + off < n_ctx), 0.0, xsh)
        elif off > 0:
            xsh = jnp.where((row < n_ctx) & (row + off >= n_ctx), 0.0, xsh)
        u = u + cw_ref[j:j + 1, :] * xsh

    ub = u.astype(BF16)
    rmod = row & (SUBLANES - 1)
    for d in range(2):
        reverse = d == 1
        r = jax.nn.sigmoid(jnp.dot(ub, wr_ref[d, 0].astype(BF16), preferred_element_type=F32) + br_ref[d:d + 1, :])
        gi = jax.nn.sigmoid(jnp.dot(ub, wi_ref[d, 0].astype(BF16), preferred_element_type=F32) + bi_ref[d:d + 1, :])
        lam = lam_ref[d:d + 1, :]
        softplus_neg = jnp.maximum(-lam, 0.0) + jnp.log(1.0 + jnp.exp(-jnp.abs(lam)))
        a = jnp.exp(-RGLRU_C * r * softplus_neg)
        b = jnp.sqrt(1.0 - a * a) * (gi * u)
        for s in (1, 2, 4):
            if reverse:
                ap, bp, ok = pltpu.roll(a, seq - s, 0), pltpu.roll(b, seq - s, 0), rmod < SUBLANES - s
            else:
                ap, bp, ok = pltpu.roll(a, s, 0), pltpu.roll(b, s, 0), rmod >= s
            b = jnp.where(ok, a * bp + b, b)
            a = jnp.where(ok, a * ap, a)
        a_ref[...] = a
        b_ref[...] = b

        def step(t, h, reverse=reverse):
            r0 = pl.multiple_of(t * SUBLANES, SUBLANES)
            h8 = a_ref[pl.ds(r0, SUBLANES), :] * h + b_ref[pl.ds(r0, SUBLANES), :]
            if reverse:
                h_ref[pl.ds(r0, SUBLANES), :] = h_ref[pl.ds(r0, SUBLANES), :] + h8
                return jnp.broadcast_to(h8[0:1, :], (SUBLANES, LANES))
            h_ref[pl.ds(r0, SUBLANES), :] = h8
            return jnp.broadcast_to(h8[SUBLANES - 1:SUBLANES, :], (SUBLANES, LANES))

        h0 = jnp.zeros((SUBLANES, LANES), F32)
        if reverse:
            hc = lax.fori_loop(0, ctx_tiles, lambda i, h: step(ctx_tiles - 1 - i, h), h0)
            lax.fori_loop(0, n_tiles - ctx_tiles, lambda i, h: step(n_tiles - 1 - i, h), hc)
        else:
            lax.fori_loop(0, n_tiles, step, h0)

    g = g_ref[...]
    gelu = 0.5 * g * (1.0 + jnp.tanh(np.sqrt(2.0 / np.pi) * (g + 0.044715 * (g * g * g))))
    o_ref[...] = h_ref[...] * gelu


def _rglru(lay, p_a, conv_w, conv_b, wr, br, wi, bi, lam):
    seq = lay.seq
    nb = LRU_BLOCKS
    col = lambda off: (lambda b, j: (b, off + j))
    vec = lambda rows: pl.BlockSpec((rows, LANES), lambda b, j: (0, j))
    wspec = pl.BlockSpec((2, 1, LANES, LANES), lambda b, j: (0, j, 0, 0))
    return pl.pallas_call(
        functools.partial(_rglru_kernel, n_ctx=lay.n_ctx),
        grid=(lay.bsz, nb),
        in_specs=[pl.BlockSpec((seq, LANES), col(0)), pl.BlockSpec((seq, LANES), col(nb)),
                  vec(CONV_W), vec(1), wspec, vec(2), wspec, vec(2), vec(2)],
        out_specs=pl.BlockSpec((seq, LANES), col(0)),
        out_shape=jax.ShapeDtypeStruct((lay.rows, GROUP_WIDTH), F32),
        scratch_shapes=[pltpu.VMEM((seq + 2 * SUBLANES, LANES), F32), pltpu.VMEM((seq, LANES), F32),
                        pltpu.VMEM((seq, LANES), F32), pltpu.VMEM((seq, LANES), F32)],
        compiler_params=_params(("arbitrary", "arbitrary")),
        name="rglru",
    )(p_a, p_a, conv_w, conv_b.reshape(1, -1), wr, br, wi, bi, lam)


def _gla_kernel(q_ref, k_ref, v_ref, g_ref, z_ref, w2_ref, b2_ref, ng_ref, o_ref,
                laf_ref, lab_ref, of_ref, ob_ref, *, n_ctx):
    seq = q_ref.shape[0]
    cs = GLA_CHUNK
    n_chunks = seq // cs
    ctx_chunks = n_ctx // cs
    z = z_ref[...]
    for d, la_ref in ((0, laf_ref), (1, lab_ref)):
        pre = jnp.dot(z, w2_ref[d], preferred_element_type=F32, precision=HIGHEST) + b2_ref[d:d + 1, :]
        log_sig = jnp.minimum(pre, 0.0) - jnp.log(1.0 + jnp.exp(-jnp.abs(pre)))
        la_ref[...] = log_sig * (1.0 / GLA_TAU)

    ri = lax.broadcasted_iota(I32, (cs, cs), 0)
    ci = lax.broadcasted_iota(I32, (cs, cs), 1)
    scale = GLA_DK ** -0.5

    def chunk(c, states, reverse):
        r0 = pl.multiple_of(c * cs, cs)
        rows = pl.ds(r0, cs)
        keep = (ri <= ci) if reverse else (ri >= ci)
        la = (lab_ref if reverse else laf_ref)[rows, :]
        cum = jnp.dot(keep.astype(F32), la, preferred_element_type=F32, precision=HIGHEST)
        last = cum[0:1, :] if reverse else cum[cs - 1:cs, :]
        q = q_ref[rows, :] * scale
        k = k_ref[rows, :]
        qe = q * jnp.exp(cum)
        ke = k * jnp.exp(-cum)
        kl = k * jnp.exp(last - cum)
        el = jnp.exp(last)
        out_ref = ob_ref if reverse else of_ref
        new_states = []
        for hh in range(2):
            dk = slice(hh * GLA_DK, (hh + 1) * GLA_DK)
            dv = slice(hh * GLA_DV, (hh + 1) * GLA_DV)
            qh = qe[:, dk].astype(BF16)
            vh = v_ref[rows, dv].astype(BF16)
            st = states[hh]
            s = lax.dot_general(qh, ke[:, dk].astype(BF16), NT_DIMS, preferred_element_type=F32)
            s = jnp.where(keep, s, 0.0)
            o = (jnp.dot(s.astype(BF16), vh, preferred_element_type=F32)
                 + lax.dot_general(qh, st.astype(BF16), NT_DIMS, preferred_element_type=F32))
            out_ref[rows, dv] = o
            new_states.append(st * el[:, dk]
                              + lax.dot_general(vh, kl[:, dk].astype(BF16), TN_DIMS, preferred_element_type=F32))
        return tuple(new_states)

    def body(i, carry):
        sf, sb = carry
        sf = chunk(i, sf, False)
        cb = jnp.where(i < ctx_chunks, ctx_chunks - 1 - i, n_chunks - 1 + ctx_chunks - i)
        sb = chunk(cb, sb, True)
        return sf, sb

    zero = jnp.zeros((GLA_DV, GLA_DK), F32)
    lax.fori_loop(0, n_chunks, body, ((zero, zero), (zero, zero)))

    ng = ng_ref[...]
    for hh in range(2):
        dv = slice(hh * GLA_DV, (hh + 1) * GLA_DV)
        o = of_ref[:, dv] + ob_ref[:, dv]
        o_ref[:, dv] = _rms(o, ng) * _silu(g_ref[:, dv])


def _gla(lay, p_b, w2pad, b2, norm_g):
    seq = lay.seq
    hp = GLA_HEADS // 2
    qk_w = 2 * GLA_DK
    v_w = 2 * GLA_DV
    k_off = GLA_HEADS * GLA_DK // qk_w
    v_off = 2 * GLA_HEADS * GLA_DK // v_w
    g_off = v_off + GLA_HEADS * GLA_DV // v_w
    blk = lambda w, off: pl.BlockSpec((seq, w), lambda b, j: (b, off + j))
    return pl.pallas_call(
        functools.partial(_gla_kernel, n_ctx=lay.n_ctx),
        grid=(lay.bsz, hp),
        in_specs=[blk(qk_w, 0), blk(qk_w, k_off), blk(v_w, v_off), blk(v_w, g_off),
                  pl.BlockSpec((seq, LANES), lambda b, j: (b, Z_COL_BLOCK)),
                  pl.BlockSpec((2, LANES, qk_w), lambda b, j: (0, 0, j)),
                  pl.BlockSpec((2, qk_w), lambda b, j: (0, j)),
                  pl.BlockSpec((1, GLA_DV), lambda b, j: (0, 0))],
        out_specs=blk(v_w, 0),
        out_shape=jax.ShapeDtypeStruct((lay.rows, GROUP_WIDTH), F32),
        scratch_shapes=[pltpu.VMEM((seq, qk_w), F32), pltpu.VMEM((seq, qk_w), F32),
                        pltpu.VMEM((seq, v_w), F32), pltpu.VMEM((seq, v_w), F32)],
        compiler_params=_params(("arbitrary", "arbitrary")),
        name="gla",
    )(p_b, p_b, p_b, p_b, p_b, w2pad, b2, norm_g.reshape(1, -1))


def _rope(x, cos, sin_signed):
    lane = lax.broadcasted_iota(I32, x.shape, 1)
    partner = jnp.where((lane & 32) != 0, pltpu.roll(x, 32, 1), pltpu.roll(x, HEAD_DIM - 32, 1))
    return x * cos + partner * sin_signed


def _attn_kernel(*refs, window, need_ctx, tq, n_ctx):
    if window:
        sink_ref, q_ref, k_ref, v_ref, cos_ref, sin_ref, o_ref, ks_ref, vs_ref = refs
        qkg_ref = None
    else:
        q_ref, k_ref, v_ref, cos_ref, sin_ref, qkg_ref, o_ref, ks_ref, vs_ref = refs
    seq = k_ref.shape[0]
    n_lat = seq - n_ctx
    g = pl.program_id(1)
    qi = pl.program_id(2)
    jq = qi if need_ctx else qi + n_ctx // tq

    @pl.when(qi == 0)
    def _():
        k = k_ref[...]
        if not window:
            k = _rms(k, qkg_ref[1:2, :])
        ks_ref[...] = _rope(k, cos_ref[...], sin_ref[...]).astype(BF16)
        vs_ref[...] = v_ref[...].astype(BF16)

    r0 = pl.multiple_of(jq * tq, tq)
    cos = cos_ref[pl.ds(r0, tq), :]
    sin = sin_ref[pl.ds(r0, tq), :]
    heads = []
    for hh in range(2):
        qh = q_ref[:, hh * HEAD_DIM:(hh + 1) * HEAD_DIM]
        if not window:
            qh = _rms(qh, qkg_ref[0:1, :])
        heads.append((_rope(qh, cos, sin) * HEAD_DIM ** -0.5).astype(BF16))
    q2 = jnp.concatenate(heads, axis=0)

    def finish(o):
        o_ref[:, 0:HEAD_DIM] = o[0:tq]
        o_ref[:, HEAD_DIM:2 * HEAD_DIM] = o[tq:2 * tq]

    def scores(lo, n):
        return lax.dot_general(q2, ks_ref[pl.ds(lo, n), :], NT_DIMS, preferred_element_type=F32)

    def pv(p, lo, n):
        return jnp.dot(p.astype(BF16), vs_ref[pl.ds(lo, n), :], preferred_element_type=F32)

    if window:
        rowi = lax.broadcasted_iota(I32, (2 * tq, 1), 0)
        sink = jnp.where(rowi < tq, sink_ref[2 * g], sink_ref[2 * g + 1])

    def ctx_tile():
        s = scores(0, n_ctx)
        m = jnp.max(s, axis=-1, keepdims=True)
        if window:
            m = jnp.maximum(m, sink)
        p = jnp.exp(s - m)
        l = jnp.sum(p, axis=-1, keepdims=True)
        if window:
            l = l + jnp.exp(sink - m)
        finish(pv(p, 0, n_ctx) / l)

    def lat_tile():
        if not window:
            s = scores(0, seq)
            m = jnp.max(s, axis=-1, keepdims=True)
            p = jnp.exp(s - m)
            finish(pv(p, 0, seq) / jnp.sum(p, axis=-1, keepdims=True))
            return
        band = 3 * WINDOW
        jl = jq - n_ctx // tq
        start = jnp.clip((jl - 1) * WINDOW, 0, n_lat - band)
        lo = pl.multiple_of(n_ctx + start, WINDOW)
        sb = scores(lo, band)
        q_pos = jl * WINDOW + lax.broadcasted_iota(I32, (2 * tq, band), 0) % tq
        k_pos = start + lax.broadcasted_iota(I32, (2 * tq, band), 1)
        sb = jnp.where(jnp.abs(q_pos - k_pos) <= WINDOW, sb, NEG_INF)
        sc = scores(0, n_ctx)
        m = jnp.maximum(jnp.maximum(jnp.max(sb, axis=-1, keepdims=True), jnp.max(sc, axis=-1, keepdims=True)), sink)
        pb = jnp.exp(sb - m)
        pc = jnp.exp(sc - m)
        l = jnp.sum(pb, axis=-1, keepdims=True) + jnp.sum(pc, axis=-1, keepdims=True) + jnp.exp(sink - m)
        finish((pv(pb, lo, band) + pv(pc, 0, n_ctx)) / l)

    if need_ctx:
        pl.when(jq * tq < n_ctx)(ctx_tile)
        pl.when(jq * tq >= n_ctx)(lat_tile)
    else:
        lat_tile()


def _attention(lay, p, cos, sin, extra, *, window, need_ctx):
    seq = lay.seq
    tq = WINDOW if window else TM
    q_tiles_b = seq // tq
    nq = q_tiles_b if need_ctx else lay.n_lat // tq
    skip = 0 if need_ctx else lay.n_ctx // tq
    q_heads_w = 2 * HEAD_DIM
    k_off = GROUP_WIDTH // HEAD_DIM
    v_off = k_off + KV_HEADS

    def im(f):
        return (lambda b, g, qi, *_: f(b, g, qi))

    in_specs = [pl.BlockSpec((tq, q_heads_w), im(lambda b, g, qi: (b * q_tiles_b + skip + qi, g))),
                pl.BlockSpec((seq, HEAD_DIM), im(lambda b, g, qi: (b, k_off + g))),
                pl.BlockSpec((seq, HEAD_DIM), im(lambda b, g, qi: (b, v_off + g))),
                pl.BlockSpec((seq, HEAD_DIM), im(lambda b, g, qi: (0, 0))),
                pl.BlockSpec((seq, HEAD_DIM), im(lambda b, g, qi: (0, 0)))]
    args = [p, p, p, cos, sin]
    if not window:
        in_specs.append(pl.BlockSpec((2, HEAD_DIM), im(lambda b, g, qi: (0, 0))))
        args.append(extra)
    out_spec = pl.BlockSpec((tq, q_heads_w), im(lambda b, g, qi: (b * q_tiles_b + skip + qi, g)))
    kern = functools.partial(_attn_kernel, window=window, need_ctx=need_ctx, tq=tq, n_ctx=lay.n_ctx)
    scratch = [pltpu.VMEM((seq, HEAD_DIM), BF16), pltpu.VMEM((seq, HEAD_DIM), BF16)]
    out_shape = jax.ShapeDtypeStruct((lay.rows, GROUP_WIDTH), F32)
    sem = ("arbitrary", "arbitrary", "arbitrary")
    grid = (lay.bsz, KV_HEADS, nq)
    if window:
        return pl.pallas_call(
            kern,
            grid_spec=pltpu.PrefetchScalarGridSpec(num_scalar_prefetch=1, grid=grid, in_specs=in_specs,
                                                   out_specs=out_spec, scratch_shapes=scratch),
            out_shape=out_shape, compiler_params=_params(sem), name="win_attn",
        )(extra, *args)
    return pl.pallas_call(kern, grid=grid, in_specs=in_specs, out_specs=out_spec, scratch_shapes=scratch,
                          out_shape=out_shape, compiler_params=_params(sem), name="glob_attn")(*args)


def _outproj_kernel(ya_ref, yb_ref, yc_ref, yd_ref, w_ref, x_ref, gm_ref, sh_ref, sc_ref, g1_ref, g2_ref,
                    xo_ref, h_ref, hp_ref):
    acc = None
    for i, y_ref in enumerate((ya_ref, yb_ref, yc_ref, yd_ref)):
        part = jnp.dot(y_ref[...].astype(BF16), w_ref[i * GROUP_WIDTH:(i + 1) * GROUP_WIDTH, :],
                       preferred_element_type=F32)
        acc = part if acc is None else acc + part
    x_new = x_ref[...] + gm_ref[0] * _rms(acc, g1_ref[...])
    xo_ref[...] = x_new
    h = _rms(x_new, g2_ref[...]) * (1.0 + sc_ref[0]) + sh_ref[0]
    h_ref[...] = h
    hp_ref[...] = _pack_bf16_pairs(h)


def _outproj(lay, ys, w_out, xt, mods, g1, g2, lat_only):
    row = lambda i: (lay.tile(i, lat_only), 0)
    full = lambda shape: pl.BlockSpec(shape, lambda i: (0, 0))
    yspec = pl.BlockSpec((TM, GROUP_WIDTH), row)
    xspec = pl.BlockSpec((TM, D_MODEL), row)
    return pl.pallas_call(
        _outproj_kernel,
        grid=(lay.n_tiles(lat_only),),
        in_specs=[yspec, yspec, yspec, yspec, full((D_MODEL, D_MODEL)), xspec,
                  _mod_spec(lay, 2, lat_only), _mod_spec(lay, 3, lat_only), _mod_spec(lay, 4, lat_only),
                  full((1, D_MODEL)), full((1, D_MODEL))],
        out_specs=[xspec, xspec, pl.BlockSpec((TM, D_MODEL // 2), row)],
        out_shape=[jax.ShapeDtypeStruct((lay.rows, D_MODEL), F32), jax.ShapeDtypeStruct((lay.rows, D_MODEL), F32),
                   jax.ShapeDtypeStruct((lay.rows, D_MODEL // 2), U32)],
        compiler_params=_params(("arbitrary",)),
        name="outproj",
    )(*ys, w_out, xt, mods, mods, mods, g1, g2)


def _router_kernel(h_ref, wt_ref, bias_ref, e_ref, gate_ref, rank_ref, cnt_ref, carry_ref):
    i = pl.program_id(0)
    tm = h_ref.shape[0]
    ne = N_EXPERTS
    gsz = ne // N_GROUPS

    @pl.when(i == 0)
    def _():
        carry_ref[...] = jnp.zeros_like(carry_ref)

    logits = lax.dot_general(wt_ref[...], h_ref[...], NT_DIMS, preferred_element_type=F32, precision=HIGHEST)
    scores = jax.nn.sigmoid(logits)
    biased = scores + bias_ref[...]

    sub = lax.broadcasted_iota(I32, (gsz, tm), 0)
    gs_rows = []
    for grp in range(N_GROUPS):
        blk = biased[grp * gsz:(grp + 1) * gsz, :]
        m1 = jnp.max(blk, axis=0, keepdims=True)
        first = jnp.min(jnp.where(blk == m1, sub, gsz), axis=0, keepdims=True)
        m2 = jnp.max(jnp.where(sub == first, -jnp.inf, blk), axis=0, keepdims=True)
        gs_rows.append(m1 + m2)
    gs = jnp.concatenate(gs_rows, axis=0)

    gidx = lax.broadcasted_iota(I32, (N_GROUPS, tm), 0)
    chosen = jnp.zeros((N_GROUPS, tm), jnp.bool_)
    cur = gs
    for _ in range(TOPK_GROUPS):
        m = jnp.max(cur, axis=0, keepdims=True)
        first = jnp.min(jnp.where(cur == m, gidx, N_GROUPS), axis=0, keepdims=True)
        sel = gidx == first
        chosen = chosen | sel
        cur = jnp.where(sel, -jnp.inf, cur)

    eidx = lax.broadcasted_iota(I32, (ne, tm), 0)
    eligible = jnp.concatenate(
        [jnp.broadcast_to(chosen[grp:grp + 1, :], (gsz, tm)) for grp in range(N_GROUPS)], axis=0)
    cur = jnp.where(eligible, biased, -jnp.inf)
    sels, gates = [], []
    onehot = jnp.zeros((ne, tm), jnp.bool_)
    for k in range(TOP_K):
        m = jnp.max(cur, axis=0, keepdims=True)
        first = jnp.min(jnp.where(cur == m, eidx, ne), axis=0, keepdims=True)
        sel = eidx == first
        e_ref[k:k + 1, :] = first
        gates.append(jnp.sum(jnp.where(sel, scores, 0.0), axis=0, keepdims=True))
        sels.append(sel)
        onehot = onehot | sel
        cur = jnp.where(sel, -jnp.inf, cur)
    total = gates[0]
    for gk in gates[1:]:
        total = total + gk
    for k in range(TOP_K):
        gate_ref[k:k + 1, :] = gates[k] / total * ROUTED_SCALE

    tr = lax.broadcasted_iota(I32, (tm, tm), 0)
    tc = lax.broadcasted_iota(I32, (tm, tm), 1)
    before = (tr < tc).astype(BF16)
    oh = onehot.astype(F32)
    prefix = jnp.dot(oh.astype(BF16), before, preferred_element_type=F32) + carry_ref[...]
    for k in range(TOP_K):
        rank_ref[k:k + 1, :] = jnp.sum(jnp.where(sels[k], prefix, 0.0), axis=0, keepdims=True).astype(I32)
    carry = carry_ref[...] + jnp.sum(oh, axis=1, keepdims=True)
    carry_ref[...] = carry
    cnt_ref[...] = carry.astype(I32)


def _router(lay, h, router_wt, bias, lat_only):
    col = lambda i: (0, lay.tile(i, lat_only))
    kt = pl.BlockSpec((TOP_K, TM), col)
    return pl.pallas_call(
        _router_kernel,
        grid=(lay.n_tiles(lat_only),),
        in_specs=[pl.BlockSpec((TM, D_MODEL), lambda i: (lay.tile(i, lat_only), 0)),
                  pl.BlockSpec((N_EXPERTS, D_MODEL), lambda i: (0, 0)),
                  pl.BlockSpec((N_EXPERTS, 1), lambda i: (0, 0))],
        out_specs=[kt, kt, kt, pl.BlockSpec((N_EXPERTS, 1), lambda i: (0, 0))],
        out_shape=[jax.ShapeDtypeStruct((TOP_K, lay.rows), I32), jax.ShapeDtypeStruct((TOP_K, lay.rows), F32),
                   jax.ShapeDtypeStruct((TOP_K, lay.rows), I32), jax.ShapeDtypeStruct((N_EXPERTS, 1), I32)],
        scratch_shapes=[pltpu.VMEM((N_EXPERTS, 1), F32)],
        compiler_params=_params(("arbitrary",)),
        name="router",
    )(h, router_wt, bias.reshape(-1, 1))


def _row_copy(src_ref, src_row, dst_ref, dst_row, sem):
    return pltpu.make_async_copy(src_ref.at[pl.ds(src_row, 1)], dst_ref.at[pl.ds(dst_row, 1)], sem)


def _dispatch_kernel(start_ref, e_ref, rank_ref, hp_ref, xs_ref, sem):
    tm = hp_ref.shape[0]

    def issue(t, carry):
        for k in range(TOP_K):
            dest = start_ref[e_ref[k, t]] + rank_ref[k, t]
            _row_copy(hp_ref, t, xs_ref, dest, sem).start()
        return carry

    def drain(t, carry):
        for k in range(TOP_K):
            _row_copy(hp_ref, 0, xs_ref, 0, sem).wait()
        return carry

    lax.fori_loop(0, tm, issue, 0)
    lax.fori_loop(0, tm, drain, 0)


def _dispatch(lay, pad_start, top_e, rank, hp, n_rows, lat_only):
    col = lambda i, s: (0, lay.tile(i, lat_only))
    smem = pl.BlockSpec((TOP_K, TM), col, memory_space=pltpu.SMEM)
    return pl.pallas_call(
        _dispatch_kernel,
        grid_spec=pltpu.PrefetchScalarGridSpec(
            num_scalar_prefetch=1, grid=(lay.n_tiles(lat_only),),
            in_specs=[smem, smem, pl.BlockSpec((TM, D_MODEL // 2), lambda i, s: (lay.tile(i, lat_only), 0))],
            out_specs=pl.BlockSpec(memory_space=pl.ANY),
            scratch_shapes=[pltpu.SemaphoreType.DMA]),
        out_shape=jax.ShapeDtypeStruct((n_rows, D_MODEL // 2), U32),
        compiler_params=_params(("arbitrary",)),
        name="dispatch",
    )(pad_start, top_e, rank, hp)


def _experts_kernel(be_ref, bv_ref, nu_ref, xs_ref, wg_ref, wu_ref, wd_ref, ys_ref, wgb_ref, wub_ref, wdb_ref):
    j = pl.program_id(0)
    live = j < nu_ref[0]
    prev = be_ref[jnp.maximum(j - 1, 0)]

    @pl.when(live & ((j == 0) | (be_ref[j] != prev)))
    def _():
        wgb_ref[...] = wg_ref[0].astype(BF16)
        wub_ref[...] = wu_ref[0].astype(BF16)
        wdb_ref[...] = wd_ref[0].astype(BF16)

    @pl.when(live)
    def _():
        rowi = lax.broadcasted_iota(I32, xs_ref.shape, 0)
        packed = jnp.where(rowi < bv_ref[j], xs_ref[...], jnp.uint32(0))
        x = _unpack_bf16_pairs(packed)
        hid = (_silu(jnp.dot(x, wgb_ref[...], preferred_element_type=F32))
               * jnp.dot(x, wub_ref[...], preferred_element_type=F32))
        ys_ref[...] = jnp.dot(hid.astype(BF16), wdb_ref[...], preferred_element_type=F32)


def _experts(block_e, block_valid, n_used, xs, w_gate, w_up, w_down):
    n_blocks = xs.shape[0] // MOE_BLK
    last = lambda j, nu: jnp.minimum(j, nu[0] - 1)
    wmap = lambda j, be, bv, nu: (be[last(j, nu)], 0, 0)
    return pl.pallas_call(
        _experts_kernel,
        grid_spec=pltpu.PrefetchScalarGridSpec(
            num_scalar_prefetch=3, grid=(n_blocks,),
            in_specs=[pl.BlockSpec((MOE_BLK, D_MODEL // 2), lambda j, be, bv, nu: (last(j, nu), 0)),
                      pl.BlockSpec((1, D_MODEL, EXPERT_HIDDEN), wmap),
                      pl.BlockSpec((1, D_MODEL, EXPERT_HIDDEN), wmap),
                      pl.BlockSpec((1, EXPERT_HIDDEN, D_MODEL), wmap)],
            out_specs=pl.BlockSpec((MOE_BLK, D_MODEL), lambda j, be, bv, nu: (last(j, nu), 0)),
            scratch_shapes=[pltpu.VMEM((D_MODEL, EXPERT_HIDDEN), BF16), pltpu.VMEM((D_MODEL, EXPERT_HIDDEN), BF16),
                            pltpu.VMEM((EXPERT_HIDDEN, D_MODEL), BF16)]),
        out_shape=jax.ShapeDtypeStruct((xs.shape[0], D_MODEL), F32),
        compiler_params=_params(("arbitrary",)),
        name="experts",
    )(block_e, block_valid, n_used, xs, w_gate, w_up, w_down)


def _combine_kernel(start_ref, e_ref, rank_ref, gate_ref, hp_ref, x_ref, gf_ref, g3_ref, sg_ref, su_ref, sd_ref,
                    ys_ref, o_ref, buf_ref, sem):
    tm = hp_ref.shape[0]

    def issue(t, carry):
        for k in range(TOP_K):
            src = start_ref[e_ref[k, t]] + rank_ref[k, t]
            pltpu.make_async_copy(ys_ref.at[pl.ds(src, 1)], buf_ref.at[k, pl.ds(t, 1)], sem).start()
        return carry

    def drain(t, carry):
        for k in range(TOP_K):
            pltpu.make_async_copy(ys_ref.at[pl.ds(0, 1)], buf_ref.at[0, pl.ds(0, 1)], sem).wait()
        return carry

    lax.fori_loop(0, tm, issue, 0)
    hb = _unpack_bf16_pairs(hp_ref[...])
    hid = (_silu(jnp.dot(hb, sg_ref[...], preferred_element_type=F32))
           * jnp.dot(hb, su_ref[...], preferred_element_type=F32))
    f = jnp.dot(hid.astype(BF16), sd_ref[...], preferred_element_type=F32)
    lax.fori_loop(0, tm, drain, 0)
    gate = gate_ref[...]
    for k in range(TOP_K):
        f = f + gate[:, k:k + 1] * buf_ref[k]
    o_ref[...] = x_ref[...] + gf_ref[0] * _rms(f, g3_ref[...])


def _combine(lay, pad_start, top_e, rank, gate_t, hp, x_mid, mods, g3, s_gate, s_up, s_down, ys, lat_only,
             compact_out):
    col = lambda i, s: (0, lay.tile(i, lat_only))
    row = lambda i, s: (lay.tile(i, lat_only), 0)
    smem = pl.BlockSpec((TOP_K, TM), col, memory_space=pltpu.SMEM)
    full = lambda shape: pl.BlockSpec(shape, lambda i, s: (0, 0))
    mod = pl.BlockSpec((1, 1, D_MODEL), lambda i, s: (lay.mod_row(i, lat_only) * 6 + 5, 0, 0))
    out_rows = lay.n_tiles(lat_only) * TM if compact_out else lay.rows
    out_map = (lambda i, s: (i, 0)) if compact_out else row
    return pl.pallas_call(
        _combine_kernel,
        grid_spec=pltpu.PrefetchScalarGridSpec(
            num_scalar_prefetch=1, grid=(lay.n_tiles(lat_only),),
            in_specs=[smem, smem, pl.BlockSpec((TM, TOP_K), row),
                      pl.BlockSpec((TM, D_MODEL // 2), row), pl.BlockSpec((TM, D_MODEL), row), mod,
                      full((1, D_MODEL)), full((D_MODEL, EXPERT_HIDDEN)), full((D_MODEL, EXPERT_HIDDEN)),
                      full((EXPERT_HIDDEN, D_MODEL)), pl.BlockSpec(memory_space=pl.ANY)],
            out_specs=pl.BlockSpec((TM, D_MODEL), out_map),
            scratch_shapes=[pltpu.VMEM((TOP_K, TM, D_MODEL), F32), pltpu.SemaphoreType.DMA]),
        out_shape=jax.ShapeDtypeStruct((out_rows, D_MODEL), F32),
        compiler_params=_params(("arbitrary",)),
        name="combine",
    )(pad_start, top_e, rank, gate_t, hp, x_mid, mods, g3, s_gate, s_up, s_down, ys)


def _rope_tables(lay):
    n_freq = HEAD_DIM // 4
    inv_freq = ROPE_THETA ** (-jnp.arange(n_freq, dtype=F32) / n_freq)
    t = jnp.arange(lay.n_lat)
    pos = jnp.stack([(t // GRID_W).astype(F32), (t % GRID_W).astype(F32)], axis=-1)
    ang = pos[:, :, None, None] * inv_freq
    ang = jnp.broadcast_to(ang, (lay.n_lat, 2, 2, n_freq)).reshape(lay.n_lat, HEAD_DIM)
    sign = jnp.tile(jnp.concatenate([-jnp.ones(n_freq, F32), jnp.ones(n_freq, F32)]), 2)
    cos = jnp.concatenate([jnp.ones((lay.n_ctx, HEAD_DIM), F32), jnp.cos(ang)], axis=0)
    sin = jnp.concatenate([jnp.zeros((lay.n_ctx, HEAD_DIM), F32), jnp.sin(ang) * sign], axis=0)
    return cos, sin


def _moe_plan(counts, n_blocks):
    counts = counts.reshape(-1)
    padded = (counts + MOE_BLK - 1) // MOE_BLK * MOE_BLK
    pad_end = jnp.cumsum(padded)
    pad_start = (pad_end - padded).astype(I32)
    blk_row = jnp.arange(n_blocks, dtype=I32) * MOE_BLK
    block_e = jnp.minimum(jnp.searchsorted(pad_end, blk_row, side='right'), N_EXPERTS - 1).astype(I32)
    block_valid = jnp.clip(counts[block_e] - (blk_row - pad_start[block_e]), 0, MOE_BLK).astype(I32)
    n_used = jnp.maximum(pad_end[-1:] // MOE_BLK, 1).astype(I32)
    return pad_start, block_e, block_valid, n_used


def kernel(x, c, ctx, c_ctx, mod_w, mod_b, norm_g, w_in, w_out, conv_w, conv_b, lru_wr, lru_br, lru_wi, lru_bi,
           lru_lambda, gla_w2, gla_b2, gla_norm_g, qk_norm_g, win_sink, router_w, router_bias, exp_w_gate,
           exp_w_up, exp_w_down, sh_w_gate, sh_w_up, sh_w_down):
    bsz, n_lat, d = x.shape
    n_ctx = ctx.shape[1]
    depth = mod_w.shape[0]
    assert d == D_MODEL and bsz < MOD_ROWS
    lay = _Layout(bsz, n_lat, n_ctx)

    cc = jnp.zeros((MOD_ROWS, d), F32).at[:bsz].set(c).at[bsz].set(c_ctx)
    mods_all = _adaln(cc, mod_w, mod_b)
    cos, sin = _rope_tables(lay)
    xt = jnp.concatenate([ctx, x], axis=1).reshape(lay.rows, d)

    splits = np.cumsum([0, W_A, W_B_RAW, W_C, W_D])
    for layer in range(depth):
        last = layer == depth - 1
        need_ctx = not last
        lat_only = last
        mods = mods_all[layer].reshape(MOD_ROWS * 6, 1, d)
        g = norm_g[layer]
        wl = w_in[layer]
        w_cat = jnp.concatenate(
            [wl[:, splits[0]:splits[1]], wl[:, splits[1]:splits[2]], jnp.zeros((d, W_B - W_B_RAW), F32),
             wl[:, splits[2]:splits[3]], wl[:, splits[3]:splits[4]]], axis=1).astype(BF16)
        p_a, p_b, p_c, p_d = _inproj(lay, xt, mods, g[0:1], w_cat)

        y_a = _rglru(lay, p_a, conv_w[layer], conv_b[layer], lru_wr[layer], lru_br[layer], lru_wi[layer],
                     lru_bi[layer], lru_lambda[layer])
        w2pad = jnp.zeros((2, LANES, GLA_HEADS * GLA_DK), F32)
        w2pad = w2pad.at[0, :GLA_RANK].set(gla_w2[layer, 0]).at[1, GLA_RANK:2 * GLA_RANK].set(gla_w2[layer, 1])
        y_b = _gla(lay, p_b, w2pad, gla_b2[layer], gla_norm_g[layer])
        y_c = _attention(lay, p_c, cos, sin, qk_norm_g[layer], window=False, need_ctx=need_ctx)
        y_d = _attention(lay, p_d, cos, sin, win_sink[layer], window=True, need_ctx=need_ctx)

        x_mid, h, hp = _outproj(lay, (y_a, y_b, y_c, y_d), w_out[layer].astype(BF16), xt, mods, g[1:2], g[2:3],
                                lat_only)

        top_e, gate, rank, counts = _router(lay, h, router_w[layer].T, router_bias[layer], lat_only)
        n_assign = lay.n_tiles(lat_only) * TM * TOP_K
        n_blocks = -(-(n_assign + N_EXPERTS * (MOE_BLK - 1)) // MOE_BLK)
        pad_start, block_e, block_valid, n_used = _moe_plan(counts, n_blocks)
        xs = _dispatch(lay, pad_start, top_e, rank, hp, n_blocks * MOE_BLK, lat_only)
        ys = _experts(block_e, block_valid, n_used, xs, exp_w_gate[layer], exp_w_up[layer], exp_w_down[layer])
        xt = _combine(lay, pad_start, top_e, rank, gate.T, hp, x_mid, mods, g[3:4], sh_w_gate[layer].astype(BF16),
                      sh_w_up[layer].astype(BF16), sh_w_down[layer].astype(BF16), ys, lat_only, compact_out=last)
    return xt.reshape(bsz, n_lat, d)
```

```python
import functools

import numpy as np
import jax
import jax.numpy as jnp
from jax import lax
from jax.experimental import pallas as pl
from jax.experimental.pallas import tpu as pltpu

F32 = jnp.float32
BF16 = jnp.bfloat16
I32 = jnp.int32
U32 = jnp.uint32
HIGHEST = lax.Precision.HIGHEST
NT_DIMS = (((1,), (1,)), ((), ()))
TN_DIMS = (((0,), (0,)), ((), ()))

LANES = 128
SUBLANES = 8
VMEM_LIMIT_BYTES = 56 * 1024 * 1024

D_MODEL = 2048
GRID_W = 64
HEAD_DIM = 128
GROUP_WIDTH = D_MODEL // 4
LRU_BLOCKS = 4
CONV_W = 4
RGLRU_C = 8.0
GLA_HEADS = 4
GLA_DK = 64
GLA_DV = 128
GLA_RANK = 16
GLA_TAU = 16.0
GLA_CHUNK = 64
KV_HEADS = 2
WINDOW = 128
N_EXPERTS = 64
N_GROUPS = 8
TOPK_GROUPS = 4
TOP_K = 8
EXPERT_HIDDEN = 512
ROUTED_SCALE = 2.5
RMS_EPS = 1e-6
ROPE_THETA = 10000.0
NEG_INF = -1e30

W_A = 2 * GROUP_WIDTH
W_B_RAW = 2 * GLA_HEADS * GLA_DK + 2 * GLA_HEADS * GLA_DV + 2 * GLA_RANK
W_B = 1664
W_C = GROUP_WIDTH + 2 * KV_HEADS * HEAD_DIM
W_D = W_C
Z_COL_BLOCK = (2 * GLA_HEADS * GLA_DK + 2 * GLA_HEADS * GLA_DV) // LANES

TM = 256
MOE_BLK = 512
MOD_ROWS = 16


def _params(sem, vmem=VMEM_LIMIT_BYTES):
    return pltpu.CompilerParams(dimension_semantics=sem, vmem_limit_bytes=vmem)


def _rms(x, g):
    return x * lax.rsqrt(jnp.mean(x * x, axis=-1, keepdims=True) + RMS_EPS) * g


def _silu(x):
    return x * jax.nn.sigmoid(x)


def _pack_bf16_pairs(h):
    n = h.shape[1] // 2
    hb = h.astype(BF16).astype(F32)
    hi = lax.bitcast_convert_type(hb[:, :n], U32)
    lo = lax.bitcast_convert_type(hb[:, n:], U32)
    return hi | (lo >> 16)


def _unpack_bf16_pairs(p):
    hi = lax.bitcast_convert_type(p & jnp.uint32(0xFFFF0000), F32)
    lo = lax.bitcast_convert_type(p << 16, F32)
    return jnp.concatenate([hi, lo], axis=1).astype(BF16)


def _adaln_kernel(c_ref, w_ref, b_ref, o_ref):
    a = _silu(c_ref[...])
    o_ref[0] = jnp.dot(a, w_ref[0], preferred_element_type=F32, precision=HIGHEST) + b_ref[0]


def _adaln(cc, mod_w, mod_b):
    depth, d, n = mod_w.shape
    tn = 1024
    return pl.pallas_call(
        _adaln_kernel,
        grid=(depth, n // tn),
        in_specs=[pl.BlockSpec((MOD_ROWS, d), lambda l, j: (0, 0)),
                  pl.BlockSpec((1, d, tn), lambda l, j: (l, 0, j)),
                  pl.BlockSpec((1, 1, tn), lambda l, j: (l, 0, j))],
        out_specs=pl.BlockSpec((1, MOD_ROWS, tn), lambda l, j: (l, 0, j)),
        out_shape=jax.ShapeDtypeStruct((depth, MOD_ROWS, n), F32),
        compiler_params=_params(("arbitrary", "arbitrary")),
        name="adaln",
    )(cc, mod_w, mod_b.reshape(depth, 1, n))


class _Layout:
    def __init__(self, bsz, n_lat, n_ctx):
        self.bsz, self.n_lat, self.n_ctx = bsz, n_lat, n_ctx
        self.seq = n_ctx + n_lat
        self.rows = bsz * self.seq
        assert n_ctx % TM == 0 and n_lat % TM == 0
        self.tiles_b = self.seq // TM
        self.ctx_tiles = n_ctx // TM
        self.lat_tiles = n_lat // TM

    def n_tiles(self, lat_only):
        return self.bsz * (self.lat_tiles if lat_only else self.tiles_b)

    def tile(self, i, lat_only):
        if lat_only:
            return (i // self.lat_tiles) * self.tiles_b + self.ctx_tiles + i % self.lat_tiles
        return i

    def mod_row(self, i, lat_only):
        if lat_only:
            return i // self.lat_tiles
        return jnp.where(i % self.tiles_b < self.ctx_tiles, self.bsz, i // self.tiles_b)


def _mod_spec(lay, chunk, lat_only):
    return pl.BlockSpec((1, 1, D_MODEL), lambda i: (lay.mod_row(i, lat_only) * 6 + chunk, 0, 0))


def _inproj_kernel(x_ref, sh_ref, sc_ref, g_ref, w_ref, oa_ref, ob_ref, oc_ref, od_ref):
    h = _rms(x_ref[...], g_ref[...]) * (1.0 + sc_ref[0]) + sh_ref[0]
    hb = h.astype(BF16)
    c0 = 0
    for o_ref in (oa_ref, ob_ref, oc_ref, od_ref):
        width = o_ref.shape[1]
        o_ref[...] = jnp.dot(hb, w_ref[:, c0:c0 + width], preferred_element_type=F32)
        c0 += width


def _inproj(lay, xt, mods, g0, w_cat):
    widths = (W_A, W_B, W_C, W_D)
    row = lambda i: (i, 0)
    return pl.pallas_call(
        _inproj_kernel,
        grid=(lay.n_tiles(False),),
        in_specs=[pl.BlockSpec((TM, D_MODEL), row),
                  _mod_spec(lay, 0, False), _mod_spec(lay, 1, False),
                  pl.BlockSpec((1, D_MODEL), lambda i: (0, 0)),
                  pl.BlockSpec((D_MODEL, sum(widths)), lambda i: (0, 0), pipeline_mode=pl.Buffered(1))],
        out_specs=[pl.BlockSpec((TM, w), row) for w in widths],
        out_shape=[jax.ShapeDtypeStruct((lay.rows, w), F32) for w in widths],
        compiler_params=_params(("arbitrary",)),
        name="inproj",
    )(xt, mods, mods, g0, w_cat)


def _rglru_kernel(x_ref, g_ref, cw_ref, cb_ref, wr_ref, br_ref, wi_ref, bi_ref, lam_ref, o_ref,
                  xs_ref, a_ref, b_ref, h_ref, *, n_ctx):
    seq = x_ref.shape[0]
    pad = SUBLANES
    n_tiles = seq // SUBLANES
    ctx_tiles = n_ctx // SUBLANES
    row = lax.broadcasted_iota(I32, (seq, LANES), 0)

    xs_ref[0:pad, :] = jnp.zeros((pad, LANES), F32)
    xs_ref[pad + seq:2 * pad + seq, :] = jnp.zeros((pad, LANES), F32)
    xs_ref[pad:pad + seq, :] = x_ref[...]
    u = jnp.zeros((seq, LANES), F32) + cb_ref[...]
    for j in range(CONV_W):
        off = j - 2
        xsh = xs_ref[pad + off:pad + off + seq, :]
        if off < 0:
            xsh = jnp.where((row >= n_ctx) & (row + off < n_ctx), 0.0, xsh)
        elif off > 0:
            xsh = jnp.where((row < n_ctx) & (row + off >= n_ctx), 0.0, xsh)
        u = u + cw_ref[j:j + 1, :] * xsh

    ub = u.astype(BF16)
    rmod = row & (SUBLANES - 1)
    for d in range(2):
        reverse = d == 1
        r = jax.nn.sigmoid(jnp.dot(ub, wr_ref[d, 0].astype(BF16), preferred_element_type=F32) + br_ref[d:d + 1, :])
        gi = jax.nn.sigmoid(jnp.dot(ub, wi_ref[d, 0].astype(BF16), preferred_element_type=F32) + bi_ref[d:d + 1, :])
        lam = lam_ref[d:d + 1, :]
        softplus_neg = jnp.maximum(-lam, 0.0) + jnp.log(1.0 + jnp.exp(-jnp.abs(lam)))
        a = jnp.exp(-RGLRU_C * r * softplus_neg)
        b = jnp.sqrt(1.0 - a * a) * (gi * u)
        for s in (1, 2, 4):
            if reverse:
                ap, bp, ok = pltpu.roll(a, seq - s, 0), pltpu.roll(b, seq - s, 0), rmod < SUBLANES - s
            else:
                ap, bp, ok = pltpu.roll(a, s, 0), pltpu.roll(b, s, 0), rmod >= s
            b = jnp.where(ok, a * bp + b, b)
            a = jnp.where(ok, a * ap, a)
        a_ref[...] = a
        b_ref[...] = b

        def step(t, h, reverse=reverse):
            r0 = pl.multiple_of(t * SUBLANES, SUBLANES)
            h8 = a_ref[pl.ds(r0, SUBLANES), :] * h + b_ref[pl.ds(r0, SUBLANES), :]
            if reverse:
                h_ref[pl.ds(r0, SUBLANES), :] = h_ref[pl.ds(r0, SUBLANES), :] + h8
                return jnp.broadcast_to(h8[0:1, :], (SUBLANES, LANES))
            h_ref[pl.ds(r0, SUBLANES), :] = h8
            return jnp.broadcast_to(h8[SUBLANES - 1:SUBLANES, :], (SUBLANES, LANES))

        h0 = jnp.zeros((SUBLANES, LANES), F32)
        if reverse:
            hc = lax.fori_loop(0, ctx_tiles, lambda i, h: step(ctx_tiles - 1 - i, h), h0)
            lax.fori_loop(0, n_tiles - ctx_tiles, lambda i, h: step(n_tiles - 1 - i, h), hc)
        else:
            lax.fori_loop(0, n_tiles, step, h0)

    g = g_ref[...]
    gelu = 0.5 * g * (1.0 + jnp.tanh(np.sqrt(2.0 / np.pi) * (g + 0.044715 * (g * g * g))))
    o_ref[...] = h_ref[...] * gelu


def _rglru(lay, p_a, conv_w, conv_b, wr, br, wi, bi, lam):
    seq = lay.seq
    nb = LRU_BLOCKS
    col = lambda off: (lambda b, j: (b, off + j))
    vec = lambda rows: pl.BlockSpec((rows, LANES), lambda b, j: (0, j))
    wspec = pl.BlockSpec((2, 1, LANES, LANES), lambda b, j: (0, j, 0, 0))
    return pl.pallas_call(
        functools.partial(_rglru_kernel, n_ctx=lay.n_ctx),
        grid=(lay.bsz, nb),
        in_specs=[pl.BlockSpec((seq, LANES), col(0)), pl.BlockSpec((seq, LANES), col(nb)),
                  vec(CONV_W), vec(1), wspec, vec(2), wspec, vec(2), vec(2)],
        out_specs=pl.BlockSpec((seq, LANES), col(0)),
        out_shape=jax.ShapeDtypeStruct((lay.rows, GROUP_WIDTH), F32),
        scratch_shapes=[pltpu.VMEM((seq + 2 * SUBLANES, LANES), F32), pltpu.VMEM((seq, LANES), F32),
                        pltpu.VMEM((seq, LANES), F32), pltpu.VMEM((seq, LANES), F32)],
        compiler_params=_params(("arbitrary", "arbitrary")),
        name="rglru",
    )(p_a, p_a, conv_w, conv_b.reshape(1, -1), wr, br, wi, bi, lam)


def _gla_kernel(q_ref, k_ref, v_ref, g_ref, z_ref, w2_ref, b2_ref, ng_ref, o_ref,
                laf_ref, lab_ref, of_ref, ob_ref, *, n_ctx):
    seq = q_ref.shape[0]
    cs = GLA_CHUNK
    n_chunks = seq // cs
    ctx_chunks = n_ctx // cs
    z = z_ref[...]
    for d, la_ref in ((0, laf_ref), (1, lab_ref)):
        pre = jnp.dot(z, w2_ref[d], preferred_element_type=F32, precision=HIGHEST) + b2_ref[d:d + 1, :]
        log_sig = jnp.minimum(pre, 0.0) - jnp.log(1.0 + jnp.exp(-jnp.abs(pre)))
        la_ref[...] = log_sig * (1.0 / GLA_TAU)

    ri = lax.broadcasted_iota(I32, (cs, cs), 0)
    ci = lax.broadcasted_iota(I32, (cs, cs), 1)
    scale = GLA_DK ** -0.5

    def chunk(c, states, reverse):
        r0 = pl.multiple_of(c * cs, cs)
        rows = pl.ds(r0, cs)
        keep = (ri <= ci) if reverse else (ri >= ci)
        la = (lab_ref if reverse else laf_ref)[rows, :]
        cum = jnp.dot(keep.astype(F32), la, preferred_element_type=F32, precision=HIGHEST)
        last = cum[0:1, :] if reverse else cum[cs - 1:cs, :]
        q = q_ref[rows, :] * scale
        k = k_ref[rows, :]
        qe = q * jnp.exp(cum)
        ke = k * jnp.exp(-cum)
        kl = k * jnp.exp(last - cum)
        el = jnp.exp(last)
        out_ref = ob_ref if reverse else of_ref
        new_states = []
        for hh in range(2):
            dk = slice(hh * GLA_DK, (hh + 1) * GLA_DK)
            dv = slice(hh * GLA_DV, (hh + 1) * GLA_DV)
            qh = qe[:, dk].astype(BF16)
            vh = v_ref[rows, dv].astype(BF16)
            st = states[hh]
            s = lax.dot_general(qh, ke[:, dk].astype(BF16), NT_DIMS, preferred_element_type=F32)
            s = jnp.where(keep, s, 0.0)
            o = (jnp.dot(s.astype(BF16), vh, preferred_element_type=F32)
                 + lax.dot_general(qh, st.astype(BF16), NT_DIMS, preferred_element_type=F32))
            out_ref[rows, dv] = o
            new_states.append(st * el[:, dk]
                              + lax.dot_general(vh, kl[:, dk].astype(BF16), TN_DIMS, preferred_element_type=F32))
        return tuple(new_states)

    def body(i, carry):
        sf, sb = carry
        sf = chunk(i, sf, False)
        cb = jnp.where(i < ctx_chunks, ctx_chunks - 1 - i, n_chunks - 1 + ctx_chunks - i)
        sb = chunk(cb, sb, True)
        return sf, sb

    zero = jnp.zeros((GLA_DV, GLA_DK), F32)
    lax.fori_loop(0, n_chunks, body, ((zero, zero), (zero, zero)))

    ng = ng_ref[...]
    for hh in range(2):
        dv = slice(hh * GLA_DV, (hh + 1) * GLA_DV)
        o = of_ref[:, dv] + ob_ref[:, dv]
        o_ref[:, dv] = _rms(o, ng) * _silu(g_ref[:, dv])


def _gla(lay, p_b, w2pad, b2, norm_g):
    seq = lay.seq
    hp = GLA_HEADS // 2
    qk_w = 2 * GLA_DK
    v_w = 2 * GLA_DV
    k_off = GLA_HEADS * GLA_DK // qk_w
    v_off = 2 * GLA_HEADS * GLA_DK // v_w
    g_off = v_off + GLA_HEADS * GLA_DV // v_w
    blk = lambda w, off: pl.BlockSpec((seq, w), lambda b, j: (b, off + j))
    return pl.pallas_call(
        functools.partial(_gla_kernel, n_ctx=lay.n_ctx),
        grid=(lay.bsz, hp),
        in_specs=[blk(qk_w, 0), blk(qk_w, k_off), blk(v_w, v_off), blk(v_w, g_off),
                  pl.BlockSpec((seq, LANES), lambda b, j: (b, Z_COL_BLOCK)),
                  pl.BlockSpec((2, LANES, qk_w), lambda b, j: (0, 0, j)),
                  pl.BlockSpec((2, qk_w), lambda b, j: (0, j)),
                  pl.BlockSpec((1, GLA_DV), lambda b, j: (0, 0))],
        out_specs=blk(v_w, 0),
        out_shape=jax.ShapeDtypeStruct((lay.rows, GROUP_WIDTH), F32),
        scratch_shapes=[pltpu.VMEM((seq, qk_w), F32), pltpu.VMEM((seq, qk_w), F32),
                        pltpu.VMEM((seq, v_w), F32), pltpu.VMEM((seq, v_w), F32)],
        compiler_params=_params(("arbitrary", "arbitrary")),
        name="gla",
    )(p_b, p_b, p_b, p_b, p_b, w2pad, b2, norm_g.reshape(1, -1))


def _rope(x, cos, sin_signed):
    lane = lax.broadcasted_iota(I32, x.shape, 1)
    partner = jnp.where((lane & 32) != 0, pltpu.roll(x, 32, 1), pltpu.roll(x, HEAD_DIM - 32, 1))
    return x * cos + partner * sin_signed


def _attn_kernel(*refs, window, need_ctx, tq, n_ctx):
    if window:
        sink_ref, q_ref, k_ref, v_ref, cos_ref, sin_ref, o_ref, ks_ref, vs_ref = refs
        qkg_ref = None
    else:
        q_ref, k_ref, v_ref, cos_ref, sin_ref, qkg_ref, o_ref, ks_ref, vs_ref = refs
    seq = k_ref.shape[0]
    n_lat = seq - n_ctx
    g = pl.program_id(1)
    qi = pl.program_id(2)
    jq = qi if need_ctx else qi + n_ctx // tq

    @pl.when(qi == 0)
    def _():
        k = k_ref[...]
        if not window:
            k = _rms(k, qkg_ref[1:2, :])
        ks_ref[...] = _rope(k, cos_ref[...], sin_ref[...]).astype(BF16)
        vs_ref[...] = v_ref[...].astype(BF16)

    r0 = pl.multiple_of(jq * tq, tq)
    cos = cos_ref[pl.ds(r0, tq), :]
    sin = sin_ref[pl.ds(r0, tq), :]
    heads = []
    for hh in range(2):
        qh = q_ref[:, hh * HEAD_DIM:(hh + 1) * HEAD_DIM]
        if not window:
            qh = _rms(qh, qkg_ref[0:1, :])
        heads.append((_rope(qh, cos, sin) * HEAD_DIM ** -0.5).astype(BF16))
    q2 = jnp.concatenate(heads, axis=0)

    def finish(o):
        o_ref[:, 0:HEAD_DIM] = o[0:tq]
        o_ref[:, HEAD_DIM:2 * HEAD_DIM] = o[tq:2 * tq]

    def scores(lo, n):
        return lax.dot_general(q2, ks_ref[pl.ds(lo, n), :], NT_DIMS, preferred_element_type=F32)

    def pv(p, lo, n):
        return jnp.dot(p.astype(BF16), vs_ref[pl.ds(lo, n), :], preferred_element_type=F32)

    if window:
        rowi = lax.broadcasted_iota(I32, (2 * tq, 1), 0)
        sink = jnp.where(rowi < tq, sink_ref[2 * g], sink_ref[2 * g + 1])

    def ctx_tile():
        s = scores(0, n_ctx)
        m = jnp.max(s, axis=-1, keepdims=True)
        if window:
            m = jnp.maximum(m, sink)
        p = jnp.exp(s - m)
        l = jnp.sum(p, axis=-1, keepdims=True)
        if window:
            l = l + jnp.exp(sink - m)
        finish(pv(p, 0, n_ctx) / l)

    def lat_tile():
        if not window:
            s = scores(0, seq)
            m = jnp.max(s, axis=-1, keepdims=True)
            p = jnp.exp(s - m)
            finish(pv(p, 0, seq) / jnp.sum(p, axis=-1, keepdims=True))
            return
        band = 3 * WINDOW
        jl = jq - n_ctx // tq
        start = jnp.clip((jl - 1) * WINDOW, 0, n_lat - band)
        lo = pl.multiple_of(n_ctx + start, WINDOW)
        sb = scores(lo, band)
        q_pos = jl * WINDOW + lax.broadcasted_iota(I32, (2 * tq, band), 0) % tq
        k_pos = start + lax.broadcasted_iota(I32, (2 * tq, band), 1)
        sb = jnp.where(jnp.abs(q_pos - k_pos) <= WINDOW, sb, NEG_INF)
        sc = scores(0, n_ctx)
        m = jnp.maximum(jnp.maximum(jnp.max(sb, axis=-1, keepdims=True), jnp.max(sc, axis=-1, keepdims=True)), sink)
        pb = jnp.exp(sb - m)
        pc = jnp.exp(sc - m)
        l = jnp.sum(pb, axis=-1, keepdims=True) + jnp.sum(pc, axis=-1, keepdims=True) + jnp.exp(sink - m)
        finish((pv(pb, lo, band) + pv(pc, 0, n_ctx)) / l)

    if need_ctx:
        pl.when(jq * tq < n_ctx)(ctx_tile)
        pl.when(jq * tq >= n_ctx)(lat_tile)
    else:
        lat_tile()


def _attention(lay, p, cos, sin, extra, *, window, need_ctx):
    seq = lay.seq
    tq = WINDOW if window else TM
    q_tiles_b = seq // tq
    nq = q_tiles_b if need_ctx else lay.n_lat // tq
    skip = 0 if need_ctx else lay.n_ctx // tq
    q_heads_w = 2 * HEAD_DIM
    k_off = GROUP_WIDTH // HEAD_DIM
    v_off = k_off + KV_HEADS

    def im(f):
        return (lambda b, g, qi, *_: f(b, g, qi))

    in_specs = [pl.BlockSpec((tq, q_heads_w), im(lambda b, g, qi: (b * q_tiles_b + skip + qi, g))),
                pl.BlockSpec((seq, HEAD_DIM), im(lambda b, g, qi: (b, k_off + g))),
                pl.BlockSpec((seq, HEAD_DIM), im(lambda b, g, qi: (b, v_off + g))),
                pl.BlockSpec((seq, HEAD_DIM), im(lambda b, g, qi: (0, 0))),
                pl.BlockSpec((seq, HEAD_DIM), im(lambda b, g, qi: (0, 0)))]
    args = [p, p, p, cos, sin]
    if not window:
        in_specs.append(pl.BlockSpec((2, HEAD_DIM), im(lambda b, g, qi: (0, 0))))
        args.append(extra)
    out_spec = pl.BlockSpec((tq, q_heads_w), im(lambda b, g, qi: (b * q_tiles_b + skip + qi, g)))
    kern = functools.partial(_attn_kernel, window=window, need_ctx=need_ctx, tq=tq, n_ctx=lay.n_ctx)
    scratch = [pltpu.VMEM((seq, HEAD_DIM), BF16), pltpu.VMEM((seq, HEAD_DIM), BF16)]
    out_shape = jax.ShapeDtypeStruct((lay.rows, GROUP_WIDTH), F32)
    sem = ("arbitrary", "arbitrary", "arbitrary")
    grid = (lay.bsz, KV_HEADS, nq)
    if window:
        return pl.pallas_call(
            kern,
            grid_spec=pltpu.PrefetchScalarGridSpec(num_scalar_prefetch=1, grid=grid, in_specs=in_specs,
                                                   out_specs=out_spec, scratch_shapes=scratch),
            out_shape=out_shape, compiler_params=_params(sem), name="win_attn",
        )(extra, *args)
    return pl.pallas_call(kern, grid=grid, in_specs=in_specs, out_specs=out_spec, scratch_shapes=scratch,
                          out_shape=out_shape, compiler_params=_params(sem), name="glob_attn")(*args)


def _outproj_kernel(ya_ref, yb_ref, yc_ref, yd_ref, w_ref, x_ref, gm_ref, sh_ref, sc_ref, g1_ref, g2_ref,
                    xo_ref, h_ref, hp_ref):
    acc = None
    for i, y_ref in enumerate((ya_ref, yb_ref, yc_ref, yd_ref)):
        part = jnp.dot(y_ref[...].astype(BF16), w_ref[i * GROUP_WIDTH:(i + 1) * GROUP_WIDTH, :],
                       preferred_element_type=F32)
        acc = part if acc is None else acc + part
    x_new = x_ref[...] + gm_ref[0] * _rms(acc, g1_ref[...])
    xo_ref[...] = x_new
    h = _rms(x_new, g2_ref[...]) * (1.0 + sc_ref[0]) + sh_ref[0]
    h_ref[...] = h
    hp_ref[...] = _pack_bf16_pairs(h)


def _outproj(lay, ys, w_out, xt, mods, g1, g2, lat_only):
    row = lambda i: (lay.tile(i, lat_only), 0)
    full = lambda shape: pl.BlockSpec(shape, lambda i: (0, 0))
    yspec = pl.BlockSpec((TM, GROUP_WIDTH), row)
    xspec = pl.BlockSpec((TM, D_MODEL), row)
    return pl.pallas_call(
        _outproj_kernel,
        grid=(lay.n_tiles(lat_only),),
        in_specs=[yspec, yspec, yspec, yspec, full((D_MODEL, D_MODEL)), xspec,
                  _mod_spec(lay, 2, lat_only), _mod_spec(lay, 3, lat_only), _mod_spec(lay, 4, lat_only),
                  full((1, D_MODEL)), full((1, D_MODEL))],
        out_specs=[xspec, xspec, pl.BlockSpec((TM, D_MODEL // 2), row)],
        out_shape=[jax.ShapeDtypeStruct((lay.rows, D_MODEL), F32), jax.ShapeDtypeStruct((lay.rows, D_MODEL), F32),
                   jax.ShapeDtypeStruct((lay.rows, D_MODEL // 2), U32)],
        compiler_params=_params(("arbitrary",)),
        name="outproj",
    )(*ys, w_out, xt, mods, mods, mods, g1, g2)


def _router_kernel(h_ref, wt_ref, bias_ref, e_ref, gate_ref, rank_ref, cnt_ref, carry_ref):
    i = pl.program_id(0)
    tm = h_ref.shape[0]
    ne = N_EXPERTS
    gsz = ne // N_GROUPS

    @pl.when(i == 0)
    def _():
        carry_ref[...] = jnp.zeros_like(carry_ref)

    logits = lax.dot_general(wt_ref[...], h_ref[...], NT_DIMS, preferred_element_type=F32, precision=HIGHEST)
    scores = jax.nn.sigmoid(logits)
    biased = scores + bias_ref[...]

    sub = lax.broadcasted_iota(I32, (gsz, tm), 0)
    gs_rows = []
    for grp in range(N_GROUPS):
        blk = biased[grp * gsz:(grp + 1) * gsz, :]
        m1 = jnp.max(blk, axis=0, keepdims=True)
        first = jnp.min(jnp.where(blk == m1, sub, gsz), axis=0, keepdims=True)
        m2 = jnp.max(jnp.where(sub == first, -jnp.inf, blk), axis=0, keepdims=True)
        gs_rows.append(m1 + m2)
    gs = jnp.concatenate(gs_rows, axis=0)

    gidx = lax.broadcasted_iota(I32, (N_GROUPS, tm), 0)
    chosen = jnp.zeros((N_GROUPS, tm), jnp.bool_)
    cur = gs
    for _ in range(TOPK_GROUPS):
        m = jnp.max(cur, axis=0, keepdims=True)
        first = jnp.min(jnp.where(cur == m, gidx, N_GROUPS), axis=0, keepdims=True)
        sel = gidx == first
        chosen = chosen | sel
        cur = jnp.where(sel, -jnp.inf, cur)

    eidx = lax.broadcasted_iota(I32, (ne, tm), 0)
    eligible = jnp.concatenate(
        [jnp.broadcast_to(chosen[grp:grp + 1, :], (gsz, tm)) for grp in range(N_GROUPS)], axis=0)
    cur = jnp.where(eligible, biased, -jnp.inf)
    sels, gates = [], []
    onehot = jnp.zeros((ne, tm), jnp.bool_)
    for k in range(TOP_K):
        m = jnp.max(cur, axis=0, keepdims=True)
        first = jnp.min(jnp.where(cur == m, eidx, ne), axis=0, keepdims=True)
        sel = eidx == first
        e_ref[k:k + 1, :] = first
        gates.append(jnp.sum(jnp.where(sel, scores, 0.0), axis=0, keepdims=True))
        sels.append(sel)
        onehot = onehot | sel
        cur = jnp.where(sel, -jnp.inf, cur)
    total = gates[0]
    for gk in gates[1:]:
        total = total + gk
    for k in range(TOP_K):
        gate_ref[k:k + 1, :] = gates[k] / total * ROUTED_SCALE

    tr = lax.broadcasted_iota(I32, (tm, tm), 0)
    tc = lax.broadcasted_iota(I32, (tm, tm), 1)
    before = (tr < tc).astype(BF16)
    oh = onehot.astype(F32)
    prefix = jnp.dot(oh.astype(BF16), before, preferred_element_type=F32) + carry_ref[...]
    for k in range(TOP_K):
        rank_ref[k:k + 1, :] = jnp.sum(jnp.where(sels[k], prefix, 0.0), axis=0, keepdims=True).astype(I32)
    carry = carry_ref[...] + jnp.sum(oh, axis=1, keepdims=True)
    carry_ref[...] = carry
    cnt_ref[...] = carry.astype(I32)


def _router(lay, h, router_wt, bias, lat_only):
    col = lambda i: (0, lay.tile(i, lat_only))
    kt = pl.BlockSpec((TOP_K, TM), col)
    return pl.pallas_call(
        _router_kernel,
        grid=(lay.n_tiles(lat_only),),
        in_specs=[pl.BlockSpec((TM, D_MODEL), lambda i: (lay.tile(i, lat_only), 0)),
                  pl.BlockSpec((N_EXPERTS, D_MODEL), lambda i: (0, 0)),
                  pl.BlockSpec((N_EXPERTS, 1), lambda i: (0, 0))],
        out_specs=[kt, kt, kt, pl.BlockSpec((N_EXPERTS, 1), lambda i: (0, 0))],
        out_shape=[jax.ShapeDtypeStruct((TOP_K, lay.rows), I32), jax.ShapeDtypeStruct((TOP_K, lay.rows), F32),
                   jax.ShapeDtypeStruct((TOP_K, lay.rows), I32), jax.ShapeDtypeStruct((N_EXPERTS, 1), I32)],
        scratch_shapes=[pltpu.VMEM((N_EXPERTS, 1), F32)],
        compiler_params=_params(("arbitrary",)),
        name="router",
    )(h, router_wt, bias.reshape(-1, 1))


def _row_copy(src_ref, src_row, dst_ref, dst_row, sem):
    return pltpu.make_async_copy(src_ref.at[pl.ds(src_row, 1)], dst_ref.at[pl.ds(dst_row, 1)], sem)


def _dispatch_kernel(start_ref, e_ref, rank_ref, hp_ref, xs_ref, sem):
    tm = hp_ref.shape[0]

    def issue(t, carry):
        for k in range(TOP_K):
            dest = start_ref[e_ref[k, t]] + rank_ref[k, t]
            _row_copy(hp_ref, t, xs_ref, dest, sem).start(priority=k % 2)
        return carry

    def drain(t, carry):
        for k in range(TOP_K):
            _row_copy(hp_ref, 0, xs_ref, 0, sem).wait()
        return carry

    lax.fori_loop(0, tm, issue, 0)
    lax.fori_loop(0, tm, drain, 0)


def _dispatch(lay, pad_start, top_e, rank, hp, n_rows, lat_only):
    col = lambda i, s: (0, lay.tile(i, lat_only))
    smem = pl.BlockSpec((TOP_K, TM), col, memory_space=pltpu.SMEM)
    return pl.pallas_call(
        _dispatch_kernel,
        grid_spec=pltpu.PrefetchScalarGridSpec(
            num_scalar_prefetch=1, grid=(lay.n_tiles(lat_only),),
            in_specs=[smem, smem, pl.BlockSpec((TM, D_MODEL // 2), lambda i, s: (lay.tile(i, lat_only), 0))],
            out_specs=pl.BlockSpec(memory_space=pl.ANY),
            scratch_shapes=[pltpu.SemaphoreType.DMA]),
        out_shape=jax.ShapeDtypeStruct((n_rows, D_MODEL // 2), U32),
        compiler_params=_params(("arbitrary",)),
        name="dispatch",
    )(pad_start, top_e, rank, hp)


def _experts_kernel(be_ref, bv_ref, nu_ref, xs_ref, wg_ref, wu_ref, wd_ref, ys_ref, wgb_ref, wub_ref, wdb_ref):
    j = pl.program_id(0)
    live = j < nu_ref[0]
    prev = be_ref[jnp.maximum(j - 1, 0)]

    @pl.when(live & ((j == 0) | (be_ref[j] != prev)))
    def _():
        wgb_ref[...] = wg_ref[0, 0].astype(BF16)
        wub_ref[...] = wu_ref[0, 0].astype(BF16)
        wdb_ref[...] = wd_ref[0, 0].astype(BF16)

    @pl.when(live)
    def _():
        rowi = lax.broadcasted_iota(I32, xs_ref.shape, 0)
        packed = jnp.where(rowi < bv_ref[j], xs_ref[...], jnp.uint32(0))
        x = _unpack_bf16_pairs(packed)
        hid = (_silu(jnp.dot(x, wgb_ref[...], preferred_element_type=F32))
               * jnp.dot(x, wub_ref[...], preferred_element_type=F32))
        ys_ref[...] = jnp.dot(hid.astype(BF16), wdb_ref[...], preferred_element_type=F32)


def _experts(block_e, block_valid, n_used, xs, w_gate, w_up, w_down, layer):
    n_blocks = xs.shape[0] // MOE_BLK
    last = lambda j, nu: jnp.minimum(j, nu[0] - 1)
    wmap = lambda j, be, bv, nu: (layer, be[last(j, nu)], 0, 0)
    return pl.pallas_call(
        _experts_kernel,
        grid_spec=pltpu.PrefetchScalarGridSpec(
            num_scalar_prefetch=3, grid=(n_blocks,),
            in_specs=[pl.BlockSpec((MOE_BLK, D_MODEL // 2), lambda j, be, bv, nu: (last(j, nu), 0)),
                      pl.BlockSpec((1, 1, D_MODEL, EXPERT_HIDDEN), wmap),
                      pl.BlockSpec((1, 1, D_MODEL, EXPERT_HIDDEN), wmap),
                      pl.BlockSpec((1, 1, EXPERT_HIDDEN, D_MODEL), wmap)],
            out_specs=pl.BlockSpec((MOE_BLK, D_MODEL), lambda j, be, bv, nu: (last(j, nu), 0)),
            scratch_shapes=[pltpu.VMEM((D_MODEL, EXPERT_HIDDEN), BF16), pltpu.VMEM((D_MODEL, EXPERT_HIDDEN), BF16),
                            pltpu.VMEM((EXPERT_HIDDEN, D_MODEL), BF16)]),
        out_shape=jax.ShapeDtypeStruct((xs.shape[0], D_MODEL), F32),
        compiler_params=_params(("arbitrary",)),
        name="experts",
    )(block_e, block_valid, n_used, xs, w_gate, w_up, w_down)


def _combine_kernel(start_ref, e_ref, rank_ref, gate_ref, hp_ref, x_ref, gf_ref, g3_ref, sg_ref, su_ref, sd_ref,
                    ys_ref, o_ref, buf_ref, sem):
    tm = hp_ref.shape[0]

    def issue(t, carry):
        for k in range(TOP_K):
            src = start_ref[e_ref[k, t]] + rank_ref[k, t]
            pltpu.make_async_copy(ys_ref.at[pl.ds(src, 1)], buf_ref.at[k, pl.ds(t, 1)], sem).start(priority=k % 2)
        return carry

    def drain(t, carry):
        for k in range(TOP_K):
            pltpu.make_async_copy(ys_ref.at[pl.ds(0, 1)], buf_ref.at[0, pl.ds(0, 1)], sem).wait()
        return carry

    lax.fori_loop(0, tm, issue, 0)
    hb = _unpack_bf16_pairs(hp_ref[...])
    hid = (_silu(jnp.dot(hb, sg_ref[...], preferred_element_type=F32))
           * jnp.dot(hb, su_ref[...], preferred_element_type=F32))
    f = jnp.dot(hid.astype(BF16), sd_ref[...], preferred_element_type=F32)
    lax.fori_loop(0, tm, drain, 0)
    gate = gate_ref[...]
    for k in range(TOP_K):
        f = f + gate[:, k:k + 1] * buf_ref[k]
    o_ref[...] = x_ref[...] + gf_ref[0] * _rms(f, g3_ref[...])


def _combine(lay, pad_start, top_e, rank, gate_t, hp, x_mid, mods, g3, s_gate, s_up, s_down, ys, lat_only,
             compact_out):
    col = lambda i, s: (0, lay.tile(i, lat_only))
    row = lambda i, s: (lay.tile(i, lat_only), 0)
    smem = pl.BlockSpec((TOP_K, TM), col, memory_space=pltpu.SMEM)
    full = lambda shape: pl.BlockSpec(shape, lambda i, s: (0, 0))
    mod = pl.BlockSpec((1, 1, D_MODEL), lambda i, s: (lay.mod_row(i, lat_only) * 6 + 5, 0, 0))
    out_rows = lay.n_tiles(lat_only) * TM if compact_out else lay.rows
    out_map = (lambda i, s: (i, 0)) if compact_out else row
    return pl.pallas_call(
        _combine_kernel,
        grid_spec=pltpu.PrefetchScalarGridSpec(
            num_scalar_prefetch=1, grid=(lay.n_tiles(lat_only),),
            in_specs=[smem, smem, pl.BlockSpec((TM, TOP_K), row),
                      pl.BlockSpec((TM, D_MODEL // 2), row), pl.BlockSpec((TM, D_MODEL), row), mod,
                      full((1, D_MODEL)), full((D_MODEL, EXPERT_HIDDEN)), full((D_MODEL, EXPERT_HIDDEN)),
                      full((EXPERT_HIDDEN, D_MODEL)), pl.BlockSpec(memory_space=pl.ANY)],
            out_specs=pl.BlockSpec((TM, D_MODEL), out_map),
            scratch_shapes=[pltpu.VMEM((TOP_K, TM, D_MODEL), F32), pltpu.SemaphoreType.DMA]),
        out_shape=jax.ShapeDtypeStruct((out_rows, D_MODEL), F32),
        compiler_params=_params(("arbitrary",)),
        name="combine",
    )(pad_start, top_e, rank, gate_t, hp, x_mid, mods, g3, s_gate, s_up, s_down, ys)


def _rope_tables(lay):
    n_freq = HEAD_DIM // 4
    inv_freq = ROPE_THETA ** (-jnp.arange(n_freq, dtype=F32) / n_freq)
    t = jnp.arange(lay.n_lat)
    pos = jnp.stack([(t // GRID_W).astype(F32), (t % GRID_W).astype(F32)], axis=-1)
    ang = pos[:, :, None, None] * inv_freq
    ang = jnp.broadcast_to(ang, (lay.n_lat, 2, 2, n_freq)).reshape(lay.n_lat, HEAD_DIM)
    sign = jnp.tile(jnp.concatenate([-jnp.ones(n_freq, F32), jnp.ones(n_freq, F32)]), 2)
    cos = jnp.concatenate([jnp.ones((lay.n_ctx, HEAD_DIM), F32), jnp.cos(ang)], axis=0)
    sin = jnp.concatenate([jnp.zeros((lay.n_ctx, HEAD_DIM), F32), jnp.sin(ang) * sign], axis=0)
    return cos, sin


def _moe_plan(counts, n_blocks):
    counts = counts.reshape(-1)
    padded = (counts + MOE_BLK - 1) // MOE_BLK * MOE_BLK
    pad_end = jnp.cumsum(padded)
    pad_start = (pad_end - padded).astype(I32)
    blk_row = jnp.arange(n_blocks, dtype=I32) * MOE_BLK
    block_e = jnp.minimum(jnp.sum(blk_row[:, None] >= pad_end[None, :], axis=1), N_EXPERTS - 1).astype(I32)
    own = block_e[:, None] == jnp.arange(N_EXPERTS, dtype=I32)[None, :]
    count_b = jnp.sum(jnp.where(own, counts[None, :], 0), axis=1)
    start_b = jnp.sum(jnp.where(own, pad_start[None, :], 0), axis=1)
    block_valid = jnp.clip(count_b - (blk_row - start_b), 0, MOE_BLK).astype(I32)
    n_used = jnp.maximum(pad_end[-1:] // MOE_BLK, 1).astype(I32)
    return pad_start, block_e, block_valid, n_used


def kernel(x, c, ctx, c_ctx, mod_w, mod_b, norm_g, w_in, w_out, conv_w, conv_b, lru_wr, lru_br, lru_wi, lru_bi,
           lru_lambda, gla_w2, gla_b2, gla_norm_g, qk_norm_g, win_sink, router_w, router_bias, exp_w_gate,
           exp_w_up, exp_w_down, sh_w_gate, sh_w_up, sh_w_down):
    bsz, n_lat, d = x.shape
    n_ctx = ctx.shape[1]
    depth = mod_w.shape[0]
    assert d == D_MODEL and bsz < MOD_ROWS
    lay = _Layout(bsz, n_lat, n_ctx)

    cc = jnp.zeros((MOD_ROWS, d), F32).at[:bsz].set(c).at[bsz].set(c_ctx)
    mods_all = _adaln(cc, mod_w, mod_b)
    cos, sin = _rope_tables(lay)
    xt = jnp.concatenate([ctx, x], axis=1).reshape(lay.rows, d)

    splits = np.cumsum([0, W_A, W_B_RAW, W_C, W_D])
    for layer in range(depth):
        last = layer == depth - 1
        need_ctx = not last
        lat_only = last
        mods = mods_all[layer].reshape(MOD_ROWS * 6, 1, d)
        g = norm_g[layer]
        wl = w_in[layer]
        w_cat = jnp.concatenate(
            [wl[:, splits[0]:splits[1]], wl[:, splits[1]:splits[2]], jnp.zeros((d, W_B - W_B_RAW), F32),
             wl[:, splits[2]:splits[3]], wl[:, splits[3]:splits[4]]], axis=1).astype(BF16)
        p_a, p_b, p_c, p_d = _inproj(lay, xt, mods, g[0:1], w_cat)

        y_a = _rglru(lay, p_a, conv_w[layer], conv_b[layer], lru_wr[layer], lru_br[layer], lru_wi[layer],
                     lru_bi[layer], lru_lambda[layer])
        w2pad = jnp.zeros((2, LANES, GLA_HEADS * GLA_DK), F32)
        w2pad = w2pad.at[0, :GLA_RANK].set(gla_w2[layer, 0]).at[1, GLA_RANK:2 * GLA_RANK].set(gla_w2[layer, 1])
        y_b = _gla(lay, p_b, w2pad, gla_b2[layer], gla_norm_g[layer])
        y_c = _attention(lay, p_c, cos, sin, qk_norm_g[layer], window=False, need_ctx=need_ctx)
        y_d = _attention(lay, p_d, cos, sin, win_sink[layer], window=True, need_ctx=need_ctx)

        x_mid, h, hp = _outproj(lay, (y_a, y_b, y_c, y_d), w_out[layer].astype(BF16), xt, mods, g[1:2], g[2:3],
                                lat_only)

        top_e, gate, rank, counts = _router(lay, h, router_w[layer].T, router_bias[layer], lat_only)
        n_assign = lay.n_tiles(lat_only) * TM * TOP_K
        n_blocks = -(-(n_assign + N_EXPERTS * (MOE_BLK - 1)) // MOE_BLK)
        pad_start, block_e, block_valid, n_used = _moe_plan(counts, n_blocks)
        xs = _dispatch(lay, pad_start, top_e, rank, hp, n_blocks * MOE_BLK, lat_only)
        ys = _experts(block_e, block_valid, n_used, xs, exp_w_gate, exp_w_up, exp_w_down, layer)
        xt = _combine(lay, pad_start, top_e, rank, gate.T, hp, x_mid, mods, g[3:4], sh_w_gate[layer].astype(BF16),
                      sh_w_up[layer].astype(BF16), sh_w_down[layer].astype(BF16), ys, lat_only, compact_out=last)
    return xt.reshape(bsz, n_lat, d)
```

```python
import functools

import numpy as np
import jax
import jax.numpy as jnp
from jax import lax
from jax.experimental import pallas as pl
from jax.experimental.pallas import tpu as pltpu

F32 = jnp.float32
BF16 = jnp.bfloat16
I32 = jnp.int32
U32 = jnp.uint32
HIGHEST = lax.Precision.HIGHEST
NT_DIMS = (((1,), (1,)), ((), ()))
TN_DIMS = (((0,), (0,)), ((), ()))

LANES = 128
SUBLANES = 8
VMEM_LIMIT_BYTES = 56 * 1024 * 1024

D_MODEL = 2048
GRID_W = 64
HEAD_DIM = 128
GROUP_WIDTH = D_MODEL // 4
LRU_BLOCKS = 4
CONV_W = 4
RGLRU_C = 8.0
GLA_HEADS = 4
GLA_DK = 64
GLA_DV = 128
GLA_RANK = 16
GLA_TAU = 16.0
GLA_CHUNK = 64
KV_HEADS = 2
WINDOW = 128
N_EXPERTS = 64
N_GROUPS = 8
TOPK_GROUPS = 4
TOP_K = 8
EXPERT_HIDDEN = 512
ROUTED_SCALE = 2.5
RMS_EPS = 1e-6
ROPE_THETA = 10000.0
NEG_INF = -1e30

W_A = 2 * GROUP_WIDTH
W_B_RAW = 2 * GLA_HEADS * GLA_DK + 2 * GLA_HEADS * GLA_DV + 2 * GLA_RANK
W_B = 1664
W_C = GROUP_WIDTH + 2 * KV_HEADS * HEAD_DIM
W_D = W_C
Z_COL_BLOCK = (2 * GLA_HEADS * GLA_DK + 2 * GLA_HEADS * GLA_DV) // LANES

TM = 256
MOE_BLK = 512
PACKED_TILES = D_MODEL // 2 // LANES
ROW_TILES = D_MODEL // LANES
MOD_ROWS = 16


def _params(sem, vmem=VMEM_LIMIT_BYTES):
    return pltpu.CompilerParams(dimension_semantics=sem, vmem_limit_bytes=vmem)


def _rms(x, g):
    return x * lax.rsqrt(jnp.mean(x * x, axis=-1, keepdims=True) + RMS_EPS) * g


def _silu(x):
    return x * jax.nn.sigmoid(x)


def _pack_bf16_pairs(h):
    n = h.shape[1] // 2
    hb = h.astype(BF16).astype(F32)
    hi = lax.bitcast_convert_type(hb[:, :n], U32)
    lo = lax.bitcast_convert_type(hb[:, n:], U32)
    return hi | (lo >> 16)


def _to_row_tiles(ref, x):
    m = x.shape[0]
    c = x.shape[1] // LANES
    for j in range(c):
        ref[pl.ds(j, m, stride=c), :] = x[:, j * LANES:(j + 1) * LANES]


def _from_row_tiles(ref, c, lead=()):
    m = ref.shape[-2] // c
    return jnp.concatenate([ref[lead + (pl.ds(j, m, stride=c), slice(None))] for j in range(c)], axis=1)


def _unpack_bf16_pairs(p):
    hi = lax.bitcast_convert_type(p & jnp.uint32(0xFFFF0000), F32)
    lo = lax.bitcast_convert_type(p << 16, F32)
    return jnp.concatenate([hi, lo], axis=1).astype(BF16)


def _adaln_kernel(c_ref, w_ref, b_ref, o_ref):
    a = _silu(c_ref[...])
    o_ref[0] = jnp.dot(a, w_ref[0], preferred_element_type=F32, precision=HIGHEST) + b_ref[0]


def _adaln(cc, mod_w, mod_b):
    depth, d, n = mod_w.shape
    tn = 1024
    return pl.pallas_call(
        _adaln_kernel,
        grid=(depth, n // tn),
        in_specs=[pl.BlockSpec((MOD_ROWS, d), lambda l, j: (0, 0)),
                  pl.BlockSpec((1, d, tn), lambda l, j: (l, 0, j)),
                  pl.BlockSpec((1, 1, tn), lambda l, j: (l, 0, j))],
        out_specs=pl.BlockSpec((1, MOD_ROWS, tn), lambda l, j: (l, 0, j)),
        out_shape=jax.ShapeDtypeStruct((depth, MOD_ROWS, n), F32),
        compiler_params=_params(("arbitrary", "arbitrary")),
        name="adaln",
    )(cc, mod_w, mod_b.reshape(depth, 1, n))


class _Layout:
    def __init__(self, bsz, n_lat, n_ctx):
        self.bsz, self.n_lat, self.n_ctx = bsz, n_lat, n_ctx
        self.seq = n_ctx + n_lat
        self.rows = bsz * self.seq
        assert n_ctx % TM == 0 and n_lat % TM == 0
        self.tiles_b = self.seq // TM
        self.ctx_tiles = n_ctx // TM
        self.lat_tiles = n_lat // TM

    def n_tiles(self, lat_only):
        return self.bsz * (self.lat_tiles if lat_only else self.tiles_b)

    def tile(self, i, lat_only):
        if lat_only:
            return (i // self.lat_tiles) * self.tiles_b + self.ctx_tiles + i % self.lat_tiles
        return i

    def mod_row(self, i, lat_only):
        if lat_only:
            return i // self.lat_tiles
        return jnp.where(i % self.tiles_b < self.ctx_tiles, self.bsz, i // self.tiles_b)


def _mod_spec(lay, chunk, lat_only):
    return pl.BlockSpec((1, 1, D_MODEL), lambda i: (lay.mod_row(i, lat_only) * 6 + chunk, 0, 0))


def _inproj_kernel(x_ref, sh_ref, sc_ref, g_ref, w_ref, oa_ref, ob_ref, oc_ref, od_ref):
    h = _rms(x_ref[...], g_ref[...]) * (1.0 + sc_ref[0]) + sh_ref[0]
    hb = h.astype(BF16)
    c0 = 0
    for o_ref in (oa_ref, ob_ref, oc_ref, od_ref):
        width = o_ref.shape[1]
        o_ref[...] = jnp.dot(hb, w_ref[:, c0:c0 + width], preferred_element_type=F32)
        c0 += width


def _inproj(lay, xt, mods, g0, w_cat):
    widths = (W_A, W_B, W_C, W_D)
    row = lambda i: (i, 0)
    return pl.pallas_call(
        _inproj_kernel,
        grid=(lay.n_tiles(False),),
        in_specs=[pl.BlockSpec((TM, D_MODEL), row),
                  _mod_spec(lay, 0, False), _mod_spec(lay, 1, False),
                  pl.BlockSpec((1, D_MODEL), lambda i: (0, 0)),
                  pl.BlockSpec((D_MODEL, sum(widths)), lambda i: (0, 0), pipeline_mode=pl.Buffered(1))],
        out_specs=[pl.BlockSpec((TM, w), row) for w in widths],
        out_shape=[jax.ShapeDtypeStruct((lay.rows, w), F32) for w in widths],
        compiler_params=_params(("arbitrary",)),
        name="inproj",
    )(xt, mods, mods, g0, w_cat)


def _rglru_kernel(x_ref, g_ref, cw_ref, cb_ref, wr_ref, br_ref, wi_ref, bi_ref, lam_ref, o_ref,
                  xs_ref, a_ref, b_ref, h_ref, *, n_ctx):
    seq = x_ref.shape[0]
    pad = SUBLANES
    n_tiles = seq // SUBLANES
    ctx_tiles = n_ctx // SUBLANES
    row = lax.broadcasted_iota(I32, (seq, LANES), 0)

    xs_ref[0:pad, :] = jnp.zeros((pad, LANES), F32)
    xs_ref[pad + seq:2 * pad + seq, :] = jnp.zeros((pad, LANES), F32)
    xs_ref[pad:pad + seq, :] = x_ref[...]
    u = jnp.zeros((seq, LANES), F32) + cb_ref[...]
    for j in range(CONV_W):
        off = j - 2
        xsh = xs_ref[pad + off:pad + off + seq, :]
        if off < 0:
            xsh = jnp.where((row >= n_ctx) & (row + off < n_ctx), 0.0, xsh)
        elif off > 0:
            xsh = jnp.where((row < n_ctx) & (row + off >= n_ctx), 0.0, xsh)
        u = u + cw_ref[j:j + 1, :] * xsh

    ub = u.astype(BF16)
    rmod = row & (SUBLANES - 1)
    for d in range(2):
        reverse = d == 1
        r = jax.nn.sigmoid(jnp.dot(ub, wr_ref[d, 0].astype(BF16), preferred_element_type=F32) + br_ref[d:d + 1, :])
        gi = jax.nn.sigmoid(jnp.dot(ub, wi_ref[d, 0].astype(BF16), preferred_element_type=F32) + bi_ref[d:d + 1, :])
        lam = lam_ref[d:d + 1, :]
        softplus_neg = jnp.maximum(-lam, 0.0) + jnp.log(1.0 + jnp.exp(-jnp.abs(lam)))
        a = jnp.exp(-RGLRU_C * r * softplus_neg)
        b = jnp.sqrt(1.0 - a * a) * (gi * u)
        for s in (1, 2, 4):
            if reverse:
                ap, bp, ok = pltpu.roll(a, seq - s, 0), pltpu.roll(b, seq - s, 0), rmod < SUBLANES - s
            else:
                ap, bp, ok = pltpu.roll(a, s, 0), pltpu.roll(b, s, 0), rmod >= s
            b = jnp.where(ok, a * bp + b, b)
            a = jnp.where(ok, a * ap, a)
        a_ref[...] = a
        b_ref[...] = b

        def step(t, h, reverse=reverse):
            r0 = pl.multiple_of(t * SUBLANES, SUBLANES)
            h8 = a_ref[pl.ds(r0, SUBLANES), :] * h + b_ref[pl.ds(r0, SUBLANES), :]
            if reverse:
                h_ref[pl.ds(r0, SUBLANES), :] = h_ref[pl.ds(r0, SUBLANES), :] + h8
                return jnp.broadcast_to(h8[0:1, :], (SUBLANES, LANES))
            h_ref[pl.ds(r0, SUBLANES), :] = h8
            return jnp.broadcast_to(h8[SUBLANES - 1:SUBLANES, :], (SUBLANES, LANES))

        h0 = jnp.zeros((SUBLANES, LANES), F32)
        if reverse:
            hc = lax.fori_loop(0, ctx_tiles, lambda i, h: step(ctx_tiles - 1 - i, h), h0)
            lax.fori_loop(0, n_tiles - ctx_tiles, lambda i, h: step(n_tiles - 1 - i, h), hc)
        else:
            lax.fori_loop(0, n_tiles, step, h0)

    g = g_ref[...]
    gelu = 0.5 * g * (1.0 + jnp.tanh(np.sqrt(2.0 / np.pi) * (g + 0.044715 * (g * g * g))))
    o_ref[...] = h_ref[...] * gelu


def _rglru(lay, p_a, conv_w, conv_b, wr, br, wi, bi, lam):
    seq = lay.seq
    nb = LRU_BLOCKS
    col = lambda off: (lambda b, j: (b, off + j))
    vec = lambda rows: pl.BlockSpec((rows, LANES), lambda b, j: (0, j))
    wspec = pl.BlockSpec((2, 1, LANES, LANES), lambda b, j: (0, j, 0, 0))
    return pl.pallas_call(
        functools.partial(_rglru_kernel, n_ctx=lay.n_ctx),
        grid=(lay.bsz, nb),
        in_specs=[pl.BlockSpec((seq, LANES), col(0)), pl.BlockSpec((seq, LANES), col(nb)),
                  vec(CONV_W), vec(1), wspec, vec(2), wspec, vec(2), vec(2)],
        out_specs=pl.BlockSpec((seq, LANES), col(0)),
        out_shape=jax.ShapeDtypeStruct((lay.rows, GROUP_WIDTH), F32),
        scratch_shapes=[pltpu.VMEM((seq + 2 * SUBLANES, LANES), F32), pltpu.VMEM((seq, LANES), F32),
                        pltpu.VMEM((seq, LANES), F32), pltpu.VMEM((seq, LANES), F32)],
        compiler_params=_params(("arbitrary", "arbitrary")),
        name="rglru",
    )(p_a, p_a, conv_w, conv_b.reshape(1, -1), wr, br, wi, bi, lam)


def _gla_kernel(q_ref, k_ref, v_ref, g_ref, z_ref, w2_ref, b2_ref, ng_ref, o_ref,
                laf_ref, lab_ref, of_ref, ob_ref, *, n_ctx):
    seq = q_ref.shape[0]
    cs = GLA_CHUNK
    n_chunks = seq // cs
    ctx_chunks = n_ctx // cs
    z = z_ref[...]
    for d, la_ref in ((0, laf_ref), (1, lab_ref)):
        pre = jnp.dot(z, w2_ref[d], preferred_element_type=F32, precision=HIGHEST) + b2_ref[d:d + 1, :]
        log_sig = jnp.minimum(pre, 0.0) - jnp.log(1.0 + jnp.exp(-jnp.abs(pre)))
        la_ref[...] = log_sig * (1.0 / GLA_TAU)

    ri = lax.broadcasted_iota(I32, (cs, cs), 0)
    ci = lax.broadcasted_iota(I32, (cs, cs), 1)
    scale = GLA_DK ** -0.5

    def chunk(c, states, reverse):
        r0 = pl.multiple_of(c * cs, cs)
        rows = pl.ds(r0, cs)
        keep = (ri <= ci) if reverse else (ri >= ci)
        la = (lab_ref if reverse else laf_ref)[rows, :]
        cum = jnp.dot(keep.astype(F32), la, preferred_element_type=F32, precision=HIGHEST)
        last = cum[0:1, :] if reverse else cum[cs - 1:cs, :]
        q = q_ref[rows, :] * scale
        k = k_ref[rows, :]
        qe = q * jnp.exp(cum)
        ke = k * jnp.exp(-cum)
        kl = k * jnp.exp(last - cum)
        el = jnp.exp(last)
        out_ref = ob_ref if reverse else of_ref
        new_states = []
        for hh in range(2):
            dk = slice(hh * GLA_DK, (hh + 1) * GLA_DK)
            dv = slice(hh * GLA_DV, (hh + 1) * GLA_DV)
            qh = qe[:, dk].astype(BF16)
            vh = v_ref[rows, dv].astype(BF16)
            st = states[hh]
            s = lax.dot_general(qh, ke[:, dk].astype(BF16), NT_DIMS, preferred_element_type=F32)
            s = jnp.where(keep, s, 0.0)
            o = (jnp.dot(s.astype(BF16), vh, preferred_element_type=F32)
                 + lax.dot_general(qh, st.astype(BF16), NT_DIMS, preferred_element_type=F32))
            out_ref[rows, dv] = o
            new_states.append(st * el[:, dk]
                              + lax.dot_general(vh, kl[:, dk].astype(BF16), TN_DIMS, preferred_element_type=F32))
        return tuple(new_states)

    def body(i, carry):
        sf, sb = carry
        sf = chunk(i, sf, False)
        cb = jnp.where(i < ctx_chunks, ctx_chunks - 1 - i, n_chunks - 1 + ctx_chunks - i)
        sb = chunk(cb, sb, True)
        return sf, sb

    zero = jnp.zeros((GLA_DV, GLA_DK), F32)
    lax.fori_loop(0, n_chunks, body, ((zero, zero), (zero, zero)))

    ng = ng_ref[...]
    for hh in range(2):
        dv = slice(hh * GLA_DV, (hh + 1) * GLA_DV)
        o = of_ref[:, dv] + ob_ref[:, dv]
        o_ref[:, dv] = _rms(o, ng) * _silu(g_ref[:, dv])


def _gla(lay, p_b, w2pad, b2, norm_g):
    seq = lay.seq
    hp = GLA_HEADS // 2
    qk_w = 2 * GLA_DK
    v_w = 2 * GLA_DV
    k_off = GLA_HEADS * GLA_DK // qk_w
    v_off = 2 * GLA_HEADS * GLA_DK // v_w
    g_off = v_off + GLA_HEADS * GLA_DV // v_w
    blk = lambda w, off: pl.BlockSpec((seq, w), lambda b, j: (b, off + j))
    return pl.pallas_call(
        functools.partial(_gla_kernel, n_ctx=lay.n_ctx),
        grid=(lay.bsz, hp),
        in_specs=[blk(qk_w, 0), blk(qk_w, k_off), blk(v_w, v_off), blk(v_w, g_off),
                  pl.BlockSpec((seq, LANES), lambda b, j: (b, Z_COL_BLOCK)),
                  pl.BlockSpec((2, LANES, qk_w), lambda b, j: (0, 0, j)),
                  pl.BlockSpec((2, qk_w), lambda b, j: (0, j)),
                  pl.BlockSpec((1, GLA_DV), lambda b, j: (0, 0))],
        out_specs=blk(v_w, 0),
        out_shape=jax.ShapeDtypeStruct((lay.rows, GROUP_WIDTH), F32),
        scratch_shapes=[pltpu.VMEM((seq, qk_w), F32), pltpu.VMEM((seq, qk_w), F32),
                        pltpu.VMEM((seq, v_w), F32), pltpu.VMEM((seq, v_w), F32)],
        compiler_params=_params(("arbitrary", "arbitrary")),
        name="gla",
    )(p_b, p_b, p_b, p_b, p_b, w2pad, b2, norm_g.reshape(1, -1))


def _rope(x, cos, sin_signed):
    lane = lax.broadcasted_iota(I32, x.shape, 1)
    partner = jnp.where((lane & 32) != 0, pltpu.roll(x, 32, 1), pltpu.roll(x, HEAD_DIM - 32, 1))
    return x * cos + partner * sin_signed


def _attn_kernel(*refs, window, need_ctx, tq, n_ctx):
    if window:
        sink_ref, q_ref, k_ref, v_ref, cos_ref, sin_ref, o_ref, ks_ref, vs_ref = refs
        qkg_ref = None
    else:
        q_ref, k_ref, v_ref, cos_ref, sin_ref, qkg_ref, o_ref, ks_ref, vs_ref = refs
    seq = k_ref.shape[0]
    n_lat = seq - n_ctx
    g = pl.program_id(1)
    qi = pl.program_id(2)
    jq = qi if need_ctx else qi + n_ctx // tq

    @pl.when(qi == 0)
    def _():
        k = k_ref[...]
        if not window:
            k = _rms(k, qkg_ref[1:2, :])
        ks_ref[...] = _rope(k, cos_ref[...], sin_ref[...]).astype(BF16)
        vs_ref[...] = v_ref[...].astype(BF16)

    r0 = pl.multiple_of(jq * tq, tq)
    cos = cos_ref[pl.ds(r0, tq), :]
    sin = sin_ref[pl.ds(r0, tq), :]
    heads = []
    for hh in range(2):
        qh = q_ref[:, hh * HEAD_DIM:(hh + 1) * HEAD_DIM]
        if not window:
            qh = _rms(qh, qkg_ref[0:1, :])
        heads.append((_rope(qh, cos, sin) * HEAD_DIM ** -0.5).astype(BF16))
    q2 = jnp.concatenate(heads, axis=0)

    def finish(o):
        o_ref[:, 0:HEAD_DIM] = o[0:tq]
        o_ref[:, HEAD_DIM:2 * HEAD_DIM] = o[tq:2 * tq]

    def scores(lo, n):
        return lax.dot_general(q2, ks_ref[pl.ds(lo, n), :], NT_DIMS, preferred_element_type=F32)

    def pv(p, lo, n):
        return jnp.dot(p.astype(BF16), vs_ref[pl.ds(lo, n), :], preferred_element_type=F32)

    if window:
        rowi = lax.broadcasted_iota(I32, (2 * tq, 1), 0)
        sink = jnp.where(rowi < tq, sink_ref[2 * g], sink_ref[2 * g + 1])

    def ctx_tile():
        s = scores(0, n_ctx)
        m = jnp.max(s, axis=-1, keepdims=True)
        if window:
            m = jnp.maximum(m, sink)
        p = jnp.exp(s - m)
        l = jnp.sum(p, axis=-1, keepdims=True)
        if window:
            l = l + jnp.exp(sink - m)
        finish(pv(p, 0, n_ctx) / l)

    def lat_tile():
        if not window:
            s = scores(0, seq)
            m = jnp.max(s, axis=-1, keepdims=True)
            p = jnp.exp(s - m)
            finish(pv(p, 0, seq) / jnp.sum(p, axis=-1, keepdims=True))
            return
        band = 3 * WINDOW
        jl = jq - n_ctx // tq
        start = jnp.clip((jl - 1) * WINDOW, 0, n_lat - band)
        lo = pl.multiple_of(n_ctx + start, WINDOW)
        sb = scores(lo, band)
        q_pos = jl * WINDOW + lax.broadcasted_iota(I32, (2 * tq, band), 0) % tq
        k_pos = start + lax.broadcasted_iota(I32, (2 * tq, band), 1)
        sb = jnp.where(jnp.abs(q_pos - k_pos) <= WINDOW, sb, NEG_INF)
        sc = scores(0, n_ctx)
        m = jnp.maximum(jnp.maximum(jnp.max(sb, axis=-1, keepdims=True), jnp.max(sc, axis=-1, keepdims=True)), sink)
        pb = jnp.exp(sb - m)
        pc = jnp.exp(sc - m)
        l = jnp.sum(pb, axis=-1, keepdims=True) + jnp.sum(pc, axis=-1, keepdims=True) + jnp.exp(sink - m)
        finish((pv(pb, lo, band) + pv(pc, 0, n_ctx)) / l)

    if need_ctx:
        pl.when(jq * tq < n_ctx)(ctx_tile)
        pl.when(jq * tq >= n_ctx)(lat_tile)
    else:
        lat_tile()


def _attention(lay, p, cos, sin, extra, *, window, need_ctx):
    seq = lay.seq
    tq = WINDOW if window else TM
    q_tiles_b = seq // tq
    nq = q_tiles_b if need_ctx else lay.n_lat // tq
    skip = 0 if need_ctx else lay.n_ctx // tq
    q_heads_w = 2 * HEAD_DIM
    k_off = GROUP_WIDTH // HEAD_DIM
    v_off = k_off + KV_HEADS

    def im(f):
        return (lambda b, g, qi, *_: f(b, g, qi))

    in_specs = [pl.BlockSpec((tq, q_heads_w), im(lambda b, g, qi: (b * q_tiles_b + skip + qi, g))),
                pl.BlockSpec((seq, HEAD_DIM), im(lambda b, g, qi: (b, k_off + g))),
                pl.BlockSpec((seq, HEAD_DIM), im(lambda b, g, qi: (b, v_off + g))),
                pl.BlockSpec((seq, HEAD_DIM), im(lambda b, g, qi: (0, 0))),
                pl.BlockSpec((seq, HEAD_DIM), im(lambda b, g, qi: (0, 0)))]
    args = [p, p, p, cos, sin]
    if not window:
        in_specs.append(pl.BlockSpec((2, HEAD_DIM), im(lambda b, g, qi: (0, 0))))
        args.append(extra)
    out_spec = pl.BlockSpec((tq, q_heads_w), im(lambda b, g, qi: (b * q_tiles_b + skip + qi, g)))
    kern = functools.partial(_attn_kernel, window=window, need_ctx=need_ctx, tq=tq, n_ctx=lay.n_ctx)
    scratch = [pltpu.VMEM((seq, HEAD_DIM), BF16), pltpu.VMEM((seq, HEAD_DIM), BF16)]
    out_shape = jax.ShapeDtypeStruct((lay.rows, GROUP_WIDTH), F32)
    sem = ("arbitrary", "arbitrary", "arbitrary")
    grid = (lay.bsz, KV_HEADS, nq)
    if window:
        return pl.pallas_call(
            kern,
            grid_spec=pltpu.PrefetchScalarGridSpec(num_scalar_prefetch=1, grid=grid, in_specs=in_specs,
                                                   out_specs=out_spec, scratch_shapes=scratch),
            out_shape=out_shape, compiler_params=_params(sem), name="win_attn",
        )(extra, *args)
    return pl.pallas_call(kern, grid=grid, in_specs=in_specs, out_specs=out_spec, scratch_shapes=scratch,
                          out_shape=out_shape, compiler_params=_params(sem), name="glob_attn")(*args)


def _outproj_kernel(ya_ref, yb_ref, yc_ref, yd_ref, w_ref, x_ref, gm_ref, sh_ref, sc_ref, g1_ref, g2_ref,
                    xo_ref, h_ref, hp_ref):
    acc = None
    for i, y_ref in enumerate((ya_ref, yb_ref, yc_ref, yd_ref)):
        part = jnp.dot(y_ref[...].astype(BF16), w_ref[i * GROUP_WIDTH:(i + 1) * GROUP_WIDTH, :],
                       preferred_element_type=F32)
        acc = part if acc is None else acc + part
    x_new = x_ref[...] + gm_ref[0] * _rms(acc, g1_ref[...])
    xo_ref[...] = x_new
    h = _rms(x_new, g2_ref[...]) * (1.0 + sc_ref[0]) + sh_ref[0]
    h_ref[...] = h
    _to_row_tiles(hp_ref, _pack_bf16_pairs(h))


def _outproj(lay, ys, w_out, xt, mods, g1, g2, lat_only):
    row = lambda i: (lay.tile(i, lat_only), 0)
    full = lambda shape: pl.BlockSpec(shape, lambda i: (0, 0))
    yspec = pl.BlockSpec((TM, GROUP_WIDTH), row)
    xspec = pl.BlockSpec((TM, D_MODEL), row)
    return pl.pallas_call(
        _outproj_kernel,
        grid=(lay.n_tiles(lat_only),),
        in_specs=[yspec, yspec, yspec, yspec, full((D_MODEL, D_MODEL)), xspec,
                  _mod_spec(lay, 2, lat_only), _mod_spec(lay, 3, lat_only), _mod_spec(lay, 4, lat_only),
                  full((1, D_MODEL)), full((1, D_MODEL))],
        out_specs=[xspec, xspec, pl.BlockSpec((TM * PACKED_TILES, LANES), row)],
        out_shape=[jax.ShapeDtypeStruct((lay.rows, D_MODEL), F32), jax.ShapeDtypeStruct((lay.rows, D_MODEL), F32),
                   jax.ShapeDtypeStruct((lay.rows * PACKED_TILES, LANES), U32)],
        compiler_params=_params(("arbitrary",)),
        name="outproj",
    )(*ys, w_out, xt, mods, mods, mods, g1, g2)


def _router_kernel(h_ref, wt_ref, bias_ref, e_ref, gate_ref, rank_ref, cnt_ref, carry_ref):
    i = pl.program_id(0)
    tm = h_ref.shape[0]
    ne = N_EXPERTS
    gsz = ne // N_GROUPS

    @pl.when(i == 0)
    def _():
        carry_ref[...] = jnp.zeros_like(carry_ref)

    logits = lax.dot_general(wt_ref[...], h_ref[...], NT_DIMS, preferred_element_type=F32, precision=HIGHEST)
    scores = jax.nn.sigmoid(logits)
    biased = scores + bias_ref[...]

    sub = lax.broadcasted_iota(I32, (gsz, tm), 0)
    gs_rows = []
    for grp in range(N_GROUPS):
        blk = biased[grp * gsz:(grp + 1) * gsz, :]
        m1 = jnp.max(blk, axis=0, keepdims=True)
        first = jnp.min(jnp.where(blk == m1, sub, gsz), axis=0, keepdims=True)
        m2 = jnp.max(jnp.where(sub == first, -jnp.inf, blk), axis=0, keepdims=True)
        gs_rows.append(m1 + m2)
    gs = jnp.concatenate(gs_rows, axis=0)

    gidx = lax.broadcasted_iota(I32, (N_GROUPS, tm), 0)
    chosen = jnp.zeros((N_GROUPS, tm), jnp.bool_)
    cur = gs
    for _ in range(TOPK_GROUPS):
        m = jnp.max(cur, axis=0, keepdims=True)
        first = jnp.min(jnp.where(cur == m, gidx, N_GROUPS), axis=0, keepdims=True)
        sel = gidx == first
        chosen = chosen | sel
        cur = jnp.where(sel, -jnp.inf, cur)

    eidx = lax.broadcasted_iota(I32, (ne, tm), 0)
    eligible = jnp.concatenate(
        [jnp.broadcast_to(chosen[grp:grp + 1, :], (gsz, tm)) for grp in range(N_GROUPS)], axis=0)
    cur = jnp.where(eligible, biased, -jnp.inf)
    sels, gates = [], []
    onehot = jnp.zeros((ne, tm), jnp.bool_)
    for k in range(TOP_K):
        m = jnp.max(cur, axis=0, keepdims=True)
        first = jnp.min(jnp.where(cur == m, eidx, ne), axis=0, keepdims=True)
        sel = eidx == first
        e_ref[k:k + 1, :] = first
        gates.append(jnp.sum(jnp.where(sel, scores, 0.0), axis=0, keepdims=True))
        sels.append(sel)
        onehot = onehot | sel
        cur = jnp.where(sel, -jnp.inf, cur)
    total = gates[0]
    for gk in gates[1:]:
        total = total + gk
    for k in range(TOP_K):
        gate_ref[k:k + 1, :] = gates[k] / total * ROUTED_SCALE

    tr = lax.broadcasted_iota(I32, (tm, tm), 0)
    tc = lax.broadcasted_iota(I32, (tm, tm), 1)
    before = (tr < tc).astype(BF16)
    oh = onehot.astype(F32)
    prefix = jnp.dot(oh.astype(BF16), before, preferred_element_type=F32) + carry_ref[...]
    for k in range(TOP_K):
        rank_ref[k:k + 1, :] = jnp.sum(jnp.where(sels[k], prefix, 0.0), axis=0, keepdims=True).astype(I32)
    carry = carry_ref[...] + jnp.sum(oh, axis=1, keepdims=True)
    carry_ref[...] = carry
    cnt_ref[...] = carry.astype(I32)


def _router(lay, h, router_wt, bias, lat_only):
    col = lambda i: (0, lay.tile(i, lat_only))
    kt = pl.BlockSpec((TOP_K, TM), col)
    return pl.pallas_call(
        _router_kernel,
        grid=(lay.n_tiles(lat_only),),
        in_specs=[pl.BlockSpec((TM, D_MODEL), lambda i: (lay.tile(i, lat_only), 0)),
                  pl.BlockSpec((N_EXPERTS, D_MODEL), lambda i: (0, 0)),
                  pl.BlockSpec((N_EXPERTS, 1), lambda i: (0, 0))],
        out_specs=[kt, kt, kt, pl.BlockSpec((N_EXPERTS, 1), lambda i: (0, 0))],
        out_shape=[jax.ShapeDtypeStruct((TOP_K, lay.rows), I32), jax.ShapeDtypeStruct((TOP_K, lay.rows), F32),
                   jax.ShapeDtypeStruct((TOP_K, lay.rows), I32), jax.ShapeDtypeStruct((N_EXPERTS, 1), I32)],
        scratch_shapes=[pltpu.VMEM((N_EXPERTS, 1), F32)],
        compiler_params=_params(("arbitrary",)),
        name="router",
    )(h, router_wt, bias.reshape(-1, 1))


def _dispatch_kernel(dest_ref, hp_ref, xs_ref, sem):
    c = PACKED_TILES
    tm = hp_ref.shape[0] // c

    def issue(t, carry):
        src = hp_ref.at[pl.ds(pl.multiple_of(t * c, c), c)]
        for k in range(TOP_K):
            dest = pl.multiple_of(dest_ref[0, 0, t * TOP_K + k] * c, c)
            pltpu.make_async_copy(src, xs_ref.at[pl.ds(dest, c)], sem).start(priority=k % 2)
        return carry

    lax.fori_loop(0, tm, issue, 0)
    for k in range(TOP_K):
        pltpu.make_async_copy(hp_ref, xs_ref.at[pl.ds(0, tm * c)], sem).wait()


def _dispatch(lay, dest_flat, hp, n_rows, lat_only):
    return pl.pallas_call(
        _dispatch_kernel,
        grid=(lay.n_tiles(lat_only),),
        in_specs=[pl.BlockSpec((1, 1, TM * TOP_K), lambda i: (lay.tile(i, lat_only), 0, 0), memory_space=pltpu.SMEM),
                  pl.BlockSpec((TM * PACKED_TILES, LANES), lambda i: (lay.tile(i, lat_only), 0))],
        out_specs=pl.BlockSpec(memory_space=pl.ANY),
        scratch_shapes=[pltpu.SemaphoreType.DMA],
        out_shape=jax.ShapeDtypeStruct((n_rows * PACKED_TILES, LANES), U32),
        compiler_params=_params(("arbitrary",)),
        name="dispatch",
    )(dest_flat, hp)


def _experts_kernel(be_ref, bv_ref, nu_ref, xs_ref, wg_ref, wu_ref, wd_ref, ys_ref, wgb_ref, wub_ref, wdb_ref):
    j = pl.program_id(0)
    live = j < nu_ref[0]
    prev = be_ref[jnp.maximum(j - 1, 0)]

    @pl.when(live & ((j == 0) | (be_ref[j] != prev)))
    def _():
        wgb_ref[...] = wg_ref[0, 0].astype(BF16)
        wub_ref[...] = wu_ref[0, 0].astype(BF16)
        wdb_ref[...] = wd_ref[0, 0].astype(BF16)

    @pl.when(live)
    def _():
        packed = _from_row_tiles(xs_ref, PACKED_TILES)
        rowi = lax.broadcasted_iota(I32, packed.shape, 0)
        packed = jnp.where(rowi < bv_ref[j], packed, jnp.uint32(0))
        x = _unpack_bf16_pairs(packed)
        hid = (_silu(jnp.dot(x, wgb_ref[...], preferred_element_type=F32))
               * jnp.dot(x, wub_ref[...], preferred_element_type=F32))
        _to_row_tiles(ys_ref, jnp.dot(hid.astype(BF16), wdb_ref[...], preferred_element_type=F32))


def _experts(block_e, block_valid, n_used, xs, w_gate, w_up, w_down, layer):
    n_blocks = xs.shape[0] // (MOE_BLK * PACKED_TILES)
    last = lambda j, nu: jnp.minimum(j, nu[0] - 1)
    wmap = lambda j, be, bv, nu: (layer, be[last(j, nu)], 0, 0)
    return pl.pallas_call(
        _experts_kernel,
        grid_spec=pltpu.PrefetchScalarGridSpec(
            num_scalar_prefetch=3, grid=(n_blocks,),
            in_specs=[pl.BlockSpec((MOE_BLK * PACKED_TILES, LANES), lambda j, be, bv, nu: (last(j, nu), 0)),
                      pl.BlockSpec((1, 1, D_MODEL, EXPERT_HIDDEN), wmap),
                      pl.BlockSpec((1, 1, D_MODEL, EXPERT_HIDDEN), wmap),
                      pl.BlockSpec((1, 1, EXPERT_HIDDEN, D_MODEL), wmap)],
            out_specs=pl.BlockSpec((MOE_BLK * ROW_TILES, LANES), lambda j, be, bv, nu: (last(j, nu), 0)),
            scratch_shapes=[pltpu.VMEM((D_MODEL, EXPERT_HIDDEN), BF16), pltpu.VMEM((D_MODEL, EXPERT_HIDDEN), BF16),
                            pltpu.VMEM((EXPERT_HIDDEN, D_MODEL), BF16)]),
        out_shape=jax.ShapeDtypeStruct((n_blocks * MOE_BLK * ROW_TILES, LANES), F32),
        compiler_params=_params(("arbitrary",)),
        name="experts",
    )(block_e, block_valid, n_used, xs, w_gate, w_up, w_down)


def _combine_kernel(dest_ref, gate_ref, hp_ref, x_ref, gf_ref, g3_ref, sg_ref, su_ref, sd_ref,
                    ys_ref, o_ref, buf_ref, sem):
    c = ROW_TILES
    tm = hp_ref.shape[0] // PACKED_TILES

    def issue(t, carry):
        slot = pl.ds(pl.multiple_of(t * c, c), c)
        for k in range(TOP_K):
            src = pl.multiple_of(dest_ref[0, 0, t * TOP_K + k] * c, c)
            pltpu.make_async_copy(ys_ref.at[pl.ds(src, c)], buf_ref.at[k, slot], sem).start(priority=k % 2)
        return carry

    lax.fori_loop(0, tm, issue, 0)
    hb = _unpack_bf16_pairs(_from_row_tiles(hp_ref, PACKED_TILES))
    hid = (_silu(jnp.dot(hb, sg_ref[...], preferred_element_type=F32))
           * jnp.dot(hb, su_ref[...], preferred_element_type=F32))
    f = jnp.dot(hid.astype(BF16), sd_ref[...], preferred_element_type=F32)
    for k in range(TOP_K):
        pltpu.make_async_copy(ys_ref.at[pl.ds(0, tm * c)], buf_ref.at[k], sem).wait()

    def weighted(t, carry):
        slot = pl.ds(pl.multiple_of(t * c, c), c)
        acc = gate_ref[0, 0, t * TOP_K] * buf_ref[0, slot, :]
        for k in range(1, TOP_K):
            acc = acc + gate_ref[0, 0, t * TOP_K + k] * buf_ref[k, slot, :]
        buf_ref[0, slot, :] = acc
        return carry

    lax.fori_loop(0, tm, weighted, 0, unroll=4)
    f = f + _from_row_tiles(buf_ref, c, lead=(0,))
    o_ref[...] = x_ref[...] + gf_ref[0] * _rms(f, g3_ref[...])


def _combine(lay, dest_flat, gate_flat, hp, x_mid, mods, g3, s_gate, s_up, s_down, ys, lat_only, compact_out):
    row = lambda i: (lay.tile(i, lat_only), 0)
    full = lambda shape: pl.BlockSpec(shape, lambda i: (0, 0))
    out_rows = lay.n_tiles(lat_only) * TM if compact_out else lay.rows
    out_map = (lambda i: (i, 0)) if compact_out else row
    return pl.pallas_call(
        _combine_kernel,
        grid=(lay.n_tiles(lat_only),),
        in_specs=[pl.BlockSpec((1, 1, TM * TOP_K), lambda i: (lay.tile(i, lat_only), 0, 0), memory_space=pltpu.SMEM),
                  pl.BlockSpec((1, 1, TM * TOP_K), lambda i: (lay.tile(i, lat_only), 0, 0), memory_space=pltpu.SMEM),
                  pl.BlockSpec((TM * PACKED_TILES, LANES), row),
                  pl.BlockSpec((TM, D_MODEL), row), _mod_spec(lay, 5, lat_only),
                  full((1, D_MODEL)), full((D_MODEL, EXPERT_HIDDEN)), full((D_MODEL, EXPERT_HIDDEN)),
                  full((EXPERT_HIDDEN, D_MODEL)), pl.BlockSpec(memory_space=pl.ANY)],
        out_specs=pl.BlockSpec((TM, D_MODEL), out_map),
        scratch_shapes=[pltpu.VMEM((TOP_K, TM * ROW_TILES, LANES), F32), pltpu.SemaphoreType.DMA],
        out_shape=jax.ShapeDtypeStruct((out_rows, D_MODEL), F32),
        compiler_params=_params(("arbitrary",)),
        name="combine",
    )(dest_flat, gate_flat, hp, x_mid, mods, g3, s_gate, s_up, s_down, ys)


def _rope_tables(lay):
    n_freq = HEAD_DIM // 4
    inv_freq = ROPE_THETA ** (-jnp.arange(n_freq, dtype=F32) / n_freq)
    t = jnp.arange(lay.n_lat)
    pos = jnp.stack([(t // GRID_W).astype(F32), (t % GRID_W).astype(F32)], axis=-1)
    ang = pos[:, :, None, None] * inv_freq
    ang = jnp.broadcast_to(ang, (lay.n_lat, 2, 2, n_freq)).reshape(lay.n_lat, HEAD_DIM)
    sign = jnp.tile(jnp.concatenate([-jnp.ones(n_freq, F32), jnp.ones(n_freq, F32)]), 2)
    cos = jnp.concatenate([jnp.ones((lay.n_ctx, HEAD_DIM), F32), jnp.cos(ang)], axis=0)
    sin = jnp.concatenate([jnp.zeros((lay.n_ctx, HEAD_DIM), F32), jnp.sin(ang) * sign], axis=0)
    return cos, sin


def _moe_plan(counts, n_blocks):
    counts = counts.reshape(-1)
    padded = (counts + MOE_BLK - 1) // MOE_BLK * MOE_BLK
    pad_end = jnp.cumsum(padded)
    pad_start = (pad_end - padded).astype(I32)
    blk_row = jnp.arange(n_blocks, dtype=I32) * MOE_BLK
    block_e = jnp.minimum(jnp.sum(blk_row[:, None] >= pad_end[None, :], axis=1), N_EXPERTS - 1).astype(I32)
    own = block_e[:, None] == jnp.arange(N_EXPERTS, dtype=I32)[None, :]
    count_b = jnp.sum(jnp.where(own, counts[None, :], 0), axis=1)
    start_b = jnp.sum(jnp.where(own, pad_start[None, :], 0), axis=1)
    block_valid = jnp.clip(count_b - (blk_row - start_b), 0, MOE_BLK).astype(I32)
    n_used = jnp.maximum(pad_end[-1:] // MOE_BLK, 1).astype(I32)
    return pad_start, block_e, block_valid, n_used


def kernel(x, c, ctx, c_ctx, mod_w, mod_b, norm_g, w_in, w_out, conv_w, conv_b, lru_wr, lru_br, lru_wi, lru_bi,
           lru_lambda, gla_w2, gla_b2, gla_norm_g, qk_norm_g, win_sink, router_w, router_bias, exp_w_gate,
           exp_w_up, exp_w_down, sh_w_gate, sh_w_up, sh_w_down):
    bsz, n_lat, d = x.shape
    n_ctx = ctx.shape[1]
    depth = mod_w.shape[0]
    assert d == D_MODEL and bsz < MOD_ROWS
    lay = _Layout(bsz, n_lat, n_ctx)

    cc = jnp.zeros((MOD_ROWS, d), F32).at[:bsz].set(c).at[bsz].set(c_ctx)
    mods_all = _adaln(cc, mod_w, mod_b)
    cos, sin = _rope_tables(lay)
    xt = jnp.concatenate([ctx, x], axis=1).reshape(lay.rows, d)

    splits = np.cumsum([0, W_A, W_B_RAW, W_C, W_D])
    for layer in range(depth):
        last = layer == depth - 1
        need_ctx = not last
        lat_only = last
        mods = mods_all[layer].reshape(MOD_ROWS * 6, 1, d)
        g = norm_g[layer]
        wl = w_in[layer]
        w_cat = jnp.concatenate(
            [wl[:, splits[0]:splits[1]], wl[:, splits[1]:splits[2]], jnp.zeros((d, W_B - W_B_RAW), F32),
             wl[:, splits[2]:splits[3]], wl[:, splits[3]:splits[4]]], axis=1).astype(BF16)
        p_a, p_b, p_c, p_d = _inproj(lay, xt, mods, g[0:1], w_cat)

        y_a = _rglru(lay, p_a, conv_w[layer], conv_b[layer], lru_wr[layer], lru_br[layer], lru_wi[layer],
                     lru_bi[layer], lru_lambda[layer])
        w2pad = jnp.zeros((2, LANES, GLA_HEADS * GLA_DK), F32)
        w2pad = w2pad.at[0, :GLA_RANK].set(gla_w2[layer, 0]).at[1, GLA_RANK:2 * GLA_RANK].set(gla_w2[layer, 1])
        y_b = _gla(lay, p_b, w2pad, gla_b2[layer], gla_norm_g[layer])
        y_c = _attention(lay, p_c, cos, sin, qk_norm_g[layer], window=False, need_ctx=need_ctx)
        y_d = _attention(lay, p_d, cos, sin, win_sink[layer], window=True, need_ctx=need_ctx)

        x_mid, h, hp = _outproj(lay, (y_a, y_b, y_c, y_d), w_out[layer].astype(BF16), xt, mods, g[1:2], g[2:3],
                                lat_only)

        top_e, gate, rank, counts = _router(lay, h, router_w[layer].T, router_bias[layer], lat_only)
        n_assign = lay.n_tiles(lat_only) * TM * TOP_K
        n_blocks = -(-(n_assign + N_EXPERTS * (MOE_BLK - 1)) // MOE_BLK)
        pad_start, block_e, block_valid, n_used = _moe_plan(counts, n_blocks)
        owner = top_e[:, :, None] == jnp.arange(N_EXPERTS, dtype=I32)
        dest = jnp.sum(jnp.where(owner, pad_start, 0), axis=-1) + rank
        dest_flat = dest.T.reshape(lay.rows // TM, 1, TM * TOP_K)
        xs = _dispatch(lay, dest_flat, hp, n_blocks * MOE_BLK, lat_only)
        ys = _experts(block_e, block_valid, n_used, xs, exp_w_gate, exp_w_up, exp_w_down, layer)
        gate_flat = gate.T.reshape(lay.rows // TM, 1, TM * TOP_K)
        xt = _combine(lay, dest_flat, gate_flat, hp, x_mid, mods, g[3:4], sh_w_gate[layer].astype(BF16),
                      sh_w_up[layer].astype(BF16), sh_w_down[layer].astype(BF16), ys, lat_only, compact_out=last)
    return xt.reshape(bsz, n_lat, d)
```

```python
import functools

import numpy as np
import jax
import jax.numpy as jnp
from jax import lax
from jax.experimental import pallas as pl
from jax.experimental.pallas import tpu as pltpu

F32 = jnp.float32
BF16 = jnp.bfloat16
I32 = jnp.int32
U32 = jnp.uint32
HIGHEST = lax.Precision.HIGHEST
NT_DIMS = (((1,), (1,)), ((), ()))
TN_DIMS = (((0,), (0,)), ((), ()))

LANES = 128
SUBLANES = 8
VMEM_LIMIT_BYTES = 56 * 1024 * 1024

D_MODEL = 2048
GRID_W = 64
HEAD_DIM = 128
GROUP_WIDTH = D_MODEL // 4
LRU_BLOCKS = 4
CONV_W = 4
RGLRU_C = 8.0
GLA_HEADS = 4
GLA_DK = 64
GLA_DV = 128
GLA_RANK = 16
GLA_TAU = 16.0
GLA_CHUNK = 64
KV_HEADS = 2
WINDOW = 128
N_EXPERTS = 64
N_GROUPS = 8
TOPK_GROUPS = 4
TOP_K = 8
EXPERT_HIDDEN = 512
ROUTED_SCALE = 2.5
RMS_EPS = 1e-6
ROPE_THETA = 10000.0
NEG_INF = -1e30

W_A = 2 * GROUP_WIDTH
W_B_RAW = 2 * GLA_HEADS * GLA_DK + 2 * GLA_HEADS * GLA_DV + 2 * GLA_RANK
W_B = 1664
W_C = GROUP_WIDTH + 2 * KV_HEADS * HEAD_DIM
W_D = W_C
Z_COL_BLOCK = (2 * GLA_HEADS * GLA_DK + 2 * GLA_HEADS * GLA_DV) // LANES

TM = 256
MOE_BLK = 512
PACKED_TILES = D_MODEL // 2 // LANES
ROW_TILES = D_MODEL // LANES
MOD_ROWS = 16


def _params(sem, vmem=VMEM_LIMIT_BYTES):
    return pltpu.CompilerParams(dimension_semantics=sem, vmem_limit_bytes=vmem)


def _rms(x, g):
    return x * lax.rsqrt(jnp.mean(x * x, axis=-1, keepdims=True) + RMS_EPS) * g


def _silu(x):
    return x * jax.nn.sigmoid(x)


def _pack_bf16_pairs(h):
    n = h.shape[1] // 2
    hb = h.astype(BF16).astype(F32)
    hi = lax.bitcast_convert_type(hb[:, :n], U32)
    lo = lax.bitcast_convert_type(hb[:, n:], U32)
    return hi | (lo >> 16)


def _to_row_tiles(ref, x):
    m = x.shape[0]
    c = x.shape[1] // LANES
    for j in range(c):
        ref[pl.ds(j, m, stride=c), :] = x[:, j * LANES:(j + 1) * LANES]


def _from_row_tiles(ref, c, lead=()):
    m = ref.shape[-2] // c
    return jnp.concatenate([ref[lead + (pl.ds(j, m, stride=c), slice(None))] for j in range(c)], axis=1)


def _unpack_bf16_pairs(p):
    hi = lax.bitcast_convert_type(p & jnp.uint32(0xFFFF0000), F32)
    lo = lax.bitcast_convert_type(p << 16, F32)
    return jnp.concatenate([hi, lo], axis=1).astype(BF16)


def _adaln_kernel(c_ref, w_ref, b_ref, o_ref):
    a = _silu(c_ref[...])
    o_ref[0] = jnp.dot(a, w_ref[0], preferred_element_type=F32, precision=HIGHEST) + b_ref[0]


def _adaln(cc, mod_w, mod_b):
    depth, d, n = mod_w.shape
    tn = 1024
    return pl.pallas_call(
        _adaln_kernel,
        grid=(depth, n // tn),
        in_specs=[pl.BlockSpec((MOD_ROWS, d), lambda l, j: (0, 0)),
                  pl.BlockSpec((1, d, tn), lambda l, j: (l, 0, j)),
                  pl.BlockSpec((1, 1, tn), lambda l, j: (l, 0, j))],
        out_specs=pl.BlockSpec((1, MOD_ROWS, tn), lambda l, j: (l, 0, j)),
        out_shape=jax.ShapeDtypeStruct((depth, MOD_ROWS, n), F32),
        compiler_params=_params(("arbitrary", "arbitrary")),
        name="adaln",
    )(cc, mod_w, mod_b.reshape(depth, 1, n))


class _Layout:
    def __init__(self, bsz, n_lat, n_ctx):
        self.bsz, self.n_lat, self.n_ctx = bsz, n_lat, n_ctx
        self.seq = n_ctx + n_lat
        self.rows = bsz * self.seq
        assert n_ctx % TM == 0 and n_lat % TM == 0
        self.tiles_b = self.seq // TM
        self.ctx_tiles = n_ctx // TM
        self.lat_tiles = n_lat // TM

    def n_tiles(self, lat_only):
        return self.bsz * (self.lat_tiles if lat_only else self.tiles_b)

    def tile(self, i, lat_only):
        if lat_only:
            return (i // self.lat_tiles) * self.tiles_b + self.ctx_tiles + i % self.lat_tiles
        return i

    def mod_row(self, i, lat_only):
        if lat_only:
            return i // self.lat_tiles
        return jnp.where(i % self.tiles_b < self.ctx_tiles, self.bsz, i // self.tiles_b)


def _mod_spec(lay, chunk, lat_only):
    return pl.BlockSpec((1, 1, D_MODEL), lambda i: (lay.mod_row(i, lat_only) * 6 + chunk, 0, 0))


def _inproj_kernel(x_ref, sh_ref, sc_ref, g_ref, w_ref, oa_ref, ob_ref, oc_ref, od_ref):
    h = _rms(x_ref[...], g_ref[...]) * (1.0 + sc_ref[0]) + sh_ref[0]
    hb = h.astype(BF16)
    c0 = 0
    for o_ref in (oa_ref, ob_ref, oc_ref, od_ref):
        width = o_ref.shape[1]
        o_ref[...] = jnp.dot(hb, w_ref[:, c0:c0 + width], preferred_element_type=F32)
        c0 += width


def _inproj(lay, xt, mods, g0, w_cat):
    widths = (W_A, W_B, W_C, W_D)
    row = lambda i: (i, 0)
    return pl.pallas_call(
        _inproj_kernel,
        grid=(lay.n_tiles(False),),
        in_specs=[pl.BlockSpec((TM, D_MODEL), row),
                  _mod_spec(lay, 0, False), _mod_spec(lay, 1, False),
                  pl.BlockSpec((1, D_MODEL), lambda i: (0, 0)),
                  pl.BlockSpec((D_MODEL, sum(widths)), lambda i: (0, 0), pipeline_mode=pl.Buffered(1))],
        out_specs=[pl.BlockSpec((TM, w), row) for w in widths],
        out_shape=[jax.ShapeDtypeStruct((lay.rows, w), F32) for w in widths],
        compiler_params=_params(("arbitrary",)),
        name="inproj",
    )(xt, mods, mods, g0, w_cat)


def _rglru_kernel(x_ref, g_ref, cw_ref, cb_ref, wr_ref, br_ref, wi_ref, bi_ref, lam_ref, o_ref,
                  xs_ref, a_ref, b_ref, h_ref, *, n_ctx):
    seq = x_ref.shape[0]
    pad = SUBLANES
    n_tiles = seq // SUBLANES
    ctx_tiles = n_ctx // SUBLANES
    row = lax.broadcasted_iota(I32, (seq, LANES), 0)

    xs_ref[0:pad, :] = jnp.zeros((pad, LANES), F32)
    xs_ref[pad + seq:2 * pad + seq, :] = jnp.zeros((pad, LANES), F32)
    xs_ref[pad:pad + seq, :] = x_ref[...]
    u = jnp.zeros((seq, LANES), F32) + cb_ref[...]
    for j in range(CONV_W):
        off = j - 2
        xsh = xs_ref[pad + off:pad + off + seq, :]
        if off < 0:
            xsh = jnp.where((row >= n_ctx) & (row + off < n_ctx), 0.0, xsh)
        elif off > 0:
            xsh = jnp.where((row < n_ctx) & (row + off >= n_ctx), 0.0, xsh)
        u = u + cw_ref[j:j + 1, :] * xsh

    ub = u.astype(BF16)
    tile_shape = (n_tiles, SUBLANES, LANES)
    rmod = lax.broadcasted_iota(I32, tile_shape, 1)
    for d in range(2):
        reverse = d == 1
        r = jax.nn.sigmoid(jnp.dot(ub, wr_ref[d, 0].astype(BF16), preferred_element_type=F32) + br_ref[d:d + 1, :])
        gi = jax.nn.sigmoid(jnp.dot(ub, wi_ref[d, 0].astype(BF16), preferred_element_type=F32) + bi_ref[d:d + 1, :])
        lam = lam_ref[d:d + 1, :]
        softplus_neg = jnp.maximum(-lam, 0.0) + jnp.log(1.0 + jnp.exp(-jnp.abs(lam)))
        a = jnp.exp(-RGLRU_C * r * softplus_neg)
        b = jnp.sqrt(1.0 - a * a) * (gi * u)
        a = a.reshape(tile_shape)
        b = b.reshape(tile_shape)
        for s in (1, 2, 4):
            if reverse:
                ap, bp, ok = pltpu.roll(a, SUBLANES - s, 1), pltpu.roll(b, SUBLANES - s, 1), rmod < SUBLANES - s
            else:
                ap, bp, ok = pltpu.roll(a, s, 1), pltpu.roll(b, s, 1), rmod >= s
            b = jnp.where(ok, a * bp + b, b)
            a = jnp.where(ok, a * ap, a)
        a_ref[...] = a.reshape(seq, LANES)
        b_ref[...] = b.reshape(seq, LANES)

        def step(t, h, reverse=reverse):
            r0 = pl.multiple_of(t * SUBLANES, SUBLANES)
            h8 = a_ref[pl.ds(r0, SUBLANES), :] * h + b_ref[pl.ds(r0, SUBLANES), :]
            if reverse:
                h_ref[pl.ds(r0, SUBLANES), :] = h_ref[pl.ds(r0, SUBLANES), :] + h8
                return jnp.broadcast_to(h8[0:1, :], (SUBLANES, LANES))
            h_ref[pl.ds(r0, SUBLANES), :] = h8
            return jnp.broadcast_to(h8[SUBLANES - 1:SUBLANES, :], (SUBLANES, LANES))

        h0 = jnp.zeros((SUBLANES, LANES), F32)
        if reverse:
            hc = lax.fori_loop(0, ctx_tiles, lambda i, h: step(ctx_tiles - 1 - i, h), h0)
            lax.fori_loop(0, n_tiles - ctx_tiles, lambda i, h: step(n_tiles - 1 - i, h), hc)
        else:
            lax.fori_loop(0, n_tiles, step, h0)

    g = g_ref[...]
    gelu = 0.5 * g * (1.0 + jnp.tanh(np.sqrt(2.0 / np.pi) * (g + 0.044715 * (g * g * g))))
    o_ref[...] = h_ref[...] * gelu


def _rglru(lay, p_a, conv_w, conv_b, wr, br, wi, bi, lam):
    seq = lay.seq
    nb = LRU_BLOCKS
    col = lambda off: (lambda b, j: (b, off + j))
    vec = lambda rows: pl.BlockSpec((rows, LANES), lambda b, j: (0, j))
    wspec = pl.BlockSpec((2, 1, LANES, LANES), lambda b, j: (0, j, 0, 0))
    return pl.pallas_call(
        functools.partial(_rglru_kernel, n_ctx=lay.n_ctx),
        grid=(lay.bsz, nb),
        in_specs=[pl.BlockSpec((seq, LANES), col(0)), pl.BlockSpec((seq, LANES), col(nb)),
                  vec(CONV_W), vec(1), wspec, vec(2), wspec, vec(2), vec(2)],
        out_specs=pl.BlockSpec((seq, LANES), col(0)),
        out_shape=jax.ShapeDtypeStruct((lay.rows, GROUP_WIDTH), F32),
        scratch_shapes=[pltpu.VMEM((seq + 2 * SUBLANES, LANES), F32), pltpu.VMEM((seq, LANES), F32),
                        pltpu.VMEM((seq, LANES), F32), pltpu.VMEM((seq, LANES), F32)],
        compiler_params=_params(("arbitrary", "arbitrary")),
        name="rglru",
    )(p_a, p_a, conv_w, conv_b.reshape(1, -1), wr, br, wi, bi, lam)


def _gla_kernel(q_ref, k_ref, v_ref, g_ref, z_ref, w2_ref, b2_ref, ng_ref, o_ref,
                laf_ref, lab_ref, of_ref, ob_ref, *, n_ctx):
    seq = q_ref.shape[0]
    cs = GLA_CHUNK
    n_chunks = seq // cs
    ctx_chunks = n_ctx // cs
    z = z_ref[...]
    for d, la_ref in ((0, laf_ref), (1, lab_ref)):
        pre = jnp.dot(z, w2_ref[d], preferred_element_type=F32, precision=HIGHEST) + b2_ref[d:d + 1, :]
        log_sig = jnp.minimum(pre, 0.0) - jnp.log(1.0 + jnp.exp(-jnp.abs(pre)))
        la_ref[...] = log_sig * (1.0 / GLA_TAU)

    ri = lax.broadcasted_iota(I32, (cs, cs), 0)
    ci = lax.broadcasted_iota(I32, (cs, cs), 1)
    scale = GLA_DK ** -0.5

    def chunk(c, states, reverse):
        r0 = pl.multiple_of(c * cs, cs)
        rows = pl.ds(r0, cs)
        keep = (ri <= ci) if reverse else (ri >= ci)
        la = (lab_ref if reverse else laf_ref)[rows, :]
        la_hi = la.astype(BF16)
        rest = la - la_hi.astype(F32)
        la_mid = rest.astype(BF16)
        la_lo = (rest - la_mid.astype(F32)).astype(BF16)
        parts = jnp.dot(keep.astype(BF16), jnp.concatenate([la_hi, la_mid, la_lo], axis=1),
                        preferred_element_type=F32)
        cum = parts[:, 0:LANES] + parts[:, LANES:2 * LANES] + parts[:, 2 * LANES:3 * LANES]
        last = cum[0:1, :] if reverse else cum[cs - 1:cs, :]
        q = q_ref[rows, :] * scale
        k = k_ref[rows, :]
        qe = q * jnp.exp(cum)
        ke = k * jnp.exp(-cum)
        kl = k * jnp.exp(last - cum)
        el = jnp.exp(last)
        out_ref = ob_ref if reverse else of_ref
        new_states = []
        for hh in range(2):
            dk = slice(hh * GLA_DK, (hh + 1) * GLA_DK)
            dv = slice(hh * GLA_DV, (hh + 1) * GLA_DV)
            qh = qe[:, dk].astype(BF16)
            vh = v_ref[rows, dv].astype(BF16)
            st = states[hh]
            s = lax.dot_general(qh, ke[:, dk].astype(BF16), NT_DIMS, preferred_element_type=F32)
            s = jnp.where(keep, s, 0.0)
            o = (jnp.dot(s.astype(BF16), vh, preferred_element_type=F32)
                 + lax.dot_general(qh, st.astype(BF16), NT_DIMS, preferred_element_type=F32))
            out_ref[rows, dv] = o
            new_states.append(st * el[:, dk]
                              + lax.dot_general(vh, kl[:, dk].astype(BF16), TN_DIMS, preferred_element_type=F32))
        return tuple(new_states)

    def body(i, carry):
        sf, sb = carry
        sf = chunk(i, sf, False)
        cb = jnp.where(i < ctx_chunks, ctx_chunks - 1 - i, n_chunks - 1 + ctx_chunks - i)
        sb = chunk(cb, sb, True)
        return sf, sb

    zero = jnp.zeros((GLA_DV, GLA_DK), F32)
    lax.fori_loop(0, n_chunks, body, ((zero, zero), (zero, zero)))

    ng = ng_ref[...]
    for hh in range(2):
        dv = slice(hh * GLA_DV, (hh + 1) * GLA_DV)
        o = of_ref[:, dv] + ob_ref[:, dv]
        o_ref[:, dv] = _rms(o, ng) * _silu(g_ref[:, dv])


def _gla(lay, p_b, w2pad, b2, norm_g):
    seq = lay.seq
    hp = GLA_HEADS // 2
    qk_w = 2 * GLA_DK
    v_w = 2 * GLA_DV
    k_off = GLA_HEADS * GLA_DK // qk_w
    v_off = 2 * GLA_HEADS * GLA_DK // v_w
    g_off = v_off + GLA_HEADS * GLA_DV // v_w
    blk = lambda w, off: pl.BlockSpec((seq, w), lambda b, j: (b, off + j))
    return pl.pallas_call(
        functools.partial(_gla_kernel, n_ctx=lay.n_ctx),
        grid=(lay.bsz, hp),
        in_specs=[blk(qk_w, 0), blk(qk_w, k_off), blk(v_w, v_off), blk(v_w, g_off),
                  pl.BlockSpec((seq, LANES), lambda b, j: (b, Z_COL_BLOCK)),
                  pl.BlockSpec((2, LANES, qk_w), lambda b, j: (0, 0, j)),
                  pl.BlockSpec((2, qk_w), lambda b, j: (0, j)),
                  pl.BlockSpec((1, GLA_DV), lambda b, j: (0, 0))],
        out_specs=blk(v_w, 0),
        out_shape=jax.ShapeDtypeStruct((lay.rows, GROUP_WIDTH), F32),
        scratch_shapes=[pltpu.VMEM((seq, qk_w), F32), pltpu.VMEM((seq, qk_w), F32),
                        pltpu.VMEM((seq, v_w), F32), pltpu.VMEM((seq, v_w), F32)],
        compiler_params=_params(("arbitrary", "arbitrary")),
        name="gla",
    )(p_b, p_b, p_b, p_b, p_b, w2pad, b2, norm_g.reshape(1, -1))


def _rope(x, cos, sin_signed):
    lane = lax.broadcasted_iota(I32, x.shape, 1)
    partner = jnp.where((lane & 32) != 0, pltpu.roll(x, 32, 1), pltpu.roll(x, HEAD_DIM - 32, 1))
    return x * cos + partner * sin_signed


def _attn_kernel(*refs, window, need_ctx, tq, n_ctx):
    if window:
        sink_ref, q_ref, k_ref, v_ref, cos_ref, sin_ref, o_ref, ks_ref, vs_ref = refs
        qkg_ref = None
    else:
        q_ref, k_ref, v_ref, cos_ref, sin_ref, qkg_ref, o_ref, ks_ref, vs_ref = refs
    seq = k_ref.shape[0]
    n_lat = seq - n_ctx
    g = pl.program_id(1)
    qi = pl.program_id(2)
    jq = qi if need_ctx else qi + n_ctx // tq

    @pl.when(qi == 0)
    def _():
        k = k_ref[...]
        if not window:
            k = _rms(k, qkg_ref[1:2, :])
        ks_ref[...] = _rope(k, cos_ref[...], sin_ref[...]).astype(BF16)
        vs_ref[...] = v_ref[...].astype(BF16)

    r0 = pl.multiple_of(jq * tq, tq)
    cos = cos_ref[pl.ds(r0, tq), :]
    sin = sin_ref[pl.ds(r0, tq), :]
    heads = []
    for hh in range(2):
        qh = q_ref[:, hh * HEAD_DIM:(hh + 1) * HEAD_DIM]
        if not window:
            qh = _rms(qh, qkg_ref[0:1, :])
        heads.append((_rope(qh, cos, sin) * HEAD_DIM ** -0.5).astype(BF16))
    q2 = jnp.concatenate(heads, axis=0)

    def finish(o):
        o_ref[:, 0:HEAD_DIM] = o[0:tq]
        o_ref[:, HEAD_DIM:2 * HEAD_DIM] = o[tq:2 * tq]

    def scores(lo, n):
        return lax.dot_general(q2, ks_ref[pl.ds(lo, n), :], NT_DIMS, preferred_element_type=F32)

    def pv(p, lo, n):
        return jnp.dot(p.astype(BF16), vs_ref[pl.ds(lo, n), :], preferred_element_type=F32)

    if window:
        rowi = lax.broadcasted_iota(I32, (2 * tq, 1), 0)
        sink = jnp.where(rowi < tq, sink_ref[2 * g], sink_ref[2 * g + 1])

    def ctx_tile():
        s = scores(0, n_ctx)
        m = jnp.max(s, axis=-1, keepdims=True)
        if window:
            m = jnp.maximum(m, sink)
        p = jnp.exp(s - m)
        l = jnp.sum(p, axis=-1, keepdims=True)
        if window:
            l = l + jnp.exp(sink - m)
        finish(pv(p, 0, n_ctx) / l)

    def lat_tile():
        if not window:
            s = scores(0, seq)
            m = jnp.max(s, axis=-1, keepdims=True)
            p = jnp.exp(s - m)
            finish(pv(p, 0, seq) / jnp.sum(p, axis=-1, keepdims=True))
            return
        band = 3 * WINDOW
        jl = jq - n_ctx // tq
        start = jnp.clip((jl - 1) * WINDOW, 0, n_lat - band)
        lo = pl.multiple_of(n_ctx + start, WINDOW)
        sb = scores(lo, band)
        q_pos = jl * WINDOW + lax.broadcasted_iota(I32, (2 * tq, band), 0) % tq
        k_pos = start + lax.broadcasted_iota(I32, (2 * tq, band), 1)
        sb = jnp.where(jnp.abs(q_pos - k_pos) <= WINDOW, sb, NEG_INF)
        sc = scores(0, n_ctx)
        m = jnp.maximum(jnp.maximum(jnp.max(sb, axis=-1, keepdims=True), jnp.max(sc, axis=-1, keepdims=True)), sink)
        pb = jnp.exp(sb - m)
        pc = jnp.exp(sc - m)
        l = jnp.sum(pb, axis=-1, keepdims=True) + jnp.sum(pc, axis=-1, keepdims=True) + jnp.exp(sink - m)
        finish((pv(pb, lo, band) + pv(pc, 0, n_ctx)) / l)

    if need_ctx:
        pl.when(jq * tq < n_ctx)(ctx_tile)
        pl.when(jq * tq >= n_ctx)(lat_tile)
    else:
        lat_tile()


def _attention(lay, p, cos, sin, extra, *, window, need_ctx):
    seq = lay.seq
    tq = WINDOW if window else TM
    q_tiles_b = seq // tq
    nq = q_tiles_b if need_ctx else lay.n_lat // tq
    skip = 0 if need_ctx else lay.n_ctx // tq
    q_heads_w = 2 * HEAD_DIM
    k_off = GROUP_WIDTH // HEAD_DIM
    v_off = k_off + KV_HEADS

    def im(f):
        return (lambda b, g, qi, *_: f(b, g, qi))

    in_specs = [pl.BlockSpec((tq, q_heads_w), im(lambda b, g, qi: (b * q_tiles_b + skip + qi, g))),
                pl.BlockSpec((seq, HEAD_DIM), im(lambda b, g, qi: (b, k_off + g))),
                pl.BlockSpec((seq, HEAD_DIM), im(lambda b, g, qi: (b, v_off + g))),
                pl.BlockSpec((seq, HEAD_DIM), im(lambda b, g, qi: (0, 0))),
                pl.BlockSpec((seq, HEAD_DIM), im(lambda b, g, qi: (0, 0)))]
    args = [p, p, p, cos, sin]
    if not window:
        in_specs.append(pl.BlockSpec((2, HEAD_DIM), im(lambda b, g, qi: (0, 0))))
        args.append(extra)
    out_spec = pl.BlockSpec((tq, q_heads_w), im(lambda b, g, qi: (b * q_tiles_b + skip + qi, g)))
    kern = functools.partial(_attn_kernel, window=window, need_ctx=need_ctx, tq=tq, n_ctx=lay.n_ctx)
    scratch = [pltpu.VMEM((seq, HEAD_DIM), BF16), pltpu.VMEM((seq, HEAD_DIM), BF16)]
    out_shape = jax.ShapeDtypeStruct((lay.rows, GROUP_WIDTH), F32)
    sem = ("arbitrary", "arbitrary", "arbitrary")
    grid = (lay.bsz, KV_HEADS, nq)
    if window:
        return pl.pallas_call(
            kern,
            grid_spec=pltpu.PrefetchScalarGridSpec(num_scalar_prefetch=1, grid=grid, in_specs=in_specs,
                                                   out_specs=out_spec, scratch_shapes=scratch),
            out_shape=out_shape, compiler_params=_params(sem), name="win_attn",
        )(extra, *args)
    return pl.pallas_call(kern, grid=grid, in_specs=in_specs, out_specs=out_spec, scratch_shapes=scratch,
                          out_shape=out_shape, compiler_params=_params(sem), name="glob_attn")(*args)


def _outproj_kernel(ya_ref, yb_ref, yc_ref, yd_ref, w_ref, x_ref, gm_ref, sh_ref, sc_ref, g1_ref, g2_ref,
                    xo_ref, h_ref, hp_ref):
    acc = None
    for i, y_ref in enumerate((ya_ref, yb_ref, yc_ref, yd_ref)):
        part = jnp.dot(y_ref[...].astype(BF16), w_ref[i * GROUP_WIDTH:(i + 1) * GROUP_WIDTH, :],
                       preferred_element_type=F32)
        acc = part if acc is None else acc + part
    x_new = x_ref[...] + gm_ref[0] * _rms(acc, g1_ref[...])
    xo_ref[...] = x_new
    h = _rms(x_new, g2_ref[...]) * (1.0 + sc_ref[0]) + sh_ref[0]
    h_ref[...] = h
    _to_row_tiles(hp_ref, _pack_bf16_pairs(h))


def _outproj(lay, ys, w_out, xt, mods, g1, g2, lat_only):
    row = lambda i: (lay.tile(i, lat_only), 0)
    full = lambda shape: pl.BlockSpec(shape, lambda i: (0, 0))
    yspec = pl.BlockSpec((TM, GROUP_WIDTH), row)
    xspec = pl.BlockSpec((TM, D_MODEL), row)
    return pl.pallas_call(
        _outproj_kernel,
        grid=(lay.n_tiles(lat_only),),
        in_specs=[yspec, yspec, yspec, yspec, full((D_MODEL, D_MODEL)), xspec,
                  _mod_spec(lay, 2, lat_only), _mod_spec(lay, 3, lat_only), _mod_spec(lay, 4, lat_only),
                  full((1, D_MODEL)), full((1, D_MODEL))],
        out_specs=[xspec, xspec, pl.BlockSpec((TM * PACKED_TILES, LANES), row)],
        out_shape=[jax.ShapeDtypeStruct((lay.rows, D_MODEL), F32), jax.ShapeDtypeStruct((lay.rows, D_MODEL), F32),
                   jax.ShapeDtypeStruct((lay.rows * PACKED_TILES, LANES), U32)],
        compiler_params=_params(("arbitrary",)),
        name="outproj",
    )(*ys, w_out, xt, mods, mods, mods, g1, g2)


def _route(h, wt_ref, bias_ref, e_ref, gate_ref, rank_ref, cnt_ref, carry_ref):
    i = pl.program_id(0)
    tm = h.shape[0]
    ne = N_EXPERTS
    gsz = ne // N_GROUPS

    @pl.when(i == 0)
    def _():
        carry_ref[...] = jnp.zeros_like(carry_ref)

    logits = lax.dot_general(wt_ref[...], h, NT_DIMS, preferred_element_type=F32, precision=HIGHEST)
    scores = jax.nn.sigmoid(logits)
    biased = scores + bias_ref[...]

    sub = lax.broadcasted_iota(I32, (gsz, tm), 0)
    gs_rows = []
    for grp in range(N_GROUPS):
        blk = biased[grp * gsz:(grp + 1) * gsz, :]
        m1 = jnp.max(blk, axis=0, keepdims=True)
        first = jnp.min(jnp.where(blk == m1, sub, gsz), axis=0, keepdims=True)
        m2 = jnp.max(jnp.where(sub == first, -jnp.inf, blk), axis=0, keepdims=True)
        gs_rows.append(m1 + m2)
    gs = jnp.concatenate(gs_rows, axis=0)

    gidx = lax.broadcasted_iota(I32, (N_GROUPS, tm), 0)
    chosen = jnp.zeros((N_GROUPS, tm), jnp.bool_)
    cur = gs
    for _ in range(TOPK_GROUPS):
        m = jnp.max(cur, axis=0, keepdims=True)
        first = jnp.min(jnp.where(cur == m, gidx, N_GROUPS), axis=0, keepdims=True)
        sel = gidx == first
        chosen = chosen | sel
        cur = jnp.where(sel, -jnp.inf, cur)

    eidx = lax.broadcasted_iota(I32, (ne, tm), 0)
    eligible = jnp.concatenate(
        [jnp.broadcast_to(chosen[grp:grp + 1, :], (gsz, tm)) for grp in range(N_GROUPS)], axis=0)
    cur = jnp.where(eligible, biased, -jnp.inf)
    sels, gates = [], []
    onehot = jnp.zeros((ne, tm), jnp.bool_)
    for k in range(TOP_K):
        m = jnp.max(cur, axis=0, keepdims=True)
        first = jnp.min(jnp.where(cur == m, eidx, ne), axis=0, keepdims=True)
        sel = eidx == first
        e_ref[k:k + 1, :] = first
        gates.append(jnp.sum(jnp.where(sel, scores, 0.0), axis=0, keepdims=True))
        sels.append(sel)
        onehot = onehot | sel
        cur = jnp.where(sel, -jnp.inf, cur)
    total = gates[0]
    for gk in gates[1:]:
        total = total + gk
    for k in range(TOP_K):
        gate_ref[k:k + 1, :] = gates[k] / total * ROUTED_SCALE

    tr = lax.broadcasted_iota(I32, (tm, tm), 0)
    tc = lax.broadcasted_iota(I32, (tm, tm), 1)
    before = (tr < tc).astype(BF16)
    oh = onehot.astype(F32)
    prefix = jnp.dot(oh.astype(BF16), before, preferred_element_type=F32) + carry_ref[...]
    for k in range(TOP_K):
        rank_ref[k:k + 1, :] = jnp.sum(jnp.where(sels[k], prefix, 0.0), axis=0, keepdims=True).astype(I32)
    carry = carry_ref[...] + jnp.sum(oh, axis=1, keepdims=True)
    carry_ref[...] = carry
    cnt_ref[...] = carry.astype(I32)


def _router_kernel(h_ref, wt_ref, bias_ref, e_ref, gate_ref, rank_ref, cnt_ref, carry_ref):
    _route(h_ref[...], wt_ref, bias_ref, e_ref, gate_ref, rank_ref, cnt_ref, carry_ref)


def _router(lay, h, router_wt, bias, lat_only):
    col = lambda i: (0, lay.tile(i, lat_only))
    kt = pl.BlockSpec((TOP_K, TM), col)
    return pl.pallas_call(
        _router_kernel,
        grid=(lay.n_tiles(lat_only),),
        in_specs=[pl.BlockSpec((TM, D_MODEL), lambda i: (lay.tile(i, lat_only), 0)),
                  pl.BlockSpec((N_EXPERTS, D_MODEL), lambda i: (0, 0)),
                  pl.BlockSpec((N_EXPERTS, 1), lambda i: (0, 0))],
        out_specs=[kt, kt, kt, pl.BlockSpec((N_EXPERTS, 1), lambda i: (0, 0))],
        out_shape=[jax.ShapeDtypeStruct((TOP_K, lay.rows), I32), jax.ShapeDtypeStruct((TOP_K, lay.rows), F32),
                   jax.ShapeDtypeStruct((TOP_K, lay.rows), I32), jax.ShapeDtypeStruct((N_EXPERTS, 1), I32)],
        scratch_shapes=[pltpu.VMEM((N_EXPERTS, 1), F32)],
        compiler_params=_params(("arbitrary",)),
        name="router",
    )(h, router_wt, bias.reshape(-1, 1))


def _dispatch_kernel(dest_ref, hp_ref, xs_ref, sem):
    c = PACKED_TILES
    tm = hp_ref.shape[0] // c

    def issue(t, carry):
        src = hp_ref.at[pl.ds(pl.multiple_of(t * c, c), c)]
        for k in range(TOP_K):
            dest = pl.multiple_of(dest_ref[0, 0, t * TOP_K + k] * c, c)
            pltpu.make_async_copy(src, xs_ref.at[pl.ds(dest, c)], sem).start(priority=k % 2)
        return carry

    lax.fori_loop(0, tm, issue, 0)
    for k in range(TOP_K):
        pltpu.make_async_copy(hp_ref, xs_ref.at[pl.ds(0, tm * c)], sem).wait()


def _dispatch(lay, dest_flat, hp, n_rows, lat_only):
    return pl.pallas_call(
        _dispatch_kernel,
        grid=(lay.n_tiles(lat_only),),
        in_specs=[pl.BlockSpec((1, 1, TM * TOP_K), lambda i: (lay.tile(i, lat_only), 0, 0), memory_space=pltpu.SMEM),
                  pl.BlockSpec((TM * PACKED_TILES, LANES), lambda i: (lay.tile(i, lat_only), 0))],
        out_specs=pl.BlockSpec(memory_space=pl.ANY),
        scratch_shapes=[pltpu.SemaphoreType.DMA],
        out_shape=jax.ShapeDtypeStruct((n_rows * PACKED_TILES, LANES), U32),
        compiler_params=_params(("arbitrary",)),
        name="dispatch",
    )(dest_flat, hp)


def _experts_kernel(be_ref, bv_ref, nu_ref, xs_ref, wg_ref, wu_ref, wd_ref, ys_ref, wgb_ref, wub_ref, wdb_ref):
    j = pl.program_id(0)
    live = j < nu_ref[0]
    prev = be_ref[jnp.maximum(j - 1, 0)]

    @pl.when(live & ((j == 0) | (be_ref[j] != prev)))
    def _():
        wgb_ref[...] = wg_ref[0, 0].astype(BF16)
        wub_ref[...] = wu_ref[0, 0].astype(BF16)
        wdb_ref[...] = wd_ref[0, 0].astype(BF16)

    @pl.when(live)
    def _():
        packed = _from_row_tiles(xs_ref, PACKED_TILES)
        rowi = lax.broadcasted_iota(I32, packed.shape, 0)
        packed = jnp.where(rowi < bv_ref[j], packed, jnp.uint32(0))
        x = _unpack_bf16_pairs(packed)
        hid = (_silu(jnp.dot(x, wgb_ref[...], preferred_element_type=F32))
               * jnp.dot(x, wub_ref[...], preferred_element_type=F32))
        _to_row_tiles(ys_ref, jnp.dot(hid.astype(BF16), wdb_ref[...], preferred_element_type=F32))


def _experts(block_e, block_valid, n_used, xs, w_gate, w_up, w_down, layer):
    n_blocks = xs.shape[0] // (MOE_BLK * PACKED_TILES)
    last = lambda j, nu: jnp.minimum(j, nu[0] - 1)
    wmap = lambda j, be, bv, nu: (layer, be[last(j, nu)], 0, 0)
    return pl.pallas_call(
        _experts_kernel,
        grid_spec=pltpu.PrefetchScalarGridSpec(
            num_scalar_prefetch=3, grid=(n_blocks,),
            in_specs=[pl.BlockSpec((MOE_BLK * PACKED_TILES, LANES), lambda j, be, bv, nu: (last(j, nu), 0)),
                      pl.BlockSpec((1, 1, D_MODEL, EXPERT_HIDDEN), wmap),
                      pl.BlockSpec((1, 1, D_MODEL, EXPERT_HIDDEN), wmap),
                      pl.BlockSpec((1, 1, EXPERT_HIDDEN, D_MODEL), wmap)],
            out_specs=pl.BlockSpec((MOE_BLK * ROW_TILES, LANES), lambda j, be, bv, nu: (last(j, nu), 0)),
            scratch_shapes=[pltpu.VMEM((D_MODEL, EXPERT_HIDDEN), BF16), pltpu.VMEM((D_MODEL, EXPERT_HIDDEN), BF16),
                            pltpu.VMEM((EXPERT_HIDDEN, D_MODEL), BF16)]),
        out_shape=jax.ShapeDtypeStruct((n_blocks * MOE_BLK * ROW_TILES, LANES), F32),
        compiler_params=_params(("arbitrary",)),
        name="experts",
    )(block_e, block_valid, n_used, xs, w_gate, w_up, w_down)


def _combine_kernel(dest_ref, gate_ref, hp_ref, x_ref, gf_ref, g3_ref, sg_ref, su_ref, sd_ref,
                    ys_ref, o_ref, buf_ref, sem):
    c = ROW_TILES
    tm = hp_ref.shape[0] // PACKED_TILES

    def issue(t, carry):
        slot = pl.ds(pl.multiple_of(t * c, c), c)
        for k in range(TOP_K):
            src = pl.multiple_of(dest_ref[0, 0, t * TOP_K + k] * c, c)
            pltpu.make_async_copy(ys_ref.at[pl.ds(src, c)], buf_ref.at[k, slot], sem).start(priority=k % 2)
        return carry

    lax.fori_loop(0, tm, issue, 0)
    hb = _unpack_bf16_pairs(_from_row_tiles(hp_ref, PACKED_TILES))
    hid = (_silu(jnp.dot(hb, sg_ref[...], preferred_element_type=F32))
           * jnp.dot(hb, su_ref[...], preferred_element_type=F32))
    f = jnp.dot(hid.astype(BF16), sd_ref[...], preferred_element_type=F32)
    for k in range(TOP_K):
        pltpu.make_async_copy(ys_ref.at[pl.ds(0, tm * c)], buf_ref.at[k], sem).wait()

    def weighted(t, carry):
        slot = pl.ds(pl.multiple_of(t * c, c), c)
        acc = gate_ref[0, 0, t * TOP_K] * buf_ref[0, slot, :]
        for k in range(1, TOP_K):
            acc = acc + gate_ref[0, 0, t * TOP_K + k] * buf_ref[k, slot, :]
        buf_ref[0, slot, :] = acc
        return carry

    lax.fori_loop(0, tm, weighted, 0, unroll=4)
    f = f + _from_row_tiles(buf_ref, c, lead=(0,))
    o_ref[...] = x_ref[...] + gf_ref[0] * _rms(f, g3_ref[...])


def _combine(lay, dest_flat, gate_flat, hp, x_mid, mods, g3, s_gate, s_up, s_down, ys, lat_only, compact_out):
    row = lambda i: (lay.tile(i, lat_only), 0)
    full = lambda shape: pl.BlockSpec(shape, lambda i: (0, 0))
    out_rows = lay.n_tiles(lat_only) * TM if compact_out else lay.rows
    out_map = (lambda i: (i, 0)) if compact_out else row
    return pl.pallas_call(
        _combine_kernel,
        grid=(lay.n_tiles(lat_only),),
        in_specs=[pl.BlockSpec((1, 1, TM * TOP_K), lambda i: (lay.tile(i, lat_only), 0, 0), memory_space=pltpu.SMEM),
                  pl.BlockSpec((1, 1, TM * TOP_K), lambda i: (lay.tile(i, lat_only), 0, 0), memory_space=pltpu.SMEM),
                  pl.BlockSpec((TM * PACKED_TILES, LANES), row),
                  pl.BlockSpec((TM, D_MODEL), row), _mod_spec(lay, 5, lat_only),
                  full((1, D_MODEL)), full((D_MODEL, EXPERT_HIDDEN)), full((D_MODEL, EXPERT_HIDDEN)),
                  full((EXPERT_HIDDEN, D_MODEL)), pl.BlockSpec(memory_space=pl.ANY)],
        out_specs=pl.BlockSpec((TM, D_MODEL), out_map),
        scratch_shapes=[pltpu.VMEM((TOP_K, TM * ROW_TILES, LANES), F32), pltpu.SemaphoreType.DMA],
        out_shape=jax.ShapeDtypeStruct((out_rows, D_MODEL), F32),
        compiler_params=_params(("arbitrary",)),
        name="combine",
    )(dest_flat, gate_flat, hp, x_mid, mods, g3, s_gate, s_up, s_down, ys)


def _rope_tables(lay):
    n_freq = HEAD_DIM // 4
    inv_freq = ROPE_THETA ** (-jnp.arange(n_freq, dtype=F32) / n_freq)
    t = jnp.arange(lay.n_lat)
    pos = jnp.stack([(t // GRID_W).astype(F32), (t % GRID_W).astype(F32)], axis=-1)
    ang = pos[:, :, None, None] * inv_freq
    ang = jnp.broadcast_to(ang, (lay.n_lat, 2, 2, n_freq)).reshape(lay.n_lat, HEAD_DIM)
    sign = jnp.tile(jnp.concatenate([-jnp.ones(n_freq, F32), jnp.ones(n_freq, F32)]), 2)
    cos = jnp.concatenate([jnp.ones((lay.n_ctx, HEAD_DIM), F32), jnp.cos(ang)], axis=0)
    sin = jnp.concatenate([jnp.zeros((lay.n_ctx, HEAD_DIM), F32), jnp.sin(ang) * sign], axis=0)
    return cos, sin


def _moe_plan(counts, n_blocks):
    counts = counts.reshape(-1)
    padded = (counts + MOE_BLK - 1) // MOE_BLK * MOE_BLK
    pad_end = jnp.cumsum(padded)
    pad_start = (pad_end - padded).astype(I32)
    blk_row = jnp.arange(n_blocks, dtype=I32) * MOE_BLK
    block_e = jnp.minimum(jnp.sum(blk_row[:, None] >= pad_end[None, :], axis=1), N_EXPERTS - 1).astype(I32)
    own = block_e[:, None] == jnp.arange(N_EXPERTS, dtype=I32)[None, :]
    count_b = jnp.sum(jnp.where(own, counts[None, :], 0), axis=1)
    start_b = jnp.sum(jnp.where(own, pad_start[None, :], 0), axis=1)
    block_valid = jnp.clip(count_b - (blk_row - start_b), 0, MOE_BLK).astype(I32)
    n_used = jnp.maximum(pad_end[-1:] // MOE_BLK, 1).astype(I32)
    return pad_start, block_e, block_valid, n_used


def kernel(x, c, ctx, c_ctx, mod_w, mod_b, norm_g, w_in, w_out, conv_w, conv_b, lru_wr, lru_br, lru_wi, lru_bi,
           lru_lambda, gla_w2, gla_b2, gla_norm_g, qk_norm_g, win_sink, router_w, router_bias, exp_w_gate,
           exp_w_up, exp_w_down, sh_w_gate, sh_w_up, sh_w_down):
    bsz, n_lat, d = x.shape
    n_ctx = ctx.shape[1]
    depth = mod_w.shape[0]
    assert d == D_MODEL and bsz < MOD_ROWS
    lay = _Layout(bsz, n_lat, n_ctx)

    cc = jnp.zeros((MOD_ROWS, d), F32).at[:bsz].set(c).at[bsz].set(c_ctx)
    mods_all = _adaln(cc, mod_w, mod_b)
    cos, sin = _rope_tables(lay)
    xt = jnp.concatenate([ctx, x], axis=1).reshape(lay.rows, d)

    splits = np.cumsum([0, W_A, W_B_RAW, W_C, W_D])
    for layer in range(depth):
        last = layer == depth - 1
        need_ctx = not last
        lat_only = last
        mods = mods_all[layer].reshape(MOD_ROWS * 6, 1, d)
        g = norm_g[layer]
        wl = w_in[layer]
        w_cat = jnp.concatenate(
            [wl[:, splits[0]:splits[1]], wl[:, splits[1]:splits[2]], jnp.zeros((d, W_B - W_B_RAW), F32),
             wl[:, splits[2]:splits[3]], wl[:, splits[3]:splits[4]]], axis=1).astype(BF16)
        p_a, p_b, p_c, p_d = _inproj(lay, xt, mods, g[0:1], w_cat)

        y_a = _rglru(lay, p_a, conv_w[layer], conv_b[layer], lru_wr[layer], lru_br[layer], lru_wi[layer],
                     lru_bi[layer], lru_lambda[layer])
        w2pad = jnp.zeros((2, LANES, GLA_HEADS * GLA_DK), F32)
        w2pad = w2pad.at[0, :GLA_RANK].set(gla_w2[layer, 0]).at[1, GLA_RANK:2 * GLA_RANK].set(gla_w2[layer, 1])
        y_b = _gla(lay, p_b, w2pad, gla_b2[layer], gla_norm_g[layer])
        y_c = _attention(lay, p_c, cos, sin, qk_norm_g[layer], window=False, need_ctx=need_ctx)
        y_d = _attention(lay, p_d, cos, sin, win_sink[layer], window=True, need_ctx=need_ctx)

        x_mid, h, hp = _outproj(lay, (y_a, y_b, y_c, y_d), w_out[layer].astype(BF16), xt, mods, g[1:2], g[2:3],
                                lat_only)
        top_e, gate, rank, counts = _router(lay, h, router_w[layer].T, router_bias[layer], lat_only)
        n_assign = lay.n_tiles(lat_only) * TM * TOP_K
        n_blocks = -(-(n_assign + N_EXPERTS * (MOE_BLK - 1)) // MOE_BLK)
        pad_start, block_e, block_valid, n_used = _moe_plan(counts, n_blocks)
        owner = top_e[:, :, None] == jnp.arange(N_EXPERTS, dtype=I32)
        dest = jnp.sum(jnp.where(owner, pad_start, 0), axis=-1) + rank
        dest_flat = dest.T.reshape(lay.rows // TM, 1, TM * TOP_K)
        xs = _dispatch(lay, dest_flat, hp, n_blocks * MOE_BLK, lat_only)
        ys = _experts(block_e, block_valid, n_used, xs, exp_w_gate, exp_w_up, exp_w_down, layer)
        gate_flat = gate.T.reshape(lay.rows // TM, 1, TM * TOP_K)
        xt = _combine(lay, dest_flat, gate_flat, hp, x_mid, mods, g[3:4], sh_w_gate[layer].astype(BF16),
                      sh_w_up[layer].astype(BF16), sh_w_down[layer].astype(BF16), ys, lat_only, compact_out=last)
    return xt.reshape(bsz, n_lat, d)
```

```python
import functools

import numpy as np
import jax
import jax.numpy as jnp
from jax import lax
from jax.experimental import pallas as pl
from jax.experimental.pallas import tpu as pltpu

F32 = jnp.float32
BF16 = jnp.bfloat16
I32 = jnp.int32
U32 = jnp.uint32
HIGHEST = lax.Precision.HIGHEST
NT_DIMS = (((1,), (1,)), ((), ()))
TN_DIMS = (((0,), (0,)), ((), ()))

LANES = 128
SUBLANES = 8
VMEM_LIMIT_BYTES = 56 * 1024 * 1024

D_MODEL = 2048
GRID_W = 64
HEAD_DIM = 128
GROUP_WIDTH = D_MODEL // 4
LRU_BLOCKS = 4
CONV_W = 4
RGLRU_C = 8.0
GLA_HEADS = 4
GLA_DK = 64
GLA_DV = 128
GLA_RANK = 16
GLA_TAU = 16.0
GLA_CHUNK = 64
KV_HEADS = 2
WINDOW = 128
N_EXPERTS = 64
N_GROUPS = 8
TOPK_GROUPS = 4
TOP_K = 8
EXPERT_HIDDEN = 512
ROUTED_SCALE = 2.5
RMS_EPS = 1e-6
ROPE_THETA = 10000.0
NEG_INF = -1e30

W_A = 2 * GROUP_WIDTH
W_B_RAW = 2 * GLA_HEADS * GLA_DK + 2 * GLA_HEADS * GLA_DV + 2 * GLA_RANK
W_B = 1664
W_C = GROUP_WIDTH + 2 * KV_HEADS * HEAD_DIM
W_D = W_C
Z_COL_BLOCK = (2 * GLA_HEADS * GLA_DK + 2 * GLA_HEADS * GLA_DV) // LANES

TM = 256
MOE_BLK = 512
PACKED_TILES = D_MODEL // 2 // LANES
MOD_ROWS = 16


def _params(sem, vmem=VMEM_LIMIT_BYTES):
    return pltpu.CompilerParams(dimension_semantics=sem, vmem_limit_bytes=vmem)


def _rms(x, g):
    return x * lax.rsqrt(jnp.mean(x * x, axis=-1, keepdims=True) + RMS_EPS) * g


def _silu(x):
    return x * jax.nn.sigmoid(x)


def _pack_bf16_pairs(h):
    n = h.shape[1] // 2
    hb = h.astype(BF16).astype(F32)
    hi = lax.bitcast_convert_type(hb[:, :n], U32)
    lo = lax.bitcast_convert_type(hb[:, n:], U32)
    return hi | (lo >> 16)


def _to_row_tiles(ref, x):
    m = x.shape[0]
    c = x.shape[1] // LANES
    for j in range(c):
        ref[pl.ds(j, m, stride=c), :] = x[:, j * LANES:(j + 1) * LANES]


def _from_row_tiles(ref, c, lead=()):
    m = ref.shape[-2] // c
    return jnp.concatenate([ref[lead + (pl.ds(j, m, stride=c), slice(None))] for j in range(c)], axis=1)


def _unpack_bf16_pairs(p):
    hi = lax.bitcast_convert_type(p & jnp.uint32(0xFFFF0000), F32)
    lo = lax.bitcast_convert_type(p << 16, F32)
    return jnp.concatenate([hi, lo], axis=1).astype(BF16)


def _adaln_kernel(c_ref, w_ref, b_ref, o_ref):
    a = _silu(c_ref[...])
    o_ref[0] = jnp.dot(a, w_ref[0], preferred_element_type=F32, precision=HIGHEST) + b_ref[0]


def _adaln(cc, mod_w, mod_b):
    depth, d, n = mod_w.shape
    tn = 1024
    return pl.pallas_call(
        _adaln_kernel,
        grid=(depth, n // tn),
        in_specs=[pl.BlockSpec((MOD_ROWS, d), lambda l, j: (0, 0)),
                  pl.BlockSpec((1, d, tn), lambda l, j: (l, 0, j)),
                  pl.BlockSpec((1, 1, tn), lambda l, j: (l, 0, j))],
        out_specs=pl.BlockSpec((1, MOD_ROWS, tn), lambda l, j: (l, 0, j)),
        out_shape=jax.ShapeDtypeStruct((depth, MOD_ROWS, n), F32),
        compiler_params=_params(("arbitrary", "arbitrary")),
        name="adaln",
    )(cc, mod_w, mod_b.reshape(depth, 1, n))


class _Layout:
    def __init__(self, bsz, n_lat, n_ctx):
        self.bsz, self.n_lat, self.n_ctx = bsz, n_lat, n_ctx
        self.seq = n_ctx + n_lat
        self.rows = bsz * self.seq
        assert n_ctx % TM == 0 and n_lat % TM == 0
        self.tiles_b = self.seq // TM
        self.ctx_tiles = n_ctx // TM
        self.lat_tiles = n_lat // TM

    def n_tiles(self, lat_only):
        return self.bsz * (self.lat_tiles if lat_only else self.tiles_b)

    def tile(self, i, lat_only):
        if lat_only:
            return (i // self.lat_tiles) * self.tiles_b + self.ctx_tiles + i % self.lat_tiles
        return i

    def mod_row(self, i, lat_only):
        if lat_only:
            return i // self.lat_tiles
        return jnp.where(i % self.tiles_b < self.ctx_tiles, self.bsz, i // self.tiles_b)


def _mod_spec(lay, chunk, lat_only):
    return pl.BlockSpec((1, 1, D_MODEL), lambda i: (lay.mod_row(i, lat_only) * 6 + chunk, 0, 0))


def _inproj_kernel(x_ref, sh_ref, sc_ref, g_ref, w_ref, oa_ref, ob_ref, oc_ref, od_ref):
    h = _rms(x_ref[...], g_ref[...]) * (1.0 + sc_ref[0]) + sh_ref[0]
    hb = h.astype(BF16)
    c0 = 0
    for o_ref in (oa_ref, ob_ref, oc_ref, od_ref):
        width = o_ref.shape[1]
        o_ref[...] = jnp.dot(hb, w_ref[:, c0:c0 + width], preferred_element_type=F32)
        c0 += width


def _inproj(lay, xt, mods, g0, w_cat):
    widths = (W_A, W_B, W_C, W_D)
    row = lambda i: (i, 0)
    return pl.pallas_call(
        _inproj_kernel,
        grid=(lay.n_tiles(False),),
        in_specs=[pl.BlockSpec((TM, D_MODEL), row),
                  _mod_spec(lay, 0, False), _mod_spec(lay, 1, False),
                  pl.BlockSpec((1, D_MODEL), lambda i: (0, 0)),
                  pl.BlockSpec((D_MODEL, sum(widths)), lambda i: (0, 0), pipeline_mode=pl.Buffered(1))],
        out_specs=[pl.BlockSpec((TM, w), row) for w in widths],
        out_shape=[jax.ShapeDtypeStruct((lay.rows, w), F32) for w in widths],
        compiler_params=_params(("arbitrary",)),
        name="inproj",
    )(xt, mods, mods, g0, w_cat)


def _rglru_kernel(x_ref, g_ref, cw_ref, cb_ref, wr_ref, br_ref, wi_ref, bi_ref, lam_ref, o_ref,
                  xs_ref, a_ref, b_ref, h_ref, *, n_ctx):
    seq = x_ref.shape[0]
    pad = SUBLANES
    n_tiles = seq // SUBLANES
    ctx_tiles = n_ctx // SUBLANES
    row = lax.broadcasted_iota(I32, (seq, LANES), 0)

    xs_ref[0:pad, :] = jnp.zeros((pad, LANES), F32)
    xs_ref[pad + seq:2 * pad + seq, :] = jnp.zeros((pad, LANES), F32)
    xs_ref[pad:pad + seq, :] = x_ref[...]
    u = jnp.zeros((seq, LANES), F32) + cb_ref[...]
    for j in range(CONV_W):
        off = j - 2
        xsh = xs_ref[pad + off:pad + off + seq, :]
        if off < 0:
            xsh = jnp.where((row >= n_ctx) & (row + off < n_ctx), 0.0, xsh)
        elif off > 0:
            xsh = jnp.where((row < n_ctx) & (row + off >= n_ctx), 0.0, xsh)
        u = u + cw_ref[j:j + 1, :] * xsh

    ub = u.astype(BF16)
    tile_shape = (n_tiles, SUBLANES, LANES)
    rmod = lax.broadcasted_iota(I32, tile_shape, 1)
    for d in range(2):
        reverse = d == 1
        r = jax.nn.sigmoid(jnp.dot(ub, wr_ref[d, 0].astype(BF16), preferred_element_type=F32) + br_ref[d:d + 1, :])
        gi = jax.nn.sigmoid(jnp.dot(ub, wi_ref[d, 0].astype(BF16), preferred_element_type=F32) + bi_ref[d:d + 1, :])
        lam = lam_ref[d:d + 1, :]
        softplus_neg = jnp.maximum(-lam, 0.0) + jnp.log(1.0 + jnp.exp(-jnp.abs(lam)))
        a = jnp.exp(-RGLRU_C * r * softplus_neg)
        b = jnp.sqrt(1.0 - a * a) * (gi * u)
        a = a.reshape(tile_shape)
        b = b.reshape(tile_shape)
        for s in (1, 2, 4):
            if reverse:
                ap, bp, ok = pltpu.roll(a, SUBLANES - s, 1), pltpu.roll(b, SUBLANES - s, 1), rmod < SUBLANES - s
            else:
                ap, bp, ok = pltpu.roll(a, s, 1), pltpu.roll(b, s, 1), rmod >= s
            b = jnp.where(ok, a * bp + b, b)
            a = jnp.where(ok, a * ap, a)
        a_ref[...] = a.reshape(seq, LANES)
        b_ref[...] = b.reshape(seq, LANES)

        def step(t, h, reverse=reverse):
            r0 = pl.multiple_of(t * SUBLANES, SUBLANES)
            h8 = a_ref[pl.ds(r0, SUBLANES), :] * h + b_ref[pl.ds(r0, SUBLANES), :]
            if reverse:
                h_ref[pl.ds(r0, SUBLANES), :] = h_ref[pl.ds(r0, SUBLANES), :] + h8
                return jnp.broadcast_to(h8[0:1, :], (SUBLANES, LANES))
            h_ref[pl.ds(r0, SUBLANES), :] = h8
            return jnp.broadcast_to(h8[SUBLANES - 1:SUBLANES, :], (SUBLANES, LANES))

        h0 = jnp.zeros((SUBLANES, LANES), F32)
        if reverse:
            hc = lax.fori_loop(0, ctx_tiles, lambda i, h: step(ctx_tiles - 1 - i, h), h0)
            lax.fori_loop(0, n_tiles - ctx_tiles, lambda i, h: step(n_tiles - 1 - i, h), hc)
        else:
            lax.fori_loop(0, n_tiles, step, h0)

    g = g_ref[...]
    gelu = 0.5 * g * (1.0 + jnp.tanh(np.sqrt(2.0 / np.pi) * (g + 0.044715 * (g * g * g))))
    o_ref[...] = h_ref[...] * gelu


def _rglru(lay, p_a, conv_w, conv_b, wr, br, wi, bi, lam):
    seq = lay.seq
    nb = LRU_BLOCKS
    col = lambda off: (lambda b, j: (b, off + j))
    vec = lambda rows: pl.BlockSpec((rows, LANES), lambda b, j: (0, j))
    wspec = pl.BlockSpec((2, 1, LANES, LANES), lambda b, j: (0, j, 0, 0))
    return pl.pallas_call(
        functools.partial(_rglru_kernel, n_ctx=lay.n_ctx),
        grid=(lay.bsz, nb),
        in_specs=[pl.BlockSpec((seq, LANES), col(0)), pl.BlockSpec((seq, LANES), col(nb)),
                  vec(CONV_W), vec(1), wspec, vec(2), wspec, vec(2), vec(2)],
        out_specs=pl.BlockSpec((seq, LANES), col(0)),
        out_shape=jax.ShapeDtypeStruct((lay.rows, GROUP_WIDTH), F32),
        scratch_shapes=[pltpu.VMEM((seq + 2 * SUBLANES, LANES), F32), pltpu.VMEM((seq, LANES), F32),
                        pltpu.VMEM((seq, LANES), F32), pltpu.VMEM((seq, LANES), F32)],
        compiler_params=_params(("arbitrary", "arbitrary")),
        name="rglru",
    )(p_a, p_a, conv_w, conv_b.reshape(1, -1), wr, br, wi, bi, lam)


def _gla_kernel(q_ref, k_ref, v_ref, g_ref, z_ref, w2_ref, b2_ref, ng_ref, o_ref,
                laf_ref, lab_ref, of_ref, ob_ref, *, n_ctx):
    seq = q_ref.shape[0]
    cs = GLA_CHUNK
    n_chunks = seq // cs
    ctx_chunks = n_ctx // cs
    z = z_ref[...]
    for d, la_ref in ((0, laf_ref), (1, lab_ref)):
        pre = jnp.dot(z, w2_ref[d], preferred_element_type=F32, precision=HIGHEST) + b2_ref[d:d + 1, :]
        log_sig = jnp.minimum(pre, 0.0) - jnp.log(1.0 + jnp.exp(-jnp.abs(pre)))
        la_ref[...] = log_sig * (1.0 / GLA_TAU)

    ri = lax.broadcasted_iota(I32, (cs, cs), 0)
    ci = lax.broadcasted_iota(I32, (cs, cs), 1)
    scale = GLA_DK ** -0.5

    def chunk(c, states, reverse):
        r0 = pl.multiple_of(c * cs, cs)
        rows = pl.ds(r0, cs)
        keep = (ri <= ci) if reverse else (ri >= ci)
        la = (lab_ref if reverse else laf_ref)[rows, :]
        la_hi = la.astype(BF16)
        rest = la - la_hi.astype(F32)
        la_mid = rest.astype(BF16)
        la_lo = (rest - la_mid.astype(F32)).astype(BF16)
        parts = jnp.dot(keep.astype(BF16), jnp.concatenate([la_hi, la_mid, la_lo], axis=1),
                        preferred_element_type=F32)
        cum = parts[:, 0:LANES] + parts[:, LANES:2 * LANES] + parts[:, 2 * LANES:3 * LANES]
        last = cum[0:1, :] if reverse else cum[cs - 1:cs, :]
        q = q_ref[rows, :] * scale
        k = k_ref[rows, :]
        qe = q * jnp.exp(cum)
        ke = k * jnp.exp(-cum)
        kl = k * jnp.exp(last - cum)
        el = jnp.exp(last)
        out_ref = ob_ref if reverse else of_ref
        new_states = []
        for hh in range(2):
            dk = slice(hh * GLA_DK, (hh + 1) * GLA_DK)
            dv = slice(hh * GLA_DV, (hh + 1) * GLA_DV)
            qh = qe[:, dk].astype(BF16)
            vh = v_ref[rows, dv].astype(BF16)
            st = states[hh]
            s = lax.dot_general(qh, ke[:, dk].astype(BF16), NT_DIMS, preferred_element_type=F32)
            s = jnp.where(keep, s, 0.0)
            o = (jnp.dot(s.astype(BF16), vh, preferred_element_type=F32)
                 + lax.dot_general(qh, st.astype(BF16), NT_DIMS, preferred_element_type=F32))
            out_ref[rows, dv] = o
            new_states.append(st * el[:, dk]
                              + lax.dot_general(vh, kl[:, dk].astype(BF16), TN_DIMS, preferred_element_type=F32))
        return tuple(new_states)

    def body(i, carry):
        sf, sb = carry
        sf = chunk(i, sf, False)
        cb = jnp.where(i < ctx_chunks, ctx_chunks - 1 - i, n_chunks - 1 + ctx_chunks - i)
        sb = chunk(cb, sb, True)
        return sf, sb

    zero = jnp.zeros((GLA_DV, GLA_DK), F32)
    lax.fori_loop(0, n_chunks, body, ((zero, zero), (zero, zero)))

    ng = ng_ref[...]
    for hh in range(2):
        dv = slice(hh * GLA_DV, (hh + 1) * GLA_DV)
        o = of_ref[:, dv] + ob_ref[:, dv]
        o_ref[:, dv] = _rms(o, ng) * _silu(g_ref[:, dv])


def _gla(lay, p_b, w2pad, b2, norm_g):
    seq = lay.seq
    hp = GLA_HEADS // 2
    qk_w = 2 * GLA_DK
    v_w = 2 * GLA_DV
    k_off = GLA_HEADS * GLA_DK // qk_w
    v_off = 2 * GLA_HEADS * GLA_DK // v_w
    g_off = v_off + GLA_HEADS * GLA_DV // v_w
    blk = lambda w, off: pl.BlockSpec((seq, w), lambda b, j: (b, off + j))
    return pl.pallas_call(
        functools.partial(_gla_kernel, n_ctx=lay.n_ctx),
        grid=(lay.bsz, hp),
        in_specs=[blk(qk_w, 0), blk(qk_w, k_off), blk(v_w, v_off), blk(v_w, g_off),
                  pl.BlockSpec((seq, LANES), lambda b, j: (b, Z_COL_BLOCK)),
                  pl.BlockSpec((2, LANES, qk_w), lambda b, j: (0, 0, j)),
                  pl.BlockSpec((2, qk_w), lambda b, j: (0, j)),
                  pl.BlockSpec((1, GLA_DV), lambda b, j: (0, 0))],
        out_specs=blk(v_w, 0),
        out_shape=jax.ShapeDtypeStruct((lay.rows, GROUP_WIDTH), F32),
        scratch_shapes=[pltpu.VMEM((seq, qk_w), F32), pltpu.VMEM((seq, qk_w), F32),
                        pltpu.VMEM((seq, v_w), F32), pltpu.VMEM((seq, v_w), F32)],
        compiler_params=_params(("arbitrary", "arbitrary")),
        name="gla",
    )(p_b, p_b, p_b, p_b, p_b, w2pad, b2, norm_g.reshape(1, -1))


def _rope(x, cos, sin_signed):
    lane = lax.broadcasted_iota(I32, x.shape, 1)
    partner = jnp.where((lane & 32) != 0, pltpu.roll(x, 32, 1), pltpu.roll(x, HEAD_DIM - 32, 1))
    return x * cos + partner * sin_signed


def _attn_kernel(*refs, window, need_ctx, tq, n_ctx):
    if window:
        sink_ref, q_ref, k_ref, v_ref, cos_ref, sin_ref, o_ref, ks_ref, vs_ref = refs
        qkg_ref = None
    else:
        q_ref, k_ref, v_ref, cos_ref, sin_ref, qkg_ref, o_ref, ks_ref, vs_ref = refs
    seq = k_ref.shape[0]
    n_lat = seq - n_ctx
    g = pl.program_id(1)
    qi = pl.program_id(2)
    jq = qi if need_ctx else qi + n_ctx // tq

    @pl.when(qi == 0)
    def _():
        k = k_ref[...]
        if not window:
            k = _rms(k, qkg_ref[1:2, :])
        ks_ref[...] = _rope(k, cos_ref[...], sin_ref[...]).astype(BF16)
        vs_ref[...] = v_ref[...].astype(BF16)

    r0 = pl.multiple_of(jq * tq, tq)
    cos = cos_ref[pl.ds(r0, tq), :]
    sin = sin_ref[pl.ds(r0, tq), :]
    heads = []
    for hh in range(2):
        qh = q_ref[:, hh * HEAD_DIM:(hh + 1) * HEAD_DIM]
        if not window:
            qh = _rms(qh, qkg_ref[0:1, :])
        heads.append((_rope(qh, cos, sin) * HEAD_DIM ** -0.5).astype(BF16))
    q2 = jnp.concatenate(heads, axis=0)

    def finish(o):
        o_ref[:, 0:HEAD_DIM] = o[0:tq]
        o_ref[:, HEAD_DIM:2 * HEAD_DIM] = o[tq:2 * tq]

    def scores(lo, n):
        return lax.dot_general(q2, ks_ref[pl.ds(lo, n), :], NT_DIMS, preferred_element_type=F32)

    def pv(p, lo, n):
        return jnp.dot(p.astype(BF16), vs_ref[pl.ds(lo, n), :], preferred_element_type=F32)

    if window:
        rowi = lax.broadcasted_iota(I32, (2 * tq, 1), 0)
        sink = jnp.where(rowi < tq, sink_ref[2 * g], sink_ref[2 * g + 1])

    def ctx_tile():
        s = scores(0, n_ctx)
        m = jnp.max(s, axis=-1, keepdims=True)
        if window:
            m = jnp.maximum(m, sink)
        p = jnp.exp(s - m)
        l = jnp.sum(p, axis=-1, keepdims=True)
        if window:
            l = l + jnp.exp(sink - m)
        finish(pv(p, 0, n_ctx) / l)

    def lat_tile():
        if not window:
            s = scores(0, seq)
            m = jnp.max(s, axis=-1, keepdims=True)
            p = jnp.exp(s - m)
            finish(pv(p, 0, seq) / jnp.sum(p, axis=-1, keepdims=True))
            return
        band = 3 * WINDOW
        jl = jq - n_ctx // tq
        start = jnp.clip((jl - 1) * WINDOW, 0, n_lat - band)
        lo = pl.multiple_of(n_ctx + start, WINDOW)
        sb = scores(lo, band)
        q_pos = jl * WINDOW + lax.broadcasted_iota(I32, (2 * tq, band), 0) % tq
        k_pos = start + lax.broadcasted_iota(I32, (2 * tq, band), 1)
        sb = jnp.where(jnp.abs(q_pos - k_pos) <= WINDOW, sb, NEG_INF)
        sc = scores(0, n_ctx)
        m = jnp.maximum(jnp.maximum(jnp.max(sb, axis=-1, keepdims=True), jnp.max(sc, axis=-1, keepdims=True)), sink)
        pb = jnp.exp(sb - m)
        pc = jnp.exp(sc - m)
        l = jnp.sum(pb, axis=-1, keepdims=True) + jnp.sum(pc, axis=-1, keepdims=True) + jnp.exp(sink - m)
        finish((pv(pb, lo, band) + pv(pc, 0, n_ctx)) / l)

    if need_ctx:
        pl.when(jq * tq < n_ctx)(ctx_tile)
        pl.when(jq * tq >= n_ctx)(lat_tile)
    else:
        lat_tile()


def _attention(lay, p, cos, sin, extra, *, window, need_ctx):
    seq = lay.seq
    tq = WINDOW if window else TM
    q_tiles_b = seq // tq
    nq = q_tiles_b if need_ctx else lay.n_lat // tq
    skip = 0 if need_ctx else lay.n_ctx // tq
    q_heads_w = 2 * HEAD_DIM
    k_off = GROUP_WIDTH // HEAD_DIM
    v_off = k_off + KV_HEADS

    def im(f):
        return (lambda b, g, qi, *_: f(b, g, qi))

    in_specs = [pl.BlockSpec((tq, q_heads_w), im(lambda b, g, qi: (b * q_tiles_b + skip + qi, g))),
                pl.BlockSpec((seq, HEAD_DIM), im(lambda b, g, qi: (b, k_off + g))),
                pl.BlockSpec((seq, HEAD_DIM), im(lambda b, g, qi: (b, v_off + g))),
                pl.BlockSpec((seq, HEAD_DIM), im(lambda b, g, qi: (0, 0))),
                pl.BlockSpec((seq, HEAD_DIM), im(lambda b, g, qi: (0, 0)))]
    args = [p, p, p, cos, sin]
    if not window:
        in_specs.append(pl.BlockSpec((2, HEAD_DIM), im(lambda b, g, qi: (0, 0))))
        args.append(extra)
    out_spec = pl.BlockSpec((tq, q_heads_w), im(lambda b, g, qi: (b * q_tiles_b + skip + qi, g)))
    kern = functools.partial(_attn_kernel, window=window, need_ctx=need_ctx, tq=tq, n_ctx=lay.n_ctx)
    scratch = [pltpu.VMEM((seq, HEAD_DIM), BF16), pltpu.VMEM((seq, HEAD_DIM), BF16)]
    out_shape = jax.ShapeDtypeStruct((lay.rows, GROUP_WIDTH), F32)
    sem = ("arbitrary", "arbitrary", "arbitrary")
    grid = (lay.bsz, KV_HEADS, nq)
    if window:
        return pl.pallas_call(
            kern,
            grid_spec=pltpu.PrefetchScalarGridSpec(num_scalar_prefetch=1, grid=grid, in_specs=in_specs,
                                                   out_specs=out_spec, scratch_shapes=scratch),
            out_shape=out_shape, compiler_params=_params(sem), name="win_attn",
        )(extra, *args)
    return pl.pallas_call(kern, grid=grid, in_specs=in_specs, out_specs=out_spec, scratch_shapes=scratch,
                          out_shape=out_shape, compiler_params=_params(sem), name="glob_attn")(*args)


def _outproj_kernel(ya_ref, yb_ref, yc_ref, yd_ref, w_ref, x_ref, gm_ref, sh_ref, sc_ref, g1_ref, g2_ref,
                    xo_ref, h_ref, hp_ref):
    acc = None
    for i, y_ref in enumerate((ya_ref, yb_ref, yc_ref, yd_ref)):
        part = jnp.dot(y_ref[...].astype(BF16), w_ref[i * GROUP_WIDTH:(i + 1) * GROUP_WIDTH, :],
                       preferred_element_type=F32)
        acc = part if acc is None else acc + part
    x_new = x_ref[...] + gm_ref[0] * _rms(acc, g1_ref[...])
    xo_ref[...] = x_new
    h = _rms(x_new, g2_ref[...]) * (1.0 + sc_ref[0]) + sh_ref[0]
    h_ref[...] = h
    _to_row_tiles(hp_ref, _pack_bf16_pairs(h))


def _outproj(lay, ys, w_out, xt, mods, g1, g2, lat_only):
    row = lambda i: (lay.tile(i, lat_only), 0)
    full = lambda shape: pl.BlockSpec(shape, lambda i: (0, 0))
    yspec = pl.BlockSpec((TM, GROUP_WIDTH), row)
    xspec = pl.BlockSpec((TM, D_MODEL), row)
    return pl.pallas_call(
        _outproj_kernel,
        grid=(lay.n_tiles(lat_only),),
        in_specs=[yspec, yspec, yspec, yspec, full((D_MODEL, D_MODEL)), xspec,
                  _mod_spec(lay, 2, lat_only), _mod_spec(lay, 3, lat_only), _mod_spec(lay, 4, lat_only),
                  full((1, D_MODEL)), full((1, D_MODEL))],
        out_specs=[xspec, xspec, pl.BlockSpec((TM * PACKED_TILES, LANES), row)],
        out_shape=[jax.ShapeDtypeStruct((lay.rows, D_MODEL), F32), jax.ShapeDtypeStruct((lay.rows, D_MODEL), F32),
                   jax.ShapeDtypeStruct((lay.rows * PACKED_TILES, LANES), U32)],
        compiler_params=_params(("arbitrary",)),
        name="outproj",
    )(*ys, w_out, xt, mods, mods, mods, g1, g2)


def _route(h, wt_ref, bias_ref, e_ref, gate_ref, rank_ref, cnt_ref, carry_ref):
    i = pl.program_id(0)
    tm = h.shape[0]
    ne = N_EXPERTS
    gsz = ne // N_GROUPS

    @pl.when(i == 0)
    def _():
        carry_ref[...] = jnp.zeros_like(carry_ref)

    logits = lax.dot_general(wt_ref[...], h, NT_DIMS, preferred_element_type=F32, precision=HIGHEST)
    scores = jax.nn.sigmoid(logits)
    biased = scores + bias_ref[...]

    sub = lax.broadcasted_iota(I32, (gsz, tm), 0)
    gs_rows = []
    for grp in range(N_GROUPS):
        blk = biased[grp * gsz:(grp + 1) * gsz, :]
        m1 = jnp.max(blk, axis=0, keepdims=True)
        first = jnp.min(jnp.where(blk == m1, sub, gsz), axis=0, keepdims=True)
        m2 = jnp.max(jnp.where(sub == first, -jnp.inf, blk), axis=0, keepdims=True)
        gs_rows.append(m1 + m2)
    gs = jnp.concatenate(gs_rows, axis=0)

    gidx = lax.broadcasted_iota(I32, (N_GROUPS, tm), 0)
    chosen = jnp.zeros((N_GROUPS, tm), jnp.bool_)
    cur = gs
    for _ in range(TOPK_GROUPS):
        m = jnp.max(cur, axis=0, keepdims=True)
        first = jnp.min(jnp.where(cur == m, gidx, N_GROUPS), axis=0, keepdims=True)
        sel = gidx == first
        chosen = chosen | sel
        cur = jnp.where(sel, -jnp.inf, cur)

    eidx = lax.broadcasted_iota(I32, (ne, tm), 0)
    eligible = jnp.concatenate(
        [jnp.broadcast_to(chosen[grp:grp + 1, :], (gsz, tm)) for grp in range(N_GROUPS)], axis=0)
    cur = jnp.where(eligible, biased, -jnp.inf)
    sels, gates = [], []
    onehot = jnp.zeros((ne, tm), jnp.bool_)
    for k in range(TOP_K):
        m = jnp.max(cur, axis=0, keepdims=True)
        first = jnp.min(jnp.where(cur == m, eidx, ne), axis=0, keepdims=True)
        sel = eidx == first
        e_ref[k:k + 1, :] = first
        gates.append(jnp.sum(jnp.where(sel, scores, 0.0), axis=0, keepdims=True))
        sels.append(sel)
        onehot = onehot | sel
        cur = jnp.where(sel, -jnp.inf, cur)
    total = gates[0]
    for gk in gates[1:]:
        total = total + gk
    for k in range(TOP_K):
        gate_ref[k:k + 1, :] = gates[k] / total * ROUTED_SCALE

    tr = lax.broadcasted_iota(I32, (tm, tm), 0)
    tc = lax.broadcasted_iota(I32, (tm, tm), 1)
    before = (tr < tc).astype(BF16)
    oh = onehot.astype(F32)
    prefix = jnp.dot(oh.astype(BF16), before, preferred_element_type=F32) + carry_ref[...]
    for k in range(TOP_K):
        rank_ref[k:k + 1, :] = jnp.sum(jnp.where(sels[k], prefix, 0.0), axis=0, keepdims=True).astype(I32)
    carry = carry_ref[...] + jnp.sum(oh, axis=1, keepdims=True)
    carry_ref[...] = carry
    cnt_ref[...] = carry.astype(I32)


def _router_kernel(h_ref, wt_ref, bias_ref, e_ref, gate_ref, rank_ref, cnt_ref, carry_ref):
    _route(h_ref[...], wt_ref, bias_ref, e_ref, gate_ref, rank_ref, cnt_ref, carry_ref)


def _router(lay, h, router_wt, bias, lat_only):
    col = lambda i: (0, lay.tile(i, lat_only))
    kt = pl.BlockSpec((TOP_K, TM), col)
    return pl.pallas_call(
        _router_kernel,
        grid=(lay.n_tiles(lat_only),),
        in_specs=[pl.BlockSpec((TM, D_MODEL), lambda i: (lay.tile(i, lat_only), 0)),
                  pl.BlockSpec((N_EXPERTS, D_MODEL), lambda i: (0, 0)),
                  pl.BlockSpec((N_EXPERTS, 1), lambda i: (0, 0))],
        out_specs=[kt, kt, kt, pl.BlockSpec((N_EXPERTS, 1), lambda i: (0, 0))],
        out_shape=[jax.ShapeDtypeStruct((TOP_K, lay.rows), I32), jax.ShapeDtypeStruct((TOP_K, lay.rows), F32),
                   jax.ShapeDtypeStruct((TOP_K, lay.rows), I32), jax.ShapeDtypeStruct((N_EXPERTS, 1), I32)],
        scratch_shapes=[pltpu.VMEM((N_EXPERTS, 1), F32)],
        compiler_params=_params(("arbitrary",)),
        name="router",
    )(h, router_wt, bias.reshape(-1, 1))


def _dispatch_kernel(dest_ref, hp_ref, xs_ref, sem):
    c = PACKED_TILES
    tm = hp_ref.shape[0] // c

    def issue(t, carry):
        src = hp_ref.at[pl.ds(pl.multiple_of(t * c, c), c)]
        for k in range(TOP_K):
            dest = pl.multiple_of(dest_ref[0, 0, t * TOP_K + k] * c, c)
            pltpu.make_async_copy(src, xs_ref.at[pl.ds(dest, c)], sem).start(priority=k % 2)
        return carry

    lax.fori_loop(0, tm, issue, 0)
    for k in range(TOP_K):
        pltpu.make_async_copy(hp_ref, xs_ref.at[pl.ds(0, tm * c)], sem).wait()


def _dispatch(lay, dest_flat, hp, n_rows, lat_only):
    return pl.pallas_call(
        _dispatch_kernel,
        grid=(lay.n_tiles(lat_only),),
        in_specs=[pl.BlockSpec((1, 1, TM * TOP_K), lambda i: (lay.tile(i, lat_only), 0, 0), memory_space=pltpu.SMEM),
                  pl.BlockSpec((TM * PACKED_TILES, LANES), lambda i: (lay.tile(i, lat_only), 0))],
        out_specs=pl.BlockSpec(memory_space=pl.ANY),
        scratch_shapes=[pltpu.SemaphoreType.DMA],
        out_shape=jax.ShapeDtypeStruct((n_rows * PACKED_TILES, LANES), U32),
        compiler_params=_params(("arbitrary",)),
        name="dispatch",
    )(dest_flat, hp)


def _experts_kernel(be_ref, bv_ref, nu_ref, xs_ref, wg_ref, wu_ref, wd_ref, ys_ref, wgb_ref, wub_ref, wdb_ref):
    j = pl.program_id(0)
    live = j < nu_ref[0]
    prev = be_ref[jnp.maximum(j - 1, 0)]

    @pl.when(live & ((j == 0) | (be_ref[j] != prev)))
    def _():
        wgb_ref[...] = wg_ref[0, 0].astype(BF16)
        wub_ref[...] = wu_ref[0, 0].astype(BF16)
        wdb_ref[...] = wd_ref[0, 0].astype(BF16)

    @pl.when(live)
    def _():
        packed = _from_row_tiles(xs_ref, PACKED_TILES)
        rowi = lax.broadcasted_iota(I32, packed.shape, 0)
        packed = jnp.where(rowi < bv_ref[j], packed, jnp.uint32(0))
        x = _unpack_bf16_pairs(packed)
        hid = (_silu(jnp.dot(x, wgb_ref[...], preferred_element_type=F32))
               * jnp.dot(x, wub_ref[...], preferred_element_type=F32))
        y = jnp.dot(hid.astype(BF16), wdb_ref[...], preferred_element_type=F32)
        _to_row_tiles(ys_ref, _pack_bf16_pairs(y))


def _experts(block_e, block_valid, n_used, xs, w_gate, w_up, w_down, layer):
    n_blocks = xs.shape[0] // (MOE_BLK * PACKED_TILES)
    last = lambda j, nu: jnp.minimum(j, nu[0] - 1)
    wmap = lambda j, be, bv, nu: (layer, be[last(j, nu)], 0, 0)
    return pl.pallas_call(
        _experts_kernel,
        grid_spec=pltpu.PrefetchScalarGridSpec(
            num_scalar_prefetch=3, grid=(n_blocks,),
            in_specs=[pl.BlockSpec((MOE_BLK * PACKED_TILES, LANES), lambda j, be, bv, nu: (last(j, nu), 0)),
                      pl.BlockSpec((1, 1, D_MODEL, EXPERT_HIDDEN), wmap),
                      pl.BlockSpec((1, 1, D_MODEL, EXPERT_HIDDEN), wmap),
                      pl.BlockSpec((1, 1, EXPERT_HIDDEN, D_MODEL), wmap)],
            out_specs=pl.BlockSpec((MOE_BLK * PACKED_TILES, LANES), lambda j, be, bv, nu: (last(j, nu), 0)),
            scratch_shapes=[pltpu.VMEM((D_MODEL, EXPERT_HIDDEN), BF16), pltpu.VMEM((D_MODEL, EXPERT_HIDDEN), BF16),
                            pltpu.VMEM((EXPERT_HIDDEN, D_MODEL), BF16)]),
        out_shape=jax.ShapeDtypeStruct((n_blocks * MOE_BLK * PACKED_TILES, LANES), U32),
        compiler_params=_params(("arbitrary",)),
        name="experts",
    )(block_e, block_valid, n_used, xs, w_gate, w_up, w_down)


def _combine_kernel(dest_ref, next_ref, gate_ref, hp_ref, x_ref, gf_ref, g3_ref, sg_ref, su_ref, sd_ref,
                    ys_ref, o_ref, buf_ref, acc_ref, sem):
    c = PACKED_TILES
    tm = hp_ref.shape[0] // c
    i = pl.program_id(0)
    cur = i % 2

    def gather(idx_ref, half):
        def issue(t, carry):
            slab = pl.ds(pl.multiple_of(t * c, c), c)
            for k in range(TOP_K):
                src = pl.multiple_of(idx_ref[0, 0, t * TOP_K + k] * c, c)
                pltpu.make_async_copy(ys_ref.at[pl.ds(src, c)], buf_ref.at[half, k, slab],
                                      sem.at[half]).start(priority=k % 2)
            return carry
        lax.fori_loop(0, tm, issue, 0)

    @pl.when(i == 0)
    def _():
        gather(dest_ref, 0)

    @pl.when(i + 1 < pl.num_programs(0))
    def _():
        gather(next_ref, 1 - cur)

    hb = _unpack_bf16_pairs(_from_row_tiles(hp_ref, c))
    hid = (_silu(jnp.dot(hb, sg_ref[...], preferred_element_type=F32))
           * jnp.dot(hb, su_ref[...], preferred_element_type=F32))
    f = jnp.dot(hid.astype(BF16), sd_ref[...], preferred_element_type=F32)
    for k in range(TOP_K):
        pltpu.make_async_copy(ys_ref.at[pl.ds(0, tm * c)], buf_ref.at[cur, k], sem.at[cur]).wait()

    def weighted(t, carry):
        slab = pl.ds(pl.multiple_of(t * c, c), c)
        acc_hi = jnp.zeros((c, LANES), F32)
        acc_lo = jnp.zeros((c, LANES), F32)
        for k in range(TOP_K):
            packed = buf_ref[cur, k, slab, :]
            gk = gate_ref[0, 0, t * TOP_K + k]
            acc_hi = acc_hi + gk * lax.bitcast_convert_type(packed & jnp.uint32(0xFFFF0000), F32)
            acc_lo = acc_lo + gk * lax.bitcast_convert_type(packed << 16, F32)
        acc_ref[0, slab, :] = acc_hi
        acc_ref[1, slab, :] = acc_lo
        return carry

    lax.fori_loop(0, tm, weighted, 0, unroll=4)
    f = f + jnp.concatenate([_from_row_tiles(acc_ref, c, lead=(0,)), _from_row_tiles(acc_ref, c, lead=(1,))], axis=1)
    o_ref[...] = x_ref[...] + gf_ref[0] * _rms(f, g3_ref[...])


def _combine(lay, dest_flat, gate_flat, hp, x_mid, mods, g3, s_gate, s_up, s_down, ys, lat_only, compact_out):
    row = lambda i: (lay.tile(i, lat_only), 0)
    full = lambda shape: pl.BlockSpec(shape, lambda i: (0, 0))
    out_rows = lay.n_tiles(lat_only) * TM if compact_out else lay.rows
    out_map = (lambda i: (i, 0)) if compact_out else row
    n_steps = lay.n_tiles(lat_only)
    flat = lambda f: pl.BlockSpec((1, 1, TM * TOP_K), lambda i: (lay.tile(f(i), lat_only), 0, 0),
                                  memory_space=pltpu.SMEM)
    return pl.pallas_call(
        _combine_kernel,
        grid=(n_steps,),
        in_specs=[flat(lambda i: i), flat(lambda i: jnp.minimum(i + 1, n_steps - 1)), flat(lambda i: i),
                  pl.BlockSpec((TM * PACKED_TILES, LANES), row),
                  pl.BlockSpec((TM, D_MODEL), row), _mod_spec(lay, 5, lat_only),
                  full((1, D_MODEL)), full((D_MODEL, EXPERT_HIDDEN)), full((D_MODEL, EXPERT_HIDDEN)),
                  full((EXPERT_HIDDEN, D_MODEL)), pl.BlockSpec(memory_space=pl.ANY)],
        out_specs=pl.BlockSpec((TM, D_MODEL), out_map),
        scratch_shapes=[pltpu.VMEM((2, TOP_K, TM * PACKED_TILES, LANES), U32),
                        pltpu.VMEM((2, TM * PACKED_TILES, LANES), F32), pltpu.SemaphoreType.DMA((2,))],
        out_shape=jax.ShapeDtypeStruct((out_rows, D_MODEL), F32),
        compiler_params=_params(("arbitrary",)),
        name="combine",
    )(dest_flat, dest_flat, gate_flat, hp, x_mid, mods, g3, s_gate, s_up, s_down, ys)


def _rope_tables(lay):
    n_freq = HEAD_DIM // 4
    inv_freq = ROPE_THETA ** (-jnp.arange(n_freq, dtype=F32) / n_freq)
    t = jnp.arange(lay.n_lat)
    pos = jnp.stack([(t // GRID_W).astype(F32), (t % GRID_W).astype(F32)], axis=-1)
    ang = pos[:, :, None, None] * inv_freq
    ang = jnp.broadcast_to(ang, (lay.n_lat, 2, 2, n_freq)).reshape(lay.n_lat, HEAD_DIM)
    sign = jnp.tile(jnp.concatenate([-jnp.ones(n_freq, F32), jnp.ones(n_freq, F32)]), 2)
    cos = jnp.concatenate([jnp.ones((lay.n_ctx, HEAD_DIM), F32), jnp.cos(ang)], axis=0)
    sin = jnp.concatenate([jnp.zeros((lay.n_ctx, HEAD_DIM), F32), jnp.sin(ang) * sign], axis=0)
    return cos, sin


def _moe_plan(counts, n_blocks):
    counts = counts.reshape(-1)
    padded = (counts + MOE_BLK - 1) // MOE_BLK * MOE_BLK
    pad_end = jnp.cumsum(padded)
    pad_start = (pad_end - padded).astype(I32)
    blk_row = jnp.arange(n_blocks, dtype=I32) * MOE_BLK
    block_e = jnp.minimum(jnp.sum(blk_row[:, None] >= pad_end[None, :], axis=1), N_EXPERTS - 1).astype(I32)
    own = block_e[:, None] == jnp.arange(N_EXPERTS, dtype=I32)[None, :]
    count_b = jnp.sum(jnp.where(own, counts[None, :], 0), axis=1)
    start_b = jnp.sum(jnp.where(own, pad_start[None, :], 0), axis=1)
    block_valid = jnp.clip(count_b - (blk_row - start_b), 0, MOE_BLK).astype(I32)
    n_used = jnp.maximum(pad_end[-1:] // MOE_BLK, 1).astype(I32)
    return pad_start, block_e, block_valid, n_used


def kernel(x, c, ctx, c_ctx, mod_w, mod_b, norm_g, w_in, w_out, conv_w, conv_b, lru_wr, lru_br, lru_wi, lru_bi,
           lru_lambda, gla_w2, gla_b2, gla_norm_g, qk_norm_g, win_sink, router_w, router_bias, exp_w_gate,
           exp_w_up, exp_w_down, sh_w_gate, sh_w_up, sh_w_down):
    bsz, n_lat, d = x.shape
    n_ctx = ctx.shape[1]
    depth = mod_w.shape[0]
    assert d == D_MODEL and bsz < MOD_ROWS
    lay = _Layout(bsz, n_lat, n_ctx)

    cc = jnp.zeros((MOD_ROWS, d), F32).at[:bsz].set(c).at[bsz].set(c_ctx)
    mods_all = _adaln(cc, mod_w, mod_b)
    cos, sin = _rope_tables(lay)
    xt = jnp.concatenate([ctx, x], axis=1).reshape(lay.rows, d)

    splits = np.cumsum([0, W_A, W_B_RAW, W_C, W_D])
    for layer in range(depth):
        last = layer == depth - 1
        need_ctx = not last
        lat_only = last
        mods = mods_all[layer].reshape(MOD_ROWS * 6, 1, d)
        g = norm_g[layer]
        wl = w_in[layer]
        w_cat = jnp.concatenate(
            [wl[:, splits[0]:splits[1]], wl[:, splits[1]:splits[2]], jnp.zeros((d, W_B - W_B_RAW), F32),
             wl[:, splits[2]:splits[3]], wl[:, splits[3]:splits[4]]], axis=1).astype(BF16)
        p_a, p_b, p_c, p_d = _inproj(lay, xt, mods, g[0:1], w_cat)

        y_a = _rglru(lay, p_a, conv_w[layer], conv_b[layer], lru_wr[layer], lru_br[layer], lru_wi[layer],
                     lru_bi[layer], lru_lambda[layer])
        w2pad = jnp.zeros((2, LANES, GLA_HEADS * GLA_DK), F32)
        w2pad = w2pad.at[0, :GLA_RANK].set(gla_w2[layer, 0]).at[1, GLA_RANK:2 * GLA_RANK].set(gla_w2[layer, 1])
        y_b = _gla(lay, p_b, w2pad, gla_b2[layer], gla_norm_g[layer])
        y_c = _attention(lay, p_c, cos, sin, qk_norm_g[layer], window=False, need_ctx=need_ctx)
        y_d = _attention(lay, p_d, cos, sin, win_sink[layer], window=True, need_ctx=need_ctx)

        x_mid, h, hp = _outproj(lay, (y_a, y_b, y_c, y_d), w_out[layer].astype(BF16), xt, mods, g[1:2], g[2:3],
                                lat_only)
        top_e, gate, rank, counts = _router(lay, h, router_w[layer].T, router_bias[layer], lat_only)
        n_assign = lay.n_tiles(lat_only) * TM * TOP_K
        n_blocks = -(-(n_assign + N_EXPERTS * (MOE_BLK - 1)) // MOE_BLK)
        pad_start, block_e, block_valid, n_used = _moe_plan(counts, n_blocks)
        owner = top_e[:, :, None] == jnp.arange(N_EXPERTS, dtype=I32)
        dest = jnp.sum(jnp.where(owner, pad_start, 0), axis=-1) + rank
        dest_flat = dest.T.reshape(lay.rows // TM, 1, TM * TOP_K)
        xs = _dispatch(lay, dest_flat, hp, n_blocks * MOE_BLK, lat_only)
        ys = _experts(block_e, block_valid, n_used, xs, exp_w_gate, exp_w_up, exp_w_down, layer)
        gate_flat = gate.T.reshape(lay.rows // TM, 1, TM * TOP_K)
        xt = _combine(lay, dest_flat, gate_flat, hp, x_mid, mods, g[3:4], sh_w_gate[layer].astype(BF16),
                      sh_w_up[layer].astype(BF16), sh_w_down[layer].astype(BF16), ys, lat_only, compact_out=last)
    return xt.reshape(bsz, n_lat, d)
```

```python
import functools

import numpy as np
import jax
import jax.numpy as jnp
from jax import lax
from jax.experimental import pallas as pl
from jax.experimental.pallas import tpu as pltpu

F32 = jnp.float32
BF16 = jnp.bfloat16
I32 = jnp.int32
U32 = jnp.uint32
HIGHEST = lax.Precision.HIGHEST
NT_DIMS = (((1,), (1,)), ((), ()))
TN_DIMS = (((0,), (0,)), ((), ()))

LANES = 128
SUBLANES = 8
VMEM_LIMIT_BYTES = 56 * 1024 * 1024

D_MODEL = 2048
GRID_W = 64
HEAD_DIM = 128
GROUP_WIDTH = D_MODEL // 4
LRU_BLOCKS = 4
CONV_W = 4
RGLRU_C = 8.0
GLA_HEADS = 4
GLA_DK = 64
GLA_DV = 128
GLA_RANK = 16
GLA_TAU = 16.0
GLA_CHUNK = 64
KV_HEADS = 2
WINDOW = 128
N_EXPERTS = 64
N_GROUPS = 8
TOPK_GROUPS = 4
TOP_K = 8
EXPERT_HIDDEN = 512
ROUTED_SCALE = 2.5
RMS_EPS = 1e-6
ROPE_THETA = 10000.0
NEG_INF = -1e30

W_A = 2 * GROUP_WIDTH
W_B_RAW = 2 * GLA_HEADS * GLA_DK + 2 * GLA_HEADS * GLA_DV + 2 * GLA_RANK
W_B = 1664
W_C = GROUP_WIDTH + 2 * KV_HEADS * HEAD_DIM
W_D = W_C
Z_COL_BLOCK = (2 * GLA_HEADS * GLA_DK + 2 * GLA_HEADS * GLA_DV) // LANES

TM = 256
MOE_BLK = 512
PACKED_TILES = D_MODEL // 2 // LANES
MOD_ROWS = 16


def _params(sem, vmem=VMEM_LIMIT_BYTES):
    return pltpu.CompilerParams(dimension_semantics=sem, vmem_limit_bytes=vmem)


def _rms(x, g):
    return x * lax.rsqrt(jnp.mean(x * x, axis=-1, keepdims=True) + RMS_EPS) * g


def _silu(x):
    return x * jax.nn.sigmoid(x)


def _pack_bf16_pairs(h):
    n = h.shape[1] // 2
    hb = h.astype(BF16).astype(F32)
    hi = lax.bitcast_convert_type(hb[:, :n], U32)
    lo = lax.bitcast_convert_type(hb[:, n:], U32)
    return hi | (lo >> 16)


def _to_row_tiles(ref, x):
    m = x.shape[0]
    c = x.shape[1] // LANES
    for j in range(c):
        ref[pl.ds(j, m, stride=c), :] = x[:, j * LANES:(j + 1) * LANES]


def _from_row_tiles(ref, c, lead=()):
    m = ref.shape[-2] // c
    return jnp.concatenate([ref[lead + (pl.ds(j, m, stride=c), slice(None))] for j in range(c)], axis=1)


def _unpack_bf16_pairs(p):
    hi = lax.bitcast_convert_type(p & jnp.uint32(0xFFFF0000), F32)
    lo = lax.bitcast_convert_type(p << 16, F32)
    return jnp.concatenate([hi, lo], axis=1).astype(BF16)


def _adaln_kernel(c_ref, w_ref, b_ref, o_ref):
    a = _silu(c_ref[...])
    o_ref[0] = jnp.dot(a, w_ref[0], preferred_element_type=F32, precision=HIGHEST) + b_ref[0]


def _adaln(cc, mod_w, mod_b):
    depth, d, n = mod_w.shape
    tn = 1024
    return pl.pallas_call(
        _adaln_kernel,
        grid=(depth, n // tn),
        in_specs=[pl.BlockSpec((MOD_ROWS, d), lambda l, j: (0, 0)),
                  pl.BlockSpec((1, d, tn), lambda l, j: (l, 0, j)),
                  pl.BlockSpec((1, 1, tn), lambda l, j: (l, 0, j))],
        out_specs=pl.BlockSpec((1, MOD_ROWS, tn), lambda l, j: (l, 0, j)),
        out_shape=jax.ShapeDtypeStruct((depth, MOD_ROWS, n), F32),
        compiler_params=_params(("arbitrary", "arbitrary")),
        name="adaln",
    )(cc, mod_w, mod_b.reshape(depth, 1, n))


class _Layout:
    def __init__(self, bsz, n_lat, n_ctx):
        self.bsz, self.n_lat, self.n_ctx = bsz, n_lat, n_ctx
        self.seq = n_ctx + n_lat
        self.rows = bsz * self.seq
        assert n_ctx % TM == 0 and n_lat % TM == 0
        self.tiles_b = self.seq // TM
        self.ctx_tiles = n_ctx // TM
        self.lat_tiles = n_lat // TM

    def n_tiles(self, lat_only):
        return self.bsz * (self.lat_tiles if lat_only else self.tiles_b)

    def tile(self, i, lat_only):
        if lat_only:
            return (i // self.lat_tiles) * self.tiles_b + self.ctx_tiles + i % self.lat_tiles
        return i

    def mod_row(self, i, lat_only):
        if lat_only:
            return i // self.lat_tiles
        return jnp.where(i % self.tiles_b < self.ctx_tiles, self.bsz, i // self.tiles_b)


def _mod_spec(lay, chunk, lat_only):
    return pl.BlockSpec((1, 1, D_MODEL), lambda i: (lay.mod_row(i, lat_only) * 6 + chunk, 0, 0))


def _inproj_kernel(*refs, tiles_b, ctx_tiles):
    if ctx_tiles:
        x_ref, c_ref, sh_ref, sc_ref, g_ref, w_ref, oa_ref, ob_ref, oc_ref, od_ref, xt_ref = refs
        x = jnp.where(pl.program_id(0) % tiles_b < ctx_tiles, c_ref[...], x_ref[...])
        xt_ref[...] = x
    else:
        x_ref, sh_ref, sc_ref, g_ref, w_ref, oa_ref, ob_ref, oc_ref, od_ref = refs
        x = x_ref[...]
    h = _rms(x, g_ref[...]) * (1.0 + sc_ref[0]) + sh_ref[0]
    hb = h.astype(BF16)
    c0 = 0
    for o_ref in (oa_ref, ob_ref, oc_ref, od_ref):
        width = o_ref.shape[1]
        o_ref[...] = jnp.dot(hb, w_ref[:, c0:c0 + width], preferred_element_type=F32)
        c0 += width


def _inproj(lay, xs, mods, g0, w_cat):
    widths = (W_A, W_B, W_C, W_D)
    row = lambda i: (i, 0)
    split = len(xs) == 2
    x_specs = [pl.BlockSpec((TM, D_MODEL), row)]
    out_specs = [pl.BlockSpec((TM, w), row) for w in widths]
    out_shape = [jax.ShapeDtypeStruct((lay.rows, w), F32) for w in widths]
    if split:
        lat = lambda i: ((i // lay.tiles_b) * lay.lat_tiles + jnp.maximum(i % lay.tiles_b - lay.ctx_tiles, 0), 0)
        ctx = lambda i: ((i // lay.tiles_b) * lay.ctx_tiles + jnp.minimum(i % lay.tiles_b, lay.ctx_tiles - 1), 0)
        x_specs = [pl.BlockSpec((TM, D_MODEL), lat), pl.BlockSpec((TM, D_MODEL), ctx)]
        out_specs.append(pl.BlockSpec((TM, D_MODEL), row))
        out_shape.append(jax.ShapeDtypeStruct((lay.rows, D_MODEL), F32))
    return pl.pallas_call(
        functools.partial(_inproj_kernel, tiles_b=lay.tiles_b, ctx_tiles=lay.ctx_tiles if split else 0),
        grid=(lay.n_tiles(False),),
        in_specs=x_specs + [_mod_spec(lay, 0, False), _mod_spec(lay, 1, False),
                            pl.BlockSpec((1, D_MODEL), lambda i: (0, 0)),
                            pl.BlockSpec((D_MODEL, sum(widths)), lambda i: (0, 0), pipeline_mode=pl.Buffered(1))],
        out_specs=out_specs,
        out_shape=out_shape,
        compiler_params=_params(("arbitrary",)),
        name="inproj",
    )(*xs, mods, mods, g0, w_cat)


def _rglru_kernel(x_ref, g_ref, cw_ref, cb_ref, wr_ref, br_ref, wi_ref, bi_ref, lam_ref, o_ref,
                  xs_ref, a_ref, b_ref, h_ref, *, n_ctx):
    seq = x_ref.shape[0]
    pad = SUBLANES
    n_tiles = seq // SUBLANES
    ctx_tiles = n_ctx // SUBLANES
    row = lax.broadcasted_iota(I32, (seq, LANES), 0)

    xs_ref[0:pad, :] = jnp.zeros((pad, LANES), F32)
    xs_ref[pad + seq:2 * pad + seq, :] = jnp.zeros((pad, LANES), F32)
    xs_ref[pad:pad + seq, :] = x_ref[...]
    u = jnp.zeros((seq, LANES), F32) + cb_ref[...]
    for j in range(CONV_W):
        off = j - 2
        xsh = xs_ref[pad + off:pad + off + seq, :]
        if off < 0:
            xsh = jnp.where((row >= n_ctx) & (row + off < n_ctx), 0.0, xsh)
        elif off > 0:
            xsh = jnp.where((row < n_ctx) & (row + off >= n_ctx), 0.0, xsh)
        u = u + cw_ref[j:j + 1, :] * xsh

    ub = u.astype(BF16)
    tile_shape = (n_tiles, SUBLANES, LANES)
    rmod = lax.broadcasted_iota(I32, tile_shape, 1)
    for d in range(2):
        reverse = d == 1
        r = jax.nn.sigmoid(jnp.dot(ub, wr_ref[d, 0].astype(BF16), preferred_element_type=F32) + br_ref[d:d + 1, :])
        gi = jax.nn.sigmoid(jnp.dot(ub, wi_ref[d, 0].astype(BF16), preferred_element_type=F32) + bi_ref[d:d + 1, :])
        lam = lam_ref[d:d + 1, :]
        softplus_neg = jnp.maximum(-lam, 0.0) + jnp.log(1.0 + jnp.exp(-jnp.abs(lam)))
        a = jnp.exp(-RGLRU_C * r * softplus_neg)
        b = jnp.sqrt(1.0 - a * a) * (gi * u)
        a = a.reshape(tile_shape)
        b = b.reshape(tile_shape)
        for s in (1, 2, 4):
            if reverse:
                ap, bp, ok = pltpu.roll(a, SUBLANES - s, 1), pltpu.roll(b, SUBLANES - s, 1), rmod < SUBLANES - s
            else:
                ap, bp, ok = pltpu.roll(a, s, 1), pltpu.roll(b, s, 1), rmod >= s
            b = jnp.where(ok, a * bp + b, b)
            a = jnp.where(ok, a * ap, a)
        a_ref[...] = a.reshape(seq, LANES)
        b_ref[...] = b.reshape(seq, LANES)

        def step(t, h, reverse=reverse):
            r0 = pl.multiple_of(t * SUBLANES, SUBLANES)
            h8 = a_ref[pl.ds(r0, SUBLANES), :] * h + b_ref[pl.ds(r0, SUBLANES), :]
            if reverse:
                h_ref[pl.ds(r0, SUBLANES), :] = h_ref[pl.ds(r0, SUBLANES), :] + h8
                return jnp.broadcast_to(h8[0:1, :], (SUBLANES, LANES))
            h_ref[pl.ds(r0, SUBLANES), :] = h8
            return jnp.broadcast_to(h8[SUBLANES - 1:SUBLANES, :], (SUBLANES, LANES))

        h0 = jnp.zeros((SUBLANES, LANES), F32)
        if reverse:
            hc = lax.fori_loop(0, ctx_tiles, lambda i, h: step(ctx_tiles - 1 - i, h), h0)
            lax.fori_loop(0, n_tiles - ctx_tiles, lambda i, h: step(n_tiles - 1 - i, h), hc)
        else:
            lax.fori_loop(0, n_tiles, step, h0)

    g = g_ref[...]
    gelu = 0.5 * g * (1.0 + jnp.tanh(np.sqrt(2.0 / np.pi) * (g + 0.044715 * (g * g * g))))
    o_ref[...] = h_ref[...] * gelu


def _rglru(lay, p_a, conv_w, conv_b, wr, br, wi, bi, lam):
    seq = lay.seq
    nb = LRU_BLOCKS
    col = lambda off: (lambda b, j: (b, off + j))
    vec = lambda rows: pl.BlockSpec((rows, LANES), lambda b, j: (0, j))
    wspec = pl.BlockSpec((2, 1, LANES, LANES), lambda b, j: (0, j, 0, 0))
    return pl.pallas_call(
        functools.partial(_rglru_kernel, n_ctx=lay.n_ctx),
        grid=(lay.bsz, nb),
        in_specs=[pl.BlockSpec((seq, LANES), col(0)), pl.BlockSpec((seq, LANES), col(nb)),
                  vec(CONV_W), vec(1), wspec, vec(2), wspec, vec(2), vec(2)],
        out_specs=pl.BlockSpec((seq, LANES), col(0)),
        out_shape=jax.ShapeDtypeStruct((lay.rows, GROUP_WIDTH), F32),
        scratch_shapes=[pltpu.VMEM((seq + 2 * SUBLANES, LANES), F32), pltpu.VMEM((seq, LANES), F32),
                        pltpu.VMEM((seq, LANES), F32), pltpu.VMEM((seq, LANES), F32)],
        compiler_params=_params(("arbitrary", "arbitrary")),
        name="rglru",
    )(p_a, p_a, conv_w, conv_b.reshape(1, -1), wr, br, wi, bi, lam)


def _gla_kernel(q_ref, k_ref, v_ref, g_ref, z_ref, w2_ref, b2_ref, ng_ref, o_ref,
                laf_ref, lab_ref, of_ref, ob_ref, *, n_ctx):
    seq = q_ref.shape[0]
    cs = GLA_CHUNK
    n_chunks = seq // cs
    ctx_chunks = n_ctx // cs
    z = z_ref[...]
    for d, la_ref in ((0, laf_ref), (1, lab_ref)):
        pre = jnp.dot(z, w2_ref[d], preferred_element_type=F32, precision=HIGHEST) + b2_ref[d:d + 1, :]
        log_sig = jnp.minimum(pre, 0.0) - jnp.log(1.0 + jnp.exp(-jnp.abs(pre)))
        la_ref[...] = log_sig * (1.0 / GLA_TAU)

    ri = lax.broadcasted_iota(I32, (cs, cs), 0)
    ci = lax.broadcasted_iota(I32, (cs, cs), 1)
    scale = GLA_DK ** -0.5

    def chunk(c, states, reverse):
        r0 = pl.multiple_of(c * cs, cs)
        rows = pl.ds(r0, cs)
        keep = (ri <= ci) if reverse else (ri >= ci)
        la = (lab_ref if reverse else laf_ref)[rows, :]
        la_hi = la.astype(BF16)
        rest = la - la_hi.astype(F32)
        la_mid = rest.astype(BF16)
        la_lo = (rest - la_mid.astype(F32)).astype(BF16)
        parts = jnp.dot(keep.astype(BF16), jnp.concatenate([la_hi, la_mid, la_lo], axis=1),
                        preferred_element_type=F32)
        cum = parts[:, 0:LANES] + parts[:, LANES:2 * LANES] + parts[:, 2 * LANES:3 * LANES]
        last = cum[0:1, :] if reverse else cum[cs - 1:cs, :]
        q = q_ref[rows, :] * scale
        k = k_ref[rows, :]
        qe = q * jnp.exp(cum)
        ke = k * jnp.exp(-cum)
        kl = k * jnp.exp(last - cum)
        el = jnp.exp(last)
        out_ref = ob_ref if reverse else of_ref
        new_states = []
        for hh in range(2):
            dk = slice(hh * GLA_DK, (hh + 1) * GLA_DK)
            dv = slice(hh * GLA_DV, (hh + 1) * GLA_DV)
            qh = qe[:, dk].astype(BF16)
            vh = v_ref[rows, dv].astype(BF16)
            st = states[hh]
            s = lax.dot_general(qh, ke[:, dk].astype(BF16), NT_DIMS, preferred_element_type=F32)
            s = jnp.where(keep, s, 0.0)
            o = (jnp.dot(s.astype(BF16), vh, preferred_element_type=F32)
                 + lax.dot_general(qh, st.astype(BF16), NT_DIMS, preferred_element_type=F32))
            out_ref[rows, dv] = o
            new_states.append(st * el[:, dk]
                              + lax.dot_general(vh, kl[:, dk].astype(BF16), TN_DIMS, preferred_element_type=F32))
        return tuple(new_states)

    def body(i, carry):
        sf, sb = carry
        sf = chunk(i, sf, False)
        cb = jnp.where(i < ctx_chunks, ctx_chunks - 1 - i, n_chunks - 1 + ctx_chunks - i)
        sb = chunk(cb, sb, True)
        return sf, sb

    zero = jnp.zeros((GLA_DV, GLA_DK), F32)
    lax.fori_loop(0, n_chunks, body, ((zero, zero), (zero, zero)))

    ng = ng_ref[...]
    for hh in range(2):
        dv = slice(hh * GLA_DV, (hh + 1) * GLA_DV)
        o = of_ref[:, dv] + ob_ref[:, dv]
        o_ref[:, dv] = _rms(o, ng) * _silu(g_ref[:, dv])


def _gla(lay, p_b, w2pad, b2, norm_g):
    seq = lay.seq
    hp = GLA_HEADS // 2
    qk_w = 2 * GLA_DK
    v_w = 2 * GLA_DV
    k_off = GLA_HEADS * GLA_DK // qk_w
    v_off = 2 * GLA_HEADS * GLA_DK // v_w
    g_off = v_off + GLA_HEADS * GLA_DV // v_w
    blk = lambda w, off: pl.BlockSpec((seq, w), lambda b, j: (b, off + j))
    return pl.pallas_call(
        functools.partial(_gla_kernel, n_ctx=lay.n_ctx),
        grid=(lay.bsz, hp),
        in_specs=[blk(qk_w, 0), blk(qk_w, k_off), blk(v_w, v_off), blk(v_w, g_off),
                  pl.BlockSpec((seq, LANES), lambda b, j: (b, Z_COL_BLOCK)),
                  pl.BlockSpec((2, LANES, qk_w), lambda b, j: (0, 0, j)),
                  pl.BlockSpec((2, qk_w), lambda b, j: (0, j)),
                  pl.BlockSpec((1, GLA_DV), lambda b, j: (0, 0))],
        out_specs=blk(v_w, 0),
        out_shape=jax.ShapeDtypeStruct((lay.rows, GROUP_WIDTH), F32),
        scratch_shapes=[pltpu.VMEM((seq, qk_w), F32), pltpu.VMEM((seq, qk_w), F32),
                        pltpu.VMEM((seq, v_w), F32), pltpu.VMEM((seq, v_w), F32)],
        compiler_params=_params(("arbitrary", "arbitrary")),
        name="gla",
    )(p_b, p_b, p_b, p_b, p_b, w2pad, b2, norm_g.reshape(1, -1))


def _rope(x, cos, sin_signed):
    lane = lax.broadcasted_iota(I32, x.shape, 1)
    partner = jnp.where((lane & 32) != 0, pltpu.roll(x, 32, 1), pltpu.roll(x, HEAD_DIM - 32, 1))
    return x * cos + partner * sin_signed


def _attn_kernel(*refs, window, need_ctx, tq, n_ctx):
    if window:
        sink_ref, q_ref, k_ref, v_ref, cos_ref, sin_ref, o_ref, ks_ref, vs_ref = refs
        qkg_ref = None
    else:
        q_ref, k_ref, v_ref, cos_ref, sin_ref, qkg_ref, o_ref, ks_ref, vs_ref = refs
    seq = k_ref.shape[0]
    n_lat = seq - n_ctx
    g = pl.program_id(1)
    qi = pl.program_id(2)
    jq = qi if need_ctx else qi + n_ctx // tq

    @pl.when(qi == 0)
    def _():
        k = k_ref[...]
        if not window:
            k = _rms(k, qkg_ref[1:2, :])
        ks_ref[...] = _rope(k, cos_ref[...], sin_ref[...]).astype(BF16)
        vs_ref[...] = v_ref[...].astype(BF16)

    r0 = pl.multiple_of(jq * tq, tq)
    cos = cos_ref[pl.ds(r0, tq), :]
    sin = sin_ref[pl.ds(r0, tq), :]
    heads = []
    for hh in range(2):
        qh = q_ref[:, hh * HEAD_DIM:(hh + 1) * HEAD_DIM]
        if not window:
            qh = _rms(qh, qkg_ref[0:1, :])
        heads.append((_rope(qh, cos, sin) * HEAD_DIM ** -0.5).astype(BF16))
    q2 = jnp.concatenate(heads, axis=0)

    def finish(o):
        o_ref[:, 0:HEAD_DIM] = o[0:tq]
        o_ref[:, HEAD_DIM:2 * HEAD_DIM] = o[tq:2 * tq]

    def scores(lo, n):
        return lax.dot_general(q2, ks_ref[pl.ds(lo, n), :], NT_DIMS, preferred_element_type=F32)

    def pv(p, lo, n):
        return jnp.dot(p.astype(BF16), vs_ref[pl.ds(lo, n), :], preferred_element_type=F32)

    if window:
        rowi = lax.broadcasted_iota(I32, (2 * tq, 1), 0)
        sink = jnp.where(rowi < tq, sink_ref[2 * g], sink_ref[2 * g + 1])

    def ctx_tile():
        s = scores(0, n_ctx)
        m = jnp.max(s, axis=-1, keepdims=True)
        if window:
            m = jnp.maximum(m, sink)
        p = jnp.exp(s - m)
        l = jnp.sum(p, axis=-1, keepdims=True)
        if window:
            l = l + jnp.exp(sink - m)
        finish(pv(p, 0, n_ctx) / l)

    def lat_tile():
        if not window:
            s = scores(0, seq)
            m = jnp.max(s, axis=-1, keepdims=True)
            p = jnp.exp(s - m)
            finish(pv(p, 0, seq) / jnp.sum(p, axis=-1, keepdims=True))
            return
        band = tq + 2 * WINDOW
        jl = jq - n_ctx // tq
        start = jnp.clip(jl * tq - WINDOW, 0, n_lat - band)
        lo = pl.multiple_of(n_ctx + start, WINDOW)
        sb = scores(lo, band)
        q_pos = jl * tq + lax.broadcasted_iota(I32, (2 * tq, band), 0) % tq
        k_pos = start + lax.broadcasted_iota(I32, (2 * tq, band), 1)
        sb = jnp.where(jnp.abs(q_pos - k_pos) <= WINDOW, sb, NEG_INF)
        sc = scores(0, n_ctx)
        m = jnp.maximum(jnp.maximum(jnp.max(sb, axis=-1, keepdims=True), jnp.max(sc, axis=-1, keepdims=True)), sink)
        pb = jnp.exp(sb - m)
        pc = jnp.exp(sc - m)
        l = jnp.sum(pb, axis=-1, keepdims=True) + jnp.sum(pc, axis=-1, keepdims=True) + jnp.exp(sink - m)
        finish((pv(pb, lo, band) + pv(pc, 0, n_ctx)) / l)

    if need_ctx:
        pl.when(jq * tq < n_ctx)(ctx_tile)
        pl.when(jq * tq >= n_ctx)(lat_tile)
    else:
        lat_tile()


def _attention(lay, p, cos, sin, extra, *, window, need_ctx):
    seq = lay.seq
    tq = TM
    q_tiles_b = seq // tq
    nq = q_tiles_b if need_ctx else lay.n_lat // tq
    skip = 0 if need_ctx else lay.n_ctx // tq
    q_heads_w = 2 * HEAD_DIM
    k_off = GROUP_WIDTH // HEAD_DIM
    v_off = k_off + KV_HEADS

    def im(f):
        return (lambda b, g, qi, *_: f(b, g, qi))

    in_specs = [pl.BlockSpec((tq, q_heads_w), im(lambda b, g, qi: (b * q_tiles_b + skip + qi, g))),
                pl.BlockSpec((seq, HEAD_DIM), im(lambda b, g, qi: (b, k_off + g))),
                pl.BlockSpec((seq, HEAD_DIM), im(lambda b, g, qi: (b, v_off + g))),
                pl.BlockSpec((seq, HEAD_DIM), im(lambda b, g, qi: (0, 0))),
                pl.BlockSpec((seq, HEAD_DIM), im(lambda b, g, qi: (0, 0)))]
    args = [p, p, p, cos, sin]
    if not window:
        in_specs.append(pl.BlockSpec((2, HEAD_DIM), im(lambda b, g, qi: (0, 0))))
        args.append(extra)
    out_spec = pl.BlockSpec((tq, q_heads_w), im(lambda b, g, qi: (b * q_tiles_b + skip + qi, g)))
    kern = functools.partial(_attn_kernel, window=window, need_ctx=need_ctx, tq=tq, n_ctx=lay.n_ctx)
    scratch = [pltpu.VMEM((seq, HEAD_DIM), BF16), pltpu.VMEM((seq, HEAD_DIM), BF16)]
    out_shape = jax.ShapeDtypeStruct((lay.rows, GROUP_WIDTH), F32)
    sem = ("arbitrary", "arbitrary", "arbitrary")
    grid = (lay.bsz, KV_HEADS, nq)
    if window:
        return pl.pallas_call(
            kern,
            grid_spec=pltpu.PrefetchScalarGridSpec(num_scalar_prefetch=1, grid=grid, in_specs=in_specs,
                                                   out_specs=out_spec, scratch_shapes=scratch),
            out_shape=out_shape, compiler_params=_params(sem), name="win_attn",
        )(extra, *args)
    return pl.pallas_call(kern, grid=grid, in_specs=in_specs, out_specs=out_spec, scratch_shapes=scratch,
                          out_shape=out_shape, compiler_params=_params(sem), name="glob_attn")(*args)


def _outproj_kernel(ya_ref, yb_ref, yc_ref, yd_ref, w_ref, x_ref, gm_ref, sh_ref, sc_ref, g1_ref, g2_ref,
                    xo_ref, h_ref, hp_ref):
    acc = None
    for i, y_ref in enumerate((ya_ref, yb_ref, yc_ref, yd_ref)):
        part = jnp.dot(y_ref[...].astype(BF16), w_ref[i * GROUP_WIDTH:(i + 1) * GROUP_WIDTH, :],
                       preferred_element_type=F32)
        acc = part if acc is None else acc + part
    x_new = x_ref[...] + gm_ref[0] * _rms(acc, g1_ref[...])
    xo_ref[...] = x_new
    h = _rms(x_new, g2_ref[...]) * (1.0 + sc_ref[0]) + sh_ref[0]
    h_ref[...] = h
    _to_row_tiles(hp_ref, _pack_bf16_pairs(h))


def _outproj(lay, ys, w_out, xt, mods, g1, g2, lat_only):
    row = lambda i: (lay.tile(i, lat_only), 0)
    full = lambda shape: pl.BlockSpec(shape, lambda i: (0, 0))
    yspec = pl.BlockSpec((TM, GROUP_WIDTH), row)
    xspec = pl.BlockSpec((TM, D_MODEL), row)
    return pl.pallas_call(
        _outproj_kernel,
        grid=(lay.n_tiles(lat_only),),
        in_specs=[yspec, yspec, yspec, yspec, full((D_MODEL, D_MODEL)), xspec,
                  _mod_spec(lay, 2, lat_only), _mod_spec(lay, 3, lat_only), _mod_spec(lay, 4, lat_only),
                  full((1, D_MODEL)), full((1, D_MODEL))],
        out_specs=[xspec, xspec, pl.BlockSpec((TM * PACKED_TILES, LANES), row)],
        out_shape=[jax.ShapeDtypeStruct((lay.rows, D_MODEL), F32), jax.ShapeDtypeStruct((lay.rows, D_MODEL), F32),
                   jax.ShapeDtypeStruct((lay.rows * PACKED_TILES, LANES), U32)],
        compiler_params=_params(("arbitrary",)),
        name="outproj",
    )(*ys, w_out, xt, mods, mods, mods, g1, g2)


def _route(h, wt_ref, bias_ref, e_ref, gate_ref, rank_ref, cnt_ref, carry_ref):
    i = pl.program_id(0)
    tm = h.shape[0]
    ne = N_EXPERTS
    gsz = ne // N_GROUPS

    @pl.when(i == 0)
    def _():
        carry_ref[...] = jnp.zeros_like(carry_ref)

    logits = lax.dot_general(wt_ref[...], h, NT_DIMS, preferred_element_type=F32, precision=HIGHEST)
    scores = jax.nn.sigmoid(logits)
    biased = scores + bias_ref[...]

    sub = lax.broadcasted_iota(I32, (gsz, tm), 0)
    gs_rows = []
    for grp in range(N_GROUPS):
        blk = biased[grp * gsz:(grp + 1) * gsz, :]
        m1 = jnp.max(blk, axis=0, keepdims=True)
        first = jnp.min(jnp.where(blk == m1, sub, gsz), axis=0, keepdims=True)
        m2 = jnp.max(jnp.where(sub == first, -jnp.inf, blk), axis=0, keepdims=True)
        gs_rows.append(m1 + m2)
    gs = jnp.concatenate(gs_rows, axis=0)

    gidx = lax.broadcasted_iota(I32, (N_GROUPS, tm), 0)
    chosen = jnp.zeros((N_GROUPS, tm), jnp.bool_)
    cur = gs
    for _ in range(TOPK_GROUPS):
        m = jnp.max(cur, axis=0, keepdims=True)
        first = jnp.min(jnp.where(cur == m, gidx, N_GROUPS), axis=0, keepdims=True)
        sel = gidx == first
        chosen = chosen | sel
        cur = jnp.where(sel, -jnp.inf, cur)

    eidx = lax.broadcasted_iota(I32, (ne, tm), 0)
    eligible = jnp.concatenate(
        [jnp.broadcast_to(chosen[grp:grp + 1, :], (gsz, tm)) for grp in range(N_GROUPS)], axis=0)
    cur = jnp.where(eligible, biased, -jnp.inf)
    sels, gates = [], []
    onehot = jnp.zeros((ne, tm), jnp.bool_)
    for k in range(TOP_K):
        m = jnp.max(cur, axis=0, keepdims=True)
        first = jnp.min(jnp.where(cur == m, eidx, ne), axis=0, keepdims=True)
        sel = eidx == first
        e_ref[k:k + 1, :] = first
        gates.append(jnp.sum(jnp.where(sel, scores, 0.0), axis=0, keepdims=True))
        sels.append(sel)
        onehot = onehot | sel
        cur = jnp.where(sel, -jnp.inf, cur)
    total = gates[0]
    for gk in gates[1:]:
        total = total + gk
    for k in range(TOP_K):
        gate_ref[k:k + 1, :] = gates[k] / total * ROUTED_SCALE

    tr = lax.broadcasted_iota(I32, (tm, tm), 0)
    tc = lax.broadcasted_iota(I32, (tm, tm), 1)
    before = (tr < tc).astype(BF16)
    oh = onehot.astype(F32)
    prefix = jnp.dot(oh.astype(BF16), before, preferred_element_type=F32) + carry_ref[...]
    for k in range(TOP_K):
        rank_ref[k:k + 1, :] = jnp.sum(jnp.where(sels[k], prefix, 0.0), axis=0, keepdims=True).astype(I32)
    carry = carry_ref[...] + jnp.sum(oh, axis=1, keepdims=True)
    carry_ref[...] = carry
    cnt_ref[...] = carry.astype(I32)


def _router_kernel(h_ref, wt_ref, bias_ref, e_ref, gate_ref, rank_ref, cnt_ref, carry_ref):
    _route(h_ref[...], wt_ref, bias_ref, e_ref, gate_ref, rank_ref, cnt_ref, carry_ref)


def _router(lay, h, router_wt, bias, lat_only):
    col = lambda i: (0, lay.tile(i, lat_only))
    kt = pl.BlockSpec((TOP_K, TM), col)
    return pl.pallas_call(
        _router_kernel,
        grid=(lay.n_tiles(lat_only),),
        in_specs=[pl.BlockSpec((TM, D_MODEL), lambda i: (lay.tile(i, lat_only), 0)),
                  pl.BlockSpec((N_EXPERTS, D_MODEL), lambda i: (0, 0)),
                  pl.BlockSpec((N_EXPERTS, 1), lambda i: (0, 0))],
        out_specs=[kt, kt, kt, pl.BlockSpec((N_EXPERTS, 1), lambda i: (0, 0))],
        out_shape=[jax.ShapeDtypeStruct((TOP_K, lay.rows), I32), jax.ShapeDtypeStruct((TOP_K, lay.rows), F32),
                   jax.ShapeDtypeStruct((TOP_K, lay.rows), I32), jax.ShapeDtypeStruct((N_EXPERTS, 1), I32)],
        scratch_shapes=[pltpu.VMEM((N_EXPERTS, 1), F32)],
        compiler_params=_params(("arbitrary",)),
        name="router",
    )(h, router_wt, bias.reshape(-1, 1))


def _dispatch_kernel(dest_ref, hp_ref, xs_ref, sem):
    c = PACKED_TILES
    tm = hp_ref.shape[0] // c

    def issue(t, carry):
        src = hp_ref.at[pl.ds(pl.multiple_of(t * c, c), c)]
        for k in range(TOP_K):
            dest = pl.multiple_of(dest_ref[0, 0, t * TOP_K + k] * c, c)
            pltpu.make_async_copy(src, xs_ref.at[pl.ds(dest, c)], sem).start(priority=k % 2)
        return carry

    lax.fori_loop(0, tm, issue, 0)
    for k in range(TOP_K):
        pltpu.make_async_copy(hp_ref, xs_ref.at[pl.ds(0, tm * c)], sem).wait()


def _dispatch(lay, dest_flat, hp, n_rows, lat_only):
    return pl.pallas_call(
        _dispatch_kernel,
        grid=(lay.n_tiles(lat_only),),
        in_specs=[pl.BlockSpec((1, 1, TM * TOP_K), lambda i: (lay.tile(i, lat_only), 0, 0), memory_space=pltpu.SMEM),
                  pl.BlockSpec((TM * PACKED_TILES, LANES), lambda i: (lay.tile(i, lat_only), 0))],
        out_specs=pl.BlockSpec(memory_space=pl.ANY),
        scratch_shapes=[pltpu.SemaphoreType.DMA],
        out_shape=jax.ShapeDtypeStruct((n_rows * PACKED_TILES, LANES), U32),
        compiler_params=_params(("arbitrary",)),
        name="dispatch",
    )(dest_flat, hp)


def _experts_kernel(be_ref, bv_ref, nu_ref, xs_ref, wg_ref, wu_ref, wd_ref, ys_ref, wgb_ref, wub_ref, wdb_ref):
    j = pl.program_id(0)
    live = j < nu_ref[0]
    prev = be_ref[jnp.maximum(j - 1, 0)]

    @pl.when(live & ((j == 0) | (be_ref[j] != prev)))
    def _():
        wgb_ref[...] = wg_ref[0, 0].astype(BF16)
        wub_ref[...] = wu_ref[0, 0].astype(BF16)
        wdb_ref[...] = wd_ref[0, 0].astype(BF16)

    @pl.when(live)
    def _():
        packed = _from_row_tiles(xs_ref, PACKED_TILES)
        rowi = lax.broadcasted_iota(I32, packed.shape, 0)
        packed = jnp.where(rowi < bv_ref[j], packed, jnp.uint32(0))
        x = _unpack_bf16_pairs(packed)
        hid = (_silu(jnp.dot(x, wgb_ref[...], preferred_element_type=F32))
               * jnp.dot(x, wub_ref[...], preferred_element_type=F32))
        y = jnp.dot(hid.astype(BF16), wdb_ref[...], preferred_element_type=F32)
        _to_row_tiles(ys_ref, _pack_bf16_pairs(y))


def _experts(block_e, block_valid, n_used, xs, w_gate, w_up, w_down, layer):
    n_blocks = xs.shape[0] // (MOE_BLK * PACKED_TILES)
    last = lambda j, nu: jnp.minimum(j, nu[0] - 1)
    wmap = lambda j, be, bv, nu: (layer, be[last(j, nu)], 0, 0)
    return pl.pallas_call(
        _experts_kernel,
        grid_spec=pltpu.PrefetchScalarGridSpec(
            num_scalar_prefetch=3, grid=(n_blocks,),
            in_specs=[pl.BlockSpec((MOE_BLK * PACKED_TILES, LANES), lambda j, be, bv, nu: (last(j, nu), 0)),
                      pl.BlockSpec((1, 1, D_MODEL, EXPERT_HIDDEN), wmap),
                      pl.BlockSpec((1, 1, D_MODEL, EXPERT_HIDDEN), wmap),
                      pl.BlockSpec((1, 1, EXPERT_HIDDEN, D_MODEL), wmap)],
            out_specs=pl.BlockSpec((MOE_BLK * PACKED_TILES, LANES), lambda j, be, bv, nu: (last(j, nu), 0)),
            scratch_shapes=[pltpu.VMEM((D_MODEL, EXPERT_HIDDEN), BF16), pltpu.VMEM((D_MODEL, EXPERT_HIDDEN), BF16),
                            pltpu.VMEM((EXPERT_HIDDEN, D_MODEL), BF16)]),
        out_shape=jax.ShapeDtypeStruct((n_blocks * MOE_BLK * PACKED_TILES, LANES), U32),
        compiler_params=_params(("arbitrary",)),
        name="experts",
    )(block_e, block_valid, n_used, xs, w_gate, w_up, w_down)


def _combine_kernel(dest_ref, next_ref, gate_ref, hp_ref, x_ref, gf_ref, g3_ref, sg_ref, su_ref, sd_ref,
                    ys_ref, o_ref, buf_ref, acc_ref, sem):
    c = PACKED_TILES
    tm = hp_ref.shape[0] // c
    i = pl.program_id(0)
    cur = i % 2

    def gather(idx_ref, half):
        def issue(t, carry):
            slab = pl.ds(pl.multiple_of(t * c, c), c)
            for k in range(TOP_K):
                src = pl.multiple_of(idx_ref[0, 0, t * TOP_K + k] * c, c)
                pltpu.make_async_copy(ys_ref.at[pl.ds(src, c)], buf_ref.at[half, k, slab],
                                      sem.at[half]).start(priority=k % 2)
            return carry
        lax.fori_loop(0, tm, issue, 0)

    @pl.when(i == 0)
    def _():
        gather(dest_ref, 0)

    @pl.when(i + 1 < pl.num_programs(0))
    def _():
        gather(next_ref, 1 - cur)

    hb = _unpack_bf16_pairs(_from_row_tiles(hp_ref, c))
    hid = (_silu(jnp.dot(hb, sg_ref[...], preferred_element_type=F32))
           * jnp.dot(hb, su_ref[...], preferred_element_type=F32))
    f = jnp.dot(hid.astype(BF16), sd_ref[...], preferred_element_type=F32)
    for k in range(TOP_K):
        pltpu.make_async_copy(ys_ref.at[pl.ds(0, tm * c)], buf_ref.at[cur, k], sem.at[cur]).wait()

    def weighted(t, carry):
        slab = pl.ds(pl.multiple_of(t * c, c), c)
        acc_hi = jnp.zeros((c, LANES), F32)
        acc_lo = jnp.zeros((c, LANES), F32)
        for k in range(TOP_K):
            packed = buf_ref[cur, k, slab, :]
            gk = gate_ref[0, 0, t * TOP_K + k]
            acc_hi = acc_hi + gk * lax.bitcast_convert_type(packed & jnp.uint32(0xFFFF0000), F32)
            acc_lo = acc_lo + gk * lax.bitcast_convert_type(packed << 16, F32)
        acc_ref[0, slab, :] = acc_hi
        acc_ref[1, slab, :] = acc_lo
        return carry

    lax.fori_loop(0, tm, weighted, 0, unroll=4)
    f = f + jnp.concatenate([_from_row_tiles(acc_ref, c, lead=(0,)), _from_row_tiles(acc_ref, c, lead=(1,))], axis=1)
    o_ref[...] = x_ref[...] + gf_ref[0] * _rms(f, g3_ref[...])


def _combine(lay, dest_flat, gate_flat, hp, x_mid, mods, g3, s_gate, s_up, s_down, ys, lat_only, compact_out):
    row = lambda i: (lay.tile(i, lat_only), 0)
    full = lambda shape: pl.BlockSpec(shape, lambda i: (0, 0))
    out_rows = lay.n_tiles(lat_only) * TM if compact_out else lay.rows
    out_map = (lambda i: (i, 0)) if compact_out else row
    n_steps = lay.n_tiles(lat_only)
    flat = lambda f: pl.BlockSpec((1, 1, TM * TOP_K), lambda i: (lay.tile(f(i), lat_only), 0, 0),
                                  memory_space=pltpu.SMEM)
    return pl.pallas_call(
        _combine_kernel,
        grid=(n_steps,),
        in_specs=[flat(lambda i: i), flat(lambda i: jnp.minimum(i + 1, n_steps - 1)), flat(lambda i: i),
                  pl.BlockSpec((TM * PACKED_TILES, LANES), row),
                  pl.BlockSpec((TM, D_MODEL), row), _mod_spec(lay, 5, lat_only),
                  full((1, D_MODEL)), full((D_MODEL, EXPERT_HIDDEN)), full((D_MODEL, EXPERT_HIDDEN)),
                  full((EXPERT_HIDDEN, D_MODEL)), pl.BlockSpec(memory_space=pl.ANY)],
        out_specs=pl.BlockSpec((TM, D_MODEL), out_map),
        scratch_shapes=[pltpu.VMEM((2, TOP_K, TM * PACKED_TILES, LANES), U32),
                        pltpu.VMEM((2, TM * PACKED_TILES, LANES), F32), pltpu.SemaphoreType.DMA((2,))],
        out_shape=jax.ShapeDtypeStruct((out_rows, D_MODEL), F32),
        compiler_params=_params(("arbitrary",)),
        name="combine",
    )(dest_flat, dest_flat, gate_flat, hp, x_mid, mods, g3, s_gate, s_up, s_down, ys)


def _rope_tables(lay):
    n_freq = HEAD_DIM // 4
    inv_freq = ROPE_THETA ** (-jnp.arange(n_freq, dtype=F32) / n_freq)
    t = jnp.arange(lay.n_lat)
    pos = jnp.stack([(t // GRID_W).astype(F32), (t % GRID_W).astype(F32)], axis=-1)
    ang = pos[:, :, None, None] * inv_freq
    ang = jnp.broadcast_to(ang, (lay.n_lat, 2, 2, n_freq)).reshape(lay.n_lat, HEAD_DIM)
    sign = jnp.tile(jnp.concatenate([-jnp.ones(n_freq, F32), jnp.ones(n_freq, F32)]), 2)
    cos = jnp.concatenate([jnp.ones((lay.n_ctx, HEAD_DIM), F32), jnp.cos(ang)], axis=0)
    sin = jnp.concatenate([jnp.zeros((lay.n_ctx, HEAD_DIM), F32), jnp.sin(ang) * sign], axis=0)
    return cos, sin


def _moe_plan(counts, n_blocks):
    counts = counts.reshape(-1)
    padded = (counts + MOE_BLK - 1) // MOE_BLK * MOE_BLK
    pad_end = jnp.cumsum(padded)
    pad_start = (pad_end - padded).astype(I32)
    blk_row = jnp.arange(n_blocks, dtype=I32) * MOE_BLK
    block_e = jnp.minimum(jnp.sum(blk_row[:, None] >= pad_end[None, :], axis=1), N_EXPERTS - 1).astype(I32)
    own = block_e[:, None] == jnp.arange(N_EXPERTS, dtype=I32)[None, :]
    count_b = jnp.sum(jnp.where(own, counts[None, :], 0), axis=1)
    start_b = jnp.sum(jnp.where(own, pad_start[None, :], 0), axis=1)
    block_valid = jnp.clip(count_b - (blk_row - start_b), 0, MOE_BLK).astype(I32)
    n_used = jnp.maximum(pad_end[-1:] // MOE_BLK, 1).astype(I32)
    return pad_start, block_e, block_valid, n_used


def kernel(x, c, ctx, c_ctx, mod_w, mod_b, norm_g, w_in, w_out, conv_w, conv_b, lru_wr, lru_br, lru_wi, lru_bi,
           lru_lambda, gla_w2, gla_b2, gla_norm_g, qk_norm_g, win_sink, router_w, router_bias, exp_w_gate,
           exp_w_up, exp_w_down, sh_w_gate, sh_w_up, sh_w_down):
    bsz, n_lat, d = x.shape
    n_ctx = ctx.shape[1]
    depth = mod_w.shape[0]
    assert d == D_MODEL and bsz < MOD_ROWS
    lay = _Layout(bsz, n_lat, n_ctx)

    cc = jnp.zeros((MOD_ROWS, d), F32).at[:bsz].set(c).at[bsz].set(c_ctx)
    mods_all = _adaln(cc, mod_w, mod_b)
    cos, sin = _rope_tables(lay)
    xt = None

    splits = np.cumsum([0, W_A, W_B_RAW, W_C, W_D])
    for layer in range(depth):
        last = layer == depth - 1
        need_ctx = not last
        lat_only = last
        mods = mods_all[layer].reshape(MOD_ROWS * 6, 1, d)
        g = norm_g[layer]
        wl = w_in[layer]
        w_cat = jnp.concatenate(
            [wl[:, splits[0]:splits[1]], wl[:, splits[1]:splits[2]], jnp.zeros((d, W_B - W_B_RAW), F32),
             wl[:, splits[2]:splits[3]], wl[:, splits[3]:splits[4]]], axis=1).astype(BF16)
        if xt is None:
            p_a, p_b, p_c, p_d, xt = _inproj(lay, (x.reshape(bsz * n_lat, d), ctx.reshape(bsz * n_ctx, d)), mods,
                                             g[0:1], w_cat)
        else:
            p_a, p_b, p_c, p_d = _inproj(lay, (xt,), mods, g[0:1], w_cat)

        y_a = _rglru(lay, p_a, conv_w[layer], conv_b[layer], lru_wr[layer], lru_br[layer], lru_wi[layer],
                     lru_bi[layer], lru_lambda[layer])
        w2pad = jnp.zeros((2, LANES, GLA_HEADS * GLA_DK), F32)
        w2pad = w2pad.at[0, :GLA_RANK].set(gla_w2[layer, 0]).at[1, GLA_RANK:2 * GLA_RANK].set(gla_w2[layer, 1])
        y_b = _gla(lay, p_b, w2pad, gla_b2[layer], gla_norm_g[layer])
        y_c = _attention(lay, p_c, cos, sin, qk_norm_g[layer], window=False, need_ctx=need_ctx)
        y_d = _attention(lay, p_d, cos, sin, win_sink[layer], window=True, need_ctx=need_ctx)

        x_mid, h, hp = _outproj(lay, (y_a, y_b, y_c, y_d), w_out[layer].astype(BF16), xt, mods, g[1:2], g[2:3],
                                lat_only)
        top_e, gate, rank, counts = _router(lay, h, router_w[layer].T, router_bias[layer], lat_only)
        n_assign = lay.n_tiles(lat_only) * TM * TOP_K
        n_blocks = -(-(n_assign + N_EXPERTS * (MOE_BLK - 1)) // MOE_BLK)
        pad_start, block_e, block_valid, n_used = _moe_plan(counts, n_blocks)
        owner = top_e[:, :, None] == jnp.arange(N_EXPERTS, dtype=I32)
        dest = jnp.sum(jnp.where(owner, pad_start, 0), axis=-1) + rank
        dest_flat = dest.T.reshape(lay.rows // TM, 1, TM * TOP_K)
        xs = _dispatch(lay, dest_flat, hp, n_blocks * MOE_BLK, lat_only)
        ys = _experts(block_e, block_valid, n_used, xs, exp_w_gate, exp_w_up, exp_w_down, layer)
        gate_flat = gate.T.reshape(lay.rows // TM, 1, TM * TOP_K)
        xt = _combine(lay, dest_flat, gate_flat, hp, x_mid, mods, g[3:4], sh_w_gate[layer].astype(BF16),
                      sh_w_up[layer].astype(BF16), sh_w_down[layer].astype(BF16), ys, lat_only, compact_out=last)
    return xt.reshape(bsz, n_lat, d)
```

```python
import functools

import numpy as np
import jax
import jax.numpy as jnp
from jax import lax
from jax.experimental import pallas as pl
from jax.experimental.pallas import tpu as pltpu

F32 = jnp.float32
BF16 = jnp.bfloat16
I32 = jnp.int32
U32 = jnp.uint32
HIGHEST = lax.Precision.HIGHEST
NT_DIMS = (((1,), (1,)), ((), ()))
TN_DIMS = (((0,), (0,)), ((), ()))

LANES = 128
SUBLANES = 8
VMEM_LIMIT_BYTES = 56 * 1024 * 1024

D_MODEL = 2048
GRID_W = 64
HEAD_DIM = 128
GROUP_WIDTH = D_MODEL // 4
LRU_BLOCKS = 4
CONV_W = 4
RGLRU_C = 8.0
GLA_HEADS = 4
GLA_DK = 64
GLA_DV = 128
GLA_RANK = 16
GLA_TAU = 16.0
GLA_CHUNK = 64
KV_HEADS = 2
WINDOW = 128
N_EXPERTS = 64
N_GROUPS = 8
TOPK_GROUPS = 4
TOP_K = 8
EXPERT_HIDDEN = 512
ROUTED_SCALE = 2.5
RMS_EPS = 1e-6
ROPE_THETA = 10000.0
NEG_INF = -1e30

W_A = 2 * GROUP_WIDTH
W_B_RAW = 2 * GLA_HEADS * GLA_DK + 2 * GLA_HEADS * GLA_DV + 2 * GLA_RANK
W_B = 1664
W_C = GROUP_WIDTH + 2 * KV_HEADS * HEAD_DIM
W_D = W_C
Z_COL_BLOCK = (2 * GLA_HEADS * GLA_DK + 2 * GLA_HEADS * GLA_DV) // LANES

TM = 256
MOE_BLK = 512
PACKED_TILES = D_MODEL // 2 // LANES
MOD_ROWS = 16


def _params(sem, vmem=VMEM_LIMIT_BYTES):
    return pltpu.CompilerParams(dimension_semantics=sem, vmem_limit_bytes=vmem)


def _rms(x, g):
    return x * lax.rsqrt(jnp.mean(x * x, axis=-1, keepdims=True) + RMS_EPS) * g


def _silu(x):
    return x * jax.nn.sigmoid(x)


def _pack_bf16_pairs(h):
    n = h.shape[1] // 2
    hb = h.astype(BF16).astype(F32)
    hi = lax.bitcast_convert_type(hb[:, :n], U32)
    lo = lax.bitcast_convert_type(hb[:, n:], U32)
    return hi | (lo >> 16)


def _to_row_tiles(ref, x):
    m = x.shape[0]
    c = x.shape[1] // LANES
    for j in range(c):
        ref[pl.ds(j, m, stride=c), :] = x[:, j * LANES:(j + 1) * LANES]


def _from_row_tiles(ref, c, lead=()):
    m = ref.shape[-2] // c
    return jnp.concatenate([ref[lead + (pl.ds(j, m, stride=c), slice(None))] for j in range(c)], axis=1)


def _unpack_bf16_pairs(p):
    hi = lax.bitcast_convert_type(p & jnp.uint32(0xFFFF0000), F32)
    lo = lax.bitcast_convert_type(p << 16, F32)
    return jnp.concatenate([hi, lo], axis=1).astype(BF16)


def _adaln_kernel(c_ref, w_ref, b_ref, o_ref):
    a = _silu(c_ref[...])
    o_ref[0] = jnp.dot(a, w_ref[0], preferred_element_type=F32, precision=HIGHEST) + b_ref[0]


def _adaln(cc, mod_w, mod_b):
    depth, d, n = mod_w.shape
    tn = 1024
    return pl.pallas_call(
        _adaln_kernel,
        grid=(depth, n // tn),
        in_specs=[pl.BlockSpec((MOD_ROWS, d), lambda l, j: (0, 0)),
                  pl.BlockSpec((1, d, tn), lambda l, j: (l, 0, j)),
                  pl.BlockSpec((1, 1, tn), lambda l, j: (l, 0, j))],
        out_specs=pl.BlockSpec((1, MOD_ROWS, tn), lambda l, j: (l, 0, j)),
        out_shape=jax.ShapeDtypeStruct((depth, MOD_ROWS, n), F32),
        compiler_params=_params(("arbitrary", "arbitrary")),
        name="adaln",
    )(cc, mod_w, mod_b.reshape(depth, 1, n))


class _Layout:
    def __init__(self, bsz, n_lat, n_ctx):
        self.bsz, self.n_lat, self.n_ctx = bsz, n_lat, n_ctx
        self.seq = n_ctx + n_lat
        self.rows = bsz * self.seq
        assert n_ctx % TM == 0 and n_lat % TM == 0
        self.tiles_b = self.seq // TM
        self.ctx_tiles = n_ctx // TM
        self.lat_tiles = n_lat // TM

    def n_tiles(self, lat_only):
        return self.bsz * (self.lat_tiles if lat_only else self.tiles_b)

    def tile(self, i, lat_only):
        if lat_only:
            return (i // self.lat_tiles) * self.tiles_b + self.ctx_tiles + i % self.lat_tiles
        return i

    def mod_row(self, i, lat_only):
        if lat_only:
            return i // self.lat_tiles
        return jnp.where(i % self.tiles_b < self.ctx_tiles, self.bsz, i // self.tiles_b)


def _mod_spec(lay, chunk, lat_only):
    return pl.BlockSpec((1, 1, D_MODEL), lambda i: (lay.mod_row(i, lat_only) * 6 + chunk, 0, 0))


def _inproj_kernel(*refs, tiles_b, ctx_tiles):
    if ctx_tiles:
        x_ref, c_ref, sh_ref, sc_ref, g_ref, w_ref, oa_ref, ob_ref, oc_ref, od_ref, xt_ref = refs
        x = jnp.where(pl.program_id(0) % tiles_b < ctx_tiles, c_ref[...], x_ref[...])
        xt_ref[...] = x
    else:
        x_ref, sh_ref, sc_ref, g_ref, w_ref, oa_ref, ob_ref, oc_ref, od_ref = refs
        x = x_ref[...]
    h = _rms(x, g_ref[...]) * (1.0 + sc_ref[0]) + sh_ref[0]
    hb = h.astype(BF16)
    c0 = 0
    for o_ref in (oa_ref, ob_ref, oc_ref, od_ref):
        width = o_ref.shape[1]
        o_ref[...] = jnp.dot(hb, w_ref[:, c0:c0 + width], preferred_element_type=F32)
        c0 += width


def _inproj(lay, xs, mods, g0, w_cat):
    widths = (W_A, W_B, W_C, W_D)
    row = lambda i: (i, 0)
    split = len(xs) == 2
    x_specs = [pl.BlockSpec((TM, D_MODEL), row)]
    out_specs = [pl.BlockSpec((TM, w), row) for w in widths]
    out_shape = [jax.ShapeDtypeStruct((lay.rows, w), F32) for w in widths]
    if split:
        lat = lambda i: ((i // lay.tiles_b) * lay.lat_tiles + jnp.maximum(i % lay.tiles_b - lay.ctx_tiles, 0), 0)
        ctx = lambda i: ((i // lay.tiles_b) * lay.ctx_tiles + jnp.minimum(i % lay.tiles_b, lay.ctx_tiles - 1), 0)
        x_specs = [pl.BlockSpec((TM, D_MODEL), lat), pl.BlockSpec((TM, D_MODEL), ctx)]
        out_specs.append(pl.BlockSpec((TM, D_MODEL), row))
        out_shape.append(jax.ShapeDtypeStruct((lay.rows, D_MODEL), F32))
    return pl.pallas_call(
        functools.partial(_inproj_kernel, tiles_b=lay.tiles_b, ctx_tiles=lay.ctx_tiles if split else 0),
        grid=(lay.n_tiles(False),),
        in_specs=x_specs + [_mod_spec(lay, 0, False), _mod_spec(lay, 1, False),
                            pl.BlockSpec((1, D_MODEL), lambda i: (0, 0)),
                            pl.BlockSpec((D_MODEL, sum(widths)), lambda i: (0, 0), pipeline_mode=pl.Buffered(1))],
        out_specs=out_specs,
        out_shape=out_shape,
        compiler_params=_params(("arbitrary",)),
        name="inproj",
    )(*xs, mods, mods, g0, w_cat)


def _rglru_kernel(x_ref, g_ref, cw_ref, cb_ref, wr_ref, br_ref, wi_ref, bi_ref, lam_ref, o_ref,
                  xs_ref, a_ref, b_ref, h_ref, *, n_ctx):
    seq = x_ref.shape[0]
    pad = SUBLANES
    n_tiles = seq // SUBLANES
    ctx_tiles = n_ctx // SUBLANES
    row = lax.broadcasted_iota(I32, (seq, LANES), 0)

    xs_ref[0:pad, :] = jnp.zeros((pad, LANES), F32)
    xs_ref[pad + seq:2 * pad + seq, :] = jnp.zeros((pad, LANES), F32)
    xs_ref[pad:pad + seq, :] = x_ref[...]
    u = jnp.zeros((seq, LANES), F32) + cb_ref[...]
    for j in range(CONV_W):
        off = j - 2
        xsh = xs_ref[pad + off:pad + off + seq, :]
        if off < 0:
            xsh = jnp.where((row >= n_ctx) & (row + off < n_ctx), 0.0, xsh)
        elif off > 0:
            xsh = jnp.where((row < n_ctx) & (row + off >= n_ctx), 0.0, xsh)
        u = u + cw_ref[j:j + 1, :] * xsh

    ub = u.astype(BF16)
    tile_shape = (n_tiles, SUBLANES, LANES)
    rmod = lax.broadcasted_iota(I32, tile_shape, 1)
    for d in range(2):
        reverse = d == 1
        r = jax.nn.sigmoid(jnp.dot(ub, wr_ref[d, 0].astype(BF16), preferred_element_type=F32) + br_ref[d:d + 1, :])
        gi = jax.nn.sigmoid(jnp.dot(ub, wi_ref[d, 0].astype(BF16), preferred_element_type=F32) + bi_ref[d:d + 1, :])
        lam = lam_ref[d:d + 1, :]
        softplus_neg = jnp.maximum(-lam, 0.0) + jnp.log(1.0 + jnp.exp(-jnp.abs(lam)))
        a = jnp.exp(-RGLRU_C * r * softplus_neg)
        b = jnp.sqrt(1.0 - a * a) * (gi * u)
        a = a.reshape(tile_shape)
        b = b.reshape(tile_shape)
        for s in (1, 2, 4):
            if reverse:
                ap, bp, ok = pltpu.roll(a, SUBLANES - s, 1), pltpu.roll(b, SUBLANES - s, 1), rmod < SUBLANES - s
            else:
                ap, bp, ok = pltpu.roll(a, s, 1), pltpu.roll(b, s, 1), rmod >= s
            b = jnp.where(ok, a * bp + b, b)
            a = jnp.where(ok, a * ap, a)
        a_ref[...] = a.reshape(seq, LANES)
        b_ref[...] = b.reshape(seq, LANES)

        def step(t, h, reverse=reverse):
            r0 = pl.multiple_of(t * SUBLANES, SUBLANES)
            h8 = a_ref[pl.ds(r0, SUBLANES), :] * h + b_ref[pl.ds(r0, SUBLANES), :]
            if reverse:
                h_ref[pl.ds(r0, SUBLANES), :] = h_ref[pl.ds(r0, SUBLANES), :] + h8
                return jnp.broadcast_to(h8[0:1, :], (SUBLANES, LANES))
            h_ref[pl.ds(r0, SUBLANES), :] = h8
            return jnp.broadcast_to(h8[SUBLANES - 1:SUBLANES, :], (SUBLANES, LANES))

        h0 = jnp.zeros((SUBLANES, LANES), F32)
        if reverse:
            hc = lax.fori_loop(0, ctx_tiles, lambda i, h: step(ctx_tiles - 1 - i, h), h0)
            lax.fori_loop(0, n_tiles - ctx_tiles, lambda i, h: step(n_tiles - 1 - i, h), hc)
        else:
            lax.fori_loop(0, n_tiles, step, h0)

    g = g_ref[...]
    gelu = 0.5 * g * (1.0 + jnp.tanh(np.sqrt(2.0 / np.pi) * (g + 0.044715 * (g * g * g))))
    o_ref[...] = h_ref[...] * gelu


def _rglru(lay, p_a, conv_w, conv_b, wr, br, wi, bi, lam):
    seq = lay.seq
    nb = LRU_BLOCKS
    col = lambda off: (lambda b, j: (b, off + j))
    vec = lambda rows: pl.BlockSpec((rows, LANES), lambda b, j: (0, j))
    wspec = pl.BlockSpec((2, 1, LANES, LANES), lambda b, j: (0, j, 0, 0))
    return pl.pallas_call(
        functools.partial(_rglru_kernel, n_ctx=lay.n_ctx),
        grid=(lay.bsz, nb),
        in_specs=[pl.BlockSpec((seq, LANES), col(0)), pl.BlockSpec((seq, LANES), col(nb)),
                  vec(CONV_W), vec(1), wspec, vec(2), wspec, vec(2), vec(2)],
        out_specs=pl.BlockSpec((seq, LANES), col(0)),
        out_shape=jax.ShapeDtypeStruct((lay.rows, GROUP_WIDTH), F32),
        scratch_shapes=[pltpu.VMEM((seq + 2 * SUBLANES, LANES), F32), pltpu.VMEM((seq, LANES), F32),
                        pltpu.VMEM((seq, LANES), F32), pltpu.VMEM((seq, LANES), F32)],
        compiler_params=_params(("arbitrary", "arbitrary")),
        name="rglru",
    )(p_a, p_a, conv_w, conv_b.reshape(1, -1), wr, br, wi, bi, lam)


def _gla_kernel(q_ref, k_ref, v_ref, g_ref, z_ref, w2_ref, b2_ref, ng_ref, o_ref,
                laf_ref, lab_ref, of_ref, ob_ref, *, n_ctx):
    seq = q_ref.shape[0]
    cs = GLA_CHUNK
    n_chunks = seq // cs
    ctx_chunks = n_ctx // cs
    z = z_ref[...]
    for d, la_ref in ((0, laf_ref), (1, lab_ref)):
        pre = jnp.dot(z, w2_ref[d], preferred_element_type=F32, precision=HIGHEST) + b2_ref[d:d + 1, :]
        log_sig = jnp.minimum(pre, 0.0) - jnp.log(1.0 + jnp.exp(-jnp.abs(pre)))
        la_ref[...] = log_sig * (1.0 / GLA_TAU)

    ri = lax.broadcasted_iota(I32, (cs, cs), 0)
    ci = lax.broadcasted_iota(I32, (cs, cs), 1)
    scale = GLA_DK ** -0.5

    def chunk(c, states, reverse):
        r0 = pl.multiple_of(c * cs, cs)
        rows = pl.ds(r0, cs)
        keep = (ri <= ci) if reverse else (ri >= ci)
        la = (lab_ref if reverse else laf_ref)[rows, :]
        la_hi = la.astype(BF16)
        rest = la - la_hi.astype(F32)
        la_mid = rest.astype(BF16)
        la_lo = (rest - la_mid.astype(F32)).astype(BF16)
        parts = jnp.dot(keep.astype(BF16), jnp.concatenate([la_hi, la_mid, la_lo], axis=1),
                        preferred_element_type=F32)
        cum = parts[:, 0:LANES] + parts[:, LANES:2 * LANES] + parts[:, 2 * LANES:3 * LANES]
        last = cum[0:1, :] if reverse else cum[cs - 1:cs, :]
        q = q_ref[rows, :] * scale
        k = k_ref[rows, :]
        qe = q * jnp.exp(cum)
        ke = k * jnp.exp(-cum)
        kl = k * jnp.exp(last - cum)
        el = jnp.exp(last)
        out_ref = ob_ref if reverse else of_ref
        new_states = []
        for hh in range(2):
            dk = slice(hh * GLA_DK, (hh + 1) * GLA_DK)
            dv = slice(hh * GLA_DV, (hh + 1) * GLA_DV)
            qh = qe[:, dk].astype(BF16)
            vh = v_ref[rows, dv].astype(BF16)
            st = states[hh]
            s = lax.dot_general(qh, ke[:, dk].astype(BF16), NT_DIMS, preferred_element_type=F32)
            s = jnp.where(keep, s, 0.0)
            o = (jnp.dot(s.astype(BF16), vh, preferred_element_type=F32)
                 + lax.dot_general(qh, st.astype(BF16), NT_DIMS, preferred_element_type=F32))
            out_ref[rows, dv] = o
            new_states.append(st * el[:, dk]
                              + lax.dot_general(vh, kl[:, dk].astype(BF16), TN_DIMS, preferred_element_type=F32))
        return tuple(new_states)

    def body(i, carry):
        sf, sb = carry
        sf = chunk(i, sf, False)
        cb = jnp.where(i < ctx_chunks, ctx_chunks - 1 - i, n_chunks - 1 + ctx_chunks - i)
        sb = chunk(cb, sb, True)
        return sf, sb

    zero = jnp.zeros((GLA_DV, GLA_DK), F32)
    lax.fori_loop(0, n_chunks, body, ((zero, zero), (zero, zero)))

    ng = ng_ref[...]
    for hh in range(2):
        dv = slice(hh * GLA_DV, (hh + 1) * GLA_DV)
        o = of_ref[:, dv] + ob_ref[:, dv]
        o_ref[:, dv] = _rms(o, ng) * _silu(g_ref[:, dv])


def _gla(lay, p_b, w2pad, b2, norm_g):
    seq = lay.seq
    hp = GLA_HEADS // 2
    qk_w = 2 * GLA_DK
    v_w = 2 * GLA_DV
    k_off = GLA_HEADS * GLA_DK // qk_w
    v_off = 2 * GLA_HEADS * GLA_DK // v_w
    g_off = v_off + GLA_HEADS * GLA_DV // v_w
    blk = lambda w, off: pl.BlockSpec((seq, w), lambda b, j: (b, off + j))
    return pl.pallas_call(
        functools.partial(_gla_kernel, n_ctx=lay.n_ctx),
        grid=(lay.bsz, hp),
        in_specs=[blk(qk_w, 0), blk(qk_w, k_off), blk(v_w, v_off), blk(v_w, g_off),
                  pl.BlockSpec((seq, LANES), lambda b, j: (b, Z_COL_BLOCK)),
                  pl.BlockSpec((2, LANES, qk_w), lambda b, j: (0, 0, j)),
                  pl.BlockSpec((2, qk_w), lambda b, j: (0, j)),
                  pl.BlockSpec((1, GLA_DV), lambda b, j: (0, 0))],
        out_specs=blk(v_w, 0),
        out_shape=jax.ShapeDtypeStruct((lay.rows, GROUP_WIDTH), F32),
        scratch_shapes=[pltpu.VMEM((seq, qk_w), F32), pltpu.VMEM((seq, qk_w), F32),
                        pltpu.VMEM((seq, v_w), F32), pltpu.VMEM((seq, v_w), F32)],
        compiler_params=_params(("arbitrary", "arbitrary")),
        name="gla",
    )(p_b, p_b, p_b, p_b, p_b, w2pad, b2, norm_g.reshape(1, -1))


def _rope(x, cos, sin_signed):
    lane = lax.broadcasted_iota(I32, x.shape, 1)
    partner = jnp.where((lane & 32) != 0, pltpu.roll(x, 32, 1), pltpu.roll(x, HEAD_DIM - 32, 1))
    return x * cos + partner * sin_signed


def _attn_kernel(*refs, window, need_ctx, tq, n_ctx):
    if window:
        sink_ref, q_ref, k_ref, v_ref, cos_ref, sin_ref, o_ref, ks_ref, vs_ref = refs
        qkg_ref = None
    else:
        q_ref, k_ref, v_ref, cos_ref, sin_ref, qkg_ref, o_ref, ks_ref, vs_ref = refs
    seq = k_ref.shape[0]
    n_lat = seq - n_ctx
    g = pl.program_id(1)
    qi = pl.program_id(2)
    jq = qi if need_ctx else qi + n_ctx // tq

    @pl.when(qi == 0)
    def _():
        k = k_ref[...]
        if not window:
            k = _rms(k, qkg_ref[1:2, :])
        ks_ref[...] = _rope(k, cos_ref[...], sin_ref[...]).astype(BF16)
        vs_ref[...] = v_ref[...].astype(BF16)

    r0 = pl.multiple_of(jq * tq, tq)
    cos = cos_ref[pl.ds(r0, tq), :]
    sin = sin_ref[pl.ds(r0, tq), :]
    heads = []
    for hh in range(2):
        qh = q_ref[:, hh * HEAD_DIM:(hh + 1) * HEAD_DIM]
        if not window:
            qh = _rms(qh, qkg_ref[0:1, :])
        heads.append((_rope(qh, cos, sin) * HEAD_DIM ** -0.5).astype(BF16))
    q2 = jnp.concatenate(heads, axis=0)

    def finish(o):
        o_ref[:, 0:HEAD_DIM] = o[0:tq]
        o_ref[:, HEAD_DIM:2 * HEAD_DIM] = o[tq:2 * tq]

    def scores(lo, n):
        return lax.dot_general(q2, ks_ref[pl.ds(lo, n), :], NT_DIMS, preferred_element_type=F32)

    def pv(p, lo, n):
        return jnp.dot(p.astype(BF16), vs_ref[pl.ds(lo, n), :], preferred_element_type=F32)

    if window:
        rowi = lax.broadcasted_iota(I32, (2 * tq, 1), 0)
        sink = jnp.where(rowi < tq, sink_ref[2 * g], sink_ref[2 * g + 1])

    def ctx_tile():
        s = scores(0, n_ctx)
        m = jnp.max(s, axis=-1, keepdims=True)
        if window:
            m = jnp.maximum(m, sink)
        p = jnp.exp(s - m)
        l = jnp.sum(p, axis=-1, keepdims=True)
        if window:
            l = l + jnp.exp(sink - m)
        finish(pv(p, 0, n_ctx) / l)

    def lat_tile():
        if not window:
            s = scores(0, seq)
            m = jnp.max(s, axis=-1, keepdims=True)
            p = jnp.exp(s - m)
            finish(pv(p, 0, seq) / jnp.sum(p, axis=-1, keepdims=True))
            return
        band = tq + 2 * WINDOW
        jl = jq - n_ctx // tq
        start = jnp.clip(jl * tq - WINDOW, 0, n_lat - band)
        lo = pl.multiple_of(n_ctx + start, WINDOW)
        sb = scores(lo, band)
        q_pos = jl * tq + lax.broadcasted_iota(I32, (2 * tq, band), 0) % tq
        k_pos = start + lax.broadcasted_iota(I32, (2 * tq, band), 1)
        sb = jnp.where(jnp.abs(q_pos - k_pos) <= WINDOW, sb, NEG_INF)
        sc = scores(0, n_ctx)
        m = jnp.maximum(jnp.maximum(jnp.max(sb, axis=-1, keepdims=True), jnp.max(sc, axis=-1, keepdims=True)), sink)
        pb = jnp.exp(sb - m)
        pc = jnp.exp(sc - m)
        l = jnp.sum(pb, axis=-1, keepdims=True) + jnp.sum(pc, axis=-1, keepdims=True) + jnp.exp(sink - m)
        finish((pv(pb, lo, band) + pv(pc, 0, n_ctx)) / l)

    if need_ctx:
        pl.when(jq * tq < n_ctx)(ctx_tile)
        pl.when(jq * tq >= n_ctx)(lat_tile)
    else:
        lat_tile()


def _attention(lay, p, cos, sin, extra, *, window, need_ctx):
    seq = lay.seq
    tq = TM
    q_tiles_b = seq // tq
    nq = q_tiles_b if need_ctx else lay.n_lat // tq
    skip = 0 if need_ctx else lay.n_ctx // tq
    q_heads_w = 2 * HEAD_DIM
    k_off = GROUP_WIDTH // HEAD_DIM
    v_off = k_off + KV_HEADS

    def im(f):
        return (lambda b, g, qi, *_: f(b, g, qi))

    in_specs = [pl.BlockSpec((tq, q_heads_w), im(lambda b, g, qi: (b * q_tiles_b + skip + qi, g))),
                pl.BlockSpec((seq, HEAD_DIM), im(lambda b, g, qi: (b, k_off + g))),
                pl.BlockSpec((seq, HEAD_DIM), im(lambda b, g, qi: (b, v_off + g))),
                pl.BlockSpec((seq, HEAD_DIM), im(lambda b, g, qi: (0, 0))),
                pl.BlockSpec((seq, HEAD_DIM), im(lambda b, g, qi: (0, 0)))]
    args = [p, p, p, cos, sin]
    if not window:
        in_specs.append(pl.BlockSpec((2, HEAD_DIM), im(lambda b, g, qi: (0, 0))))
        args.append(extra)
    out_spec = pl.BlockSpec((tq, q_heads_w), im(lambda b, g, qi: (b * q_tiles_b + skip + qi, g)))
    kern = functools.partial(_attn_kernel, window=window, need_ctx=need_ctx, tq=tq, n_ctx=lay.n_ctx)
    scratch = [pltpu.VMEM((seq, HEAD_DIM), BF16), pltpu.VMEM((seq, HEAD_DIM), BF16)]
    out_shape = jax.ShapeDtypeStruct((lay.rows, GROUP_WIDTH), F32)
    sem = ("arbitrary", "arbitrary", "arbitrary")
    grid = (lay.bsz, KV_HEADS, nq)
    if window:
        return pl.pallas_call(
            kern,
            grid_spec=pltpu.PrefetchScalarGridSpec(num_scalar_prefetch=1, grid=grid, in_specs=in_specs,
                                                   out_specs=out_spec, scratch_shapes=scratch),
            out_shape=out_shape, compiler_params=_params(sem), name="win_attn",
        )(extra, *args)
    return pl.pallas_call(kern, grid=grid, in_specs=in_specs, out_specs=out_spec, scratch_shapes=scratch,
                          out_shape=out_shape, compiler_params=_params(sem), name="glob_attn")(*args)


def _outproj_kernel(ya_ref, yb_ref, yc_ref, yd_ref, w_ref, x_ref, gm_ref, sh_ref, sc_ref, g1_ref, g2_ref,
                    xo_ref, h_ref, hp_ref):
    acc = None
    for i, y_ref in enumerate((ya_ref, yb_ref, yc_ref, yd_ref)):
        part = jnp.dot(y_ref[...].astype(BF16), w_ref[i * GROUP_WIDTH:(i + 1) * GROUP_WIDTH, :],
                       preferred_element_type=F32)
        acc = part if acc is None else acc + part
    x_new = x_ref[...] + gm_ref[0] * _rms(acc, g1_ref[...])
    xo_ref[...] = x_new
    h = _rms(x_new, g2_ref[...]) * (1.0 + sc_ref[0]) + sh_ref[0]
    h_ref[...] = h
    _to_row_tiles(hp_ref, _pack_bf16_pairs(h))


def _outproj(lay, ys, w_out, xt, mods, g1, g2, lat_only):
    row = lambda i: (lay.tile(i, lat_only), 0)
    full = lambda shape: pl.BlockSpec(shape, lambda i: (0, 0))
    yspec = pl.BlockSpec((TM, GROUP_WIDTH), row)
    xspec = pl.BlockSpec((TM, D_MODEL), row)
    return pl.pallas_call(
        _outproj_kernel,
        grid=(lay.n_tiles(lat_only),),
        in_specs=[yspec, yspec, yspec, yspec, full((D_MODEL, D_MODEL)), xspec,
                  _mod_spec(lay, 2, lat_only), _mod_spec(lay, 3, lat_only), _mod_spec(lay, 4, lat_only),
                  full((1, D_MODEL)), full((1, D_MODEL))],
        out_specs=[xspec, xspec, pl.BlockSpec((TM * PACKED_TILES, LANES), row)],
        out_shape=[jax.ShapeDtypeStruct((lay.rows, D_MODEL), F32), jax.ShapeDtypeStruct((lay.rows, D_MODEL), F32),
                   jax.ShapeDtypeStruct((lay.rows * PACKED_TILES, LANES), U32)],
        compiler_params=_params(("arbitrary",)),
        name="outproj",
    )(*ys, w_out, xt, mods, mods, mods, g1, g2)


def _route(h, wt_ref, bias_ref, e_ref, gate_ref, rank_ref, cnt_ref, carry_ref):
    i = pl.program_id(0)
    tm = h.shape[0]
    ne = N_EXPERTS
    gsz = ne // N_GROUPS

    @pl.when(i == 0)
    def _():
        carry_ref[...] = jnp.zeros_like(carry_ref)

    logits = lax.dot_general(wt_ref[...], h, NT_DIMS, preferred_element_type=F32, precision=HIGHEST)
    scores = jax.nn.sigmoid(logits)
    biased = scores + bias_ref[...]

    sub = lax.broadcasted_iota(I32, (gsz, tm), 0)
    gs_rows = []
    for grp in range(N_GROUPS):
        blk = biased[grp * gsz:(grp + 1) * gsz, :]
        m1 = jnp.max(blk, axis=0, keepdims=True)
        first = jnp.min(jnp.where(blk == m1, sub, gsz), axis=0, keepdims=True)
        m2 = jnp.max(jnp.where(sub == first, -jnp.inf, blk), axis=0, keepdims=True)
        gs_rows.append(m1 + m2)
    gs = jnp.concatenate(gs_rows, axis=0)

    gidx = lax.broadcasted_iota(I32, (N_GROUPS, tm), 0)
    chosen = jnp.zeros((N_GROUPS, tm), jnp.bool_)
    cur = gs
    for _ in range(TOPK_GROUPS):
        m = jnp.max(cur, axis=0, keepdims=True)
        first = jnp.min(jnp.where(cur == m, gidx, N_GROUPS), axis=0, keepdims=True)
        sel = gidx == first
        chosen = chosen | sel
        cur = jnp.where(sel, -jnp.inf, cur)

    eidx = lax.broadcasted_iota(I32, (ne, tm), 0)
    eligible = jnp.concatenate(
        [jnp.broadcast_to(chosen[grp:grp + 1, :], (gsz, tm)) for grp in range(N_GROUPS)], axis=0)
    cur = jnp.where(eligible, biased, -jnp.inf)
    sels, gates = [], []
    onehot = jnp.zeros((ne, tm), jnp.bool_)
    for k in range(TOP_K):
        m = jnp.max(cur, axis=0, keepdims=True)
        first = jnp.min(jnp.where(cur == m, eidx, ne), axis=0, keepdims=True)
        sel = eidx == first
        e_ref[k:k + 1, :] = first
        gates.append(jnp.sum(jnp.where(sel, scores, 0.0), axis=0, keepdims=True))
        sels.append(sel)
        onehot = onehot | sel
        cur = jnp.where(sel, -jnp.inf, cur)
    total = gates[0]
    for gk in gates[1:]:
        total = total + gk
    for k in range(TOP_K):
        gate_ref[k:k + 1, :] = gates[k] / total * ROUTED_SCALE

    tr = lax.broadcasted_iota(I32, (tm, tm), 0)
    tc = lax.broadcasted_iota(I32, (tm, tm), 1)
    before = (tr < tc).astype(BF16)
    oh = onehot.astype(F32)
    prefix = jnp.dot(oh.astype(BF16), before, preferred_element_type=F32) + carry_ref[...]
    for k in range(TOP_K):
        rank_ref[k:k + 1, :] = jnp.sum(jnp.where(sels[k], prefix, 0.0), axis=0, keepdims=True).astype(I32)
    carry = carry_ref[...] + jnp.sum(oh, axis=1, keepdims=True)
    carry_ref[...] = carry
    cnt_ref[...] = carry.astype(I32)


def _router_kernel(h_ref, wt_ref, bias_ref, e_ref, gate_ref, rank_ref, cnt_ref, carry_ref):
    _route(h_ref[...], wt_ref, bias_ref, e_ref, gate_ref, rank_ref, cnt_ref, carry_ref)


def _router(lay, h, router_wt, bias, lat_only):
    col = lambda i: (0, lay.tile(i, lat_only))
    kt = pl.BlockSpec((TOP_K, TM), col)
    return pl.pallas_call(
        _router_kernel,
        grid=(lay.n_tiles(lat_only),),
        in_specs=[pl.BlockSpec((TM, D_MODEL), lambda i: (lay.tile(i, lat_only), 0)),
                  pl.BlockSpec((N_EXPERTS, D_MODEL), lambda i: (0, 0)),
                  pl.BlockSpec((N_EXPERTS, 1), lambda i: (0, 0))],
        out_specs=[kt, kt, kt, pl.BlockSpec((N_EXPERTS, 1), lambda i: (0, 0))],
        out_shape=[jax.ShapeDtypeStruct((TOP_K, lay.rows), I32), jax.ShapeDtypeStruct((TOP_K, lay.rows), F32),
                   jax.ShapeDtypeStruct((TOP_K, lay.rows), I32), jax.ShapeDtypeStruct((N_EXPERTS, 1), I32)],
        scratch_shapes=[pltpu.VMEM((N_EXPERTS, 1), F32)],
        compiler_params=_params(("arbitrary",)),
        name="router",
    )(h, router_wt, bias.reshape(-1, 1))


def _dispatch_kernel(dest_ref, hp_ref, xs_ref, sem):
    c = PACKED_TILES
    tm = hp_ref.shape[0] // c

    def issue(t, carry):
        src = hp_ref.at[pl.ds(pl.multiple_of(t * c, c), c)]
        for k in range(TOP_K):
            dest = pl.multiple_of(dest_ref[0, 0, t * TOP_K + k] * c, c)
            pltpu.make_async_copy(src, xs_ref.at[pl.ds(dest, c)], sem).start(priority=k % 2)
        return carry

    lax.fori_loop(0, tm, issue, 0)
    for k in range(TOP_K):
        pltpu.make_async_copy(hp_ref, xs_ref.at[pl.ds(0, tm * c)], sem).wait()


def _dispatch(lay, dest_flat, hp, n_rows, lat_only):
    return pl.pallas_call(
        _dispatch_kernel,
        grid=(lay.n_tiles(lat_only),),
        in_specs=[pl.BlockSpec((1, 1, TM * TOP_K), lambda i: (lay.tile(i, lat_only), 0, 0), memory_space=pltpu.SMEM),
                  pl.BlockSpec((TM * PACKED_TILES, LANES), lambda i: (lay.tile(i, lat_only), 0))],
        out_specs=pl.BlockSpec(memory_space=pl.ANY),
        scratch_shapes=[pltpu.SemaphoreType.DMA],
        out_shape=jax.ShapeDtypeStruct((n_rows * PACKED_TILES, LANES), U32),
        compiler_params=_params(("arbitrary",)),
        name="dispatch",
    )(dest_flat, hp)


def _experts_kernel(be_ref, bv_ref, nu_ref, xs_ref, wg_ref, wu_ref, wd_ref, ys_ref, wgb_ref, wub_ref, wdb_ref):
    j = pl.program_id(0)
    live = j < nu_ref[0]
    prev = be_ref[jnp.maximum(j - 1, 0)]

    @pl.when(live & ((j == 0) | (be_ref[j] != prev)))
    def _():
        wgb_ref[...] = wg_ref[0, 0].astype(BF16)
        wub_ref[...] = wu_ref[0, 0].astype(BF16)
        wdb_ref[...] = wd_ref[0, 0].astype(BF16)

    @pl.when(live)
    def _():
        packed = _from_row_tiles(xs_ref, PACKED_TILES)
        rowi = lax.broadcasted_iota(I32, packed.shape, 0)
        packed = jnp.where(rowi < bv_ref[j], packed, jnp.uint32(0))
        x = _unpack_bf16_pairs(packed)
        hid = (_silu(jnp.dot(x, wgb_ref[...], preferred_element_type=F32))
               * jnp.dot(x, wub_ref[...], preferred_element_type=F32))
        y = jnp.dot(hid.astype(BF16), wdb_ref[...], preferred_element_type=F32)
        _to_row_tiles(ys_ref, _pack_bf16_pairs(y))


def _experts(block_e, block_valid, n_used, xs, w_gate, w_up, w_down, layer):
    n_blocks = xs.shape[0] // (MOE_BLK * PACKED_TILES)
    last = lambda j, nu: jnp.minimum(j, nu[0] - 1)
    wmap = lambda j, be, bv, nu: (layer, be[last(j, nu)], 0, 0)
    return pl.pallas_call(
        _experts_kernel,
        grid_spec=pltpu.PrefetchScalarGridSpec(
            num_scalar_prefetch=3, grid=(n_blocks,),
            in_specs=[pl.BlockSpec((MOE_BLK * PACKED_TILES, LANES), lambda j, be, bv, nu: (last(j, nu), 0)),
                      pl.BlockSpec((1, 1, D_MODEL, EXPERT_HIDDEN), wmap),
                      pl.BlockSpec((1, 1, D_MODEL, EXPERT_HIDDEN), wmap),
                      pl.BlockSpec((1, 1, EXPERT_HIDDEN, D_MODEL), wmap)],
            out_specs=pl.BlockSpec((MOE_BLK * PACKED_TILES, LANES), lambda j, be, bv, nu: (last(j, nu), 0)),
            scratch_shapes=[pltpu.VMEM((D_MODEL, EXPERT_HIDDEN), BF16), pltpu.VMEM((D_MODEL, EXPERT_HIDDEN), BF16),
                            pltpu.VMEM((EXPERT_HIDDEN, D_MODEL), BF16)]),
        out_shape=jax.ShapeDtypeStruct((n_blocks * MOE_BLK * PACKED_TILES, LANES), U32),
        compiler_params=_params(("arbitrary",)),
        name="experts",
    )(block_e, block_valid, n_used, xs, w_gate, w_up, w_down)


def _combine_kernel(dest_ref, next_ref, gate_ref, hp_ref, x_ref, gf_ref, g3_ref, sg_ref, su_ref, sd_ref,
                    ys_ref, o_ref, buf_ref, acc_ref, sem):
    c = PACKED_TILES
    tm = hp_ref.shape[0] // c
    i = pl.program_id(0)
    cur = i % 2

    parts = 4
    has_next = i + 1 < pl.num_programs(0)

    def gather(idx_ref, half, part):
        def issue(t, carry):
            slab = pl.ds(pl.multiple_of(t * c, c), c)
            for k in range(TOP_K):
                src = pl.multiple_of(idx_ref[0, 0, t * TOP_K + k] * c, c)
                pltpu.make_async_copy(ys_ref.at[pl.ds(src, c)], buf_ref.at[half, k, slab],
                                      sem.at[half]).start(priority=k % 2)
            return carry
        lax.fori_loop(part * (tm // parts), (part + 1) * (tm // parts), issue, 0)

    def gather_next(part):
        pl.when(has_next)(lambda: gather(next_ref, 1 - cur, part))

    @pl.when(i == 0)
    def _():
        for part in range(parts):
            gather(dest_ref, 0, part)

    gather_next(0)
    hb = _unpack_bf16_pairs(_from_row_tiles(hp_ref, c))
    hid = (_silu(jnp.dot(hb, sg_ref[...], preferred_element_type=F32))
           * jnp.dot(hb, su_ref[...], preferred_element_type=F32))
    f = jnp.dot(hid.astype(BF16), sd_ref[...], preferred_element_type=F32)
    gather_next(1)
    for k in range(TOP_K):
        pltpu.make_async_copy(ys_ref.at[pl.ds(0, tm * c)], buf_ref.at[cur, k], sem.at[cur]).wait()

    def weighted(t, carry):
        slab = pl.ds(pl.multiple_of(t * c, c), c)
        acc_hi = jnp.zeros((c, LANES), F32)
        acc_lo = jnp.zeros((c, LANES), F32)
        for k in range(TOP_K):
            packed = buf_ref[cur, k, slab, :]
            gk = gate_ref[0, 0, t * TOP_K + k]
            acc_hi = acc_hi + gk * lax.bitcast_convert_type(packed & jnp.uint32(0xFFFF0000), F32)
            acc_lo = acc_lo + gk * lax.bitcast_convert_type(packed << 16, F32)
        acc_ref[0, slab, :] = acc_hi
        acc_ref[1, slab, :] = acc_lo
        return carry

    lax.fori_loop(0, tm // 2, weighted, 0, unroll=4)
    gather_next(2)
    lax.fori_loop(tm // 2, tm, weighted, 0, unroll=4)
    gather_next(3)
    f = f + jnp.concatenate([_from_row_tiles(acc_ref, c, lead=(0,)), _from_row_tiles(acc_ref, c, lead=(1,))], axis=1)
    o_ref[...] = x_ref[...] + gf_ref[0] * _rms(f, g3_ref[...])


def _combine(lay, dest_flat, gate_flat, hp, x_mid, mods, g3, s_gate, s_up, s_down, ys, lat_only, compact_out):
    row = lambda i: (lay.tile(i, lat_only), 0)
    full = lambda shape: pl.BlockSpec(shape, lambda i: (0, 0))
    out_rows = lay.n_tiles(lat_only) * TM if compact_out else lay.rows
    out_map = (lambda i: (i, 0)) if compact_out else row
    n_steps = lay.n_tiles(lat_only)
    flat = lambda f: pl.BlockSpec((1, 1, TM * TOP_K), lambda i: (lay.tile(f(i), lat_only), 0, 0),
                                  memory_space=pltpu.SMEM)
    return pl.pallas_call(
        _combine_kernel,
        grid=(n_steps,),
        in_specs=[flat(lambda i: i), flat(lambda i: jnp.minimum(i + 1, n_steps - 1)), flat(lambda i: i),
                  pl.BlockSpec((TM * PACKED_TILES, LANES), row),
                  pl.BlockSpec((TM, D_MODEL), row), _mod_spec(lay, 5, lat_only),
                  full((1, D_MODEL)), full((D_MODEL, EXPERT_HIDDEN)), full((D_MODEL, EXPERT_HIDDEN)),
                  full((EXPERT_HIDDEN, D_MODEL)), pl.BlockSpec(memory_space=pl.ANY)],
        out_specs=pl.BlockSpec((TM, D_MODEL), out_map),
        scratch_shapes=[pltpu.VMEM((2, TOP_K, TM * PACKED_TILES, LANES), U32),
                        pltpu.VMEM((2, TM * PACKED_TILES, LANES), F32), pltpu.SemaphoreType.DMA((2,))],
        out_shape=jax.ShapeDtypeStruct((out_rows, D_MODEL), F32),
        compiler_params=_params(("arbitrary",)),
        name="combine",
    )(dest_flat, dest_flat, gate_flat, hp, x_mid, mods, g3, s_gate, s_up, s_down, ys)


def _rope_tables(lay):
    n_freq = HEAD_DIM // 4
    inv_freq = ROPE_THETA ** (-jnp.arange(n_freq, dtype=F32) / n_freq)
    t = jnp.arange(lay.n_lat)
    pos = jnp.stack([(t // GRID_W).astype(F32), (t % GRID_W).astype(F32)], axis=-1)
    ang = pos[:, :, None, None] * inv_freq
    ang = jnp.broadcast_to(ang, (lay.n_lat, 2, 2, n_freq)).reshape(lay.n_lat, HEAD_DIM)
    sign = jnp.tile(jnp.concatenate([-jnp.ones(n_freq, F32), jnp.ones(n_freq, F32)]), 2)
    cos = jnp.concatenate([jnp.ones((lay.n_ctx, HEAD_DIM), F32), jnp.cos(ang)], axis=0)
    sin = jnp.concatenate([jnp.zeros((lay.n_ctx, HEAD_DIM), F32), jnp.sin(ang) * sign], axis=0)
    return cos, sin


def _moe_plan(counts, n_blocks):
    counts = counts.reshape(-1)
    padded = (counts + MOE_BLK - 1) // MOE_BLK * MOE_BLK
    pad_end = jnp.cumsum(padded)
    pad_start = (pad_end - padded).astype(I32)
    blk_row = jnp.arange(n_blocks, dtype=I32) * MOE_BLK
    block_e = jnp.minimum(jnp.sum(blk_row[:, None] >= pad_end[None, :], axis=1), N_EXPERTS - 1).astype(I32)
    own = block_e[:, None] == jnp.arange(N_EXPERTS, dtype=I32)[None, :]
    count_b = jnp.sum(jnp.where(own, counts[None, :], 0), axis=1)
    start_b = jnp.sum(jnp.where(own, pad_start[None, :], 0), axis=1)
    block_valid = jnp.clip(count_b - (blk_row - start_b), 0, MOE_BLK).astype(I32)
    n_used = jnp.maximum(pad_end[-1:] // MOE_BLK, 1).astype(I32)
    return pad_start, block_e, block_valid, n_used


def kernel(x, c, ctx, c_ctx, mod_w, mod_b, norm_g, w_in, w_out, conv_w, conv_b, lru_wr, lru_br, lru_wi, lru_bi,
           lru_lambda, gla_w2, gla_b2, gla_norm_g, qk_norm_g, win_sink, router_w, router_bias, exp_w_gate,
           exp_w_up, exp_w_down, sh_w_gate, sh_w_up, sh_w_down):
    bsz, n_lat, d = x.shape
    n_ctx = ctx.shape[1]
    depth = mod_w.shape[0]
    assert d == D_MODEL and bsz < MOD_ROWS
    lay = _Layout(bsz, n_lat, n_ctx)

    cc = jnp.zeros((MOD_ROWS, d), F32).at[:bsz].set(c).at[bsz].set(c_ctx)
    mods_all = _adaln(cc, mod_w, mod_b)
    cos, sin = _rope_tables(lay)
    xt = None

    splits = np.cumsum([0, W_A, W_B_RAW, W_C, W_D])
    for layer in range(depth):
        last = layer == depth - 1
        need_ctx = not last
        lat_only = last
        mods = mods_all[layer].reshape(MOD_ROWS * 6, 1, d)
        g = norm_g[layer]
        wl = w_in[layer]
        w_cat = jnp.concatenate(
            [wl[:, splits[0]:splits[1]], wl[:, splits[1]:splits[2]], jnp.zeros((d, W_B - W_B_RAW), F32),
             wl[:, splits[2]:splits[3]], wl[:, splits[3]:splits[4]]], axis=1).astype(BF16)
        if xt is None:
            p_a, p_b, p_c, p_d, xt = _inproj(lay, (x.reshape(bsz * n_lat, d), ctx.reshape(bsz * n_ctx, d)), mods,
                                             g[0:1], w_cat)
        else:
            p_a, p_b, p_c, p_d = _inproj(lay, (xt,), mods, g[0:1], w_cat)

        y_a = _rglru(lay, p_a, conv_w[layer], conv_b[layer], lru_wr[layer], lru_br[layer], lru_wi[layer],
                     lru_bi[layer], lru_lambda[layer])
        w2pad = jnp.zeros((2, LANES, GLA_HEADS * GLA_DK), F32)
        w2pad = w2pad.at[0, :GLA_RANK].set(gla_w2[layer, 0]).at[1, GLA_RANK:2 * GLA_RANK].set(gla_w2[layer, 1])
        y_b = _gla(lay, p_b, w2pad, gla_b2[layer], gla_norm_g[layer])
        y_c = _attention(lay, p_c, cos, sin, qk_norm_g[layer], window=False, need_ctx=need_ctx)
        y_d = _attention(lay, p_d, cos, sin, win_sink[layer], window=True, need_ctx=need_ctx)

        x_mid, h, hp = _outproj(lay, (y_a, y_b, y_c, y_d), w_out[layer].astype(BF16), xt, mods, g[1:2], g[2:3],
                                lat_only)
        top_e, gate, rank, counts = _router(lay, h, router_w[layer].T, router_bias[layer], lat_only)
        n_assign = lay.n_tiles(lat_only) * TM * TOP_K
        n_blocks = -(-(n_assign + N_EXPERTS * (MOE_BLK - 1)) // MOE_BLK)
        pad_start, block_e, block_valid, n_used = _moe_plan(counts, n_blocks)
        owner = top_e[:, :, None] == jnp.arange(N_EXPERTS, dtype=I32)
        dest = jnp.sum(jnp.where(owner, pad_start, 0), axis=-1) + rank
        dest_flat = dest.T.reshape(lay.rows // TM, 1, TM * TOP_K)
        xs = _dispatch(lay, dest_flat, hp, n_blocks * MOE_BLK, lat_only)
        ys = _experts(block_e, block_valid, n_used, xs, exp_w_gate, exp_w_up, exp_w_down, layer)
        gate_flat = gate.T.reshape(lay.rows // TM, 1, TM * TOP_K)
        xt = _combine(lay, dest_flat, gate_flat, hp, x_mid, mods, g[3:4], sh_w_gate[layer].astype(BF16),
                      sh_w_up[layer].astype(BF16), sh_w_down[layer].astype(BF16), ys, lat_only, compact_out=last)
    return xt.reshape(bsz, n_lat, d)
```

```python
import functools

import numpy as np
import jax
import jax.numpy as jnp
from jax import lax
from jax.experimental import pallas as pl
from jax.experimental.pallas import tpu as pltpu

F32 = jnp.float32
BF16 = jnp.bfloat16
I32 = jnp.int32
U32 = jnp.uint32
HIGHEST = lax.Precision.HIGHEST
NT_DIMS = (((1,), (1,)), ((), ()))
TN_DIMS = (((0,), (0,)), ((), ()))

LANES = 128
SUBLANES = 8
VMEM_LIMIT_BYTES = 56 * 1024 * 1024

D_MODEL = 2048
GRID_W = 64
HEAD_DIM = 128
GROUP_WIDTH = D_MODEL // 4
LRU_BLOCKS = 4
CONV_W = 4
RGLRU_C = 8.0
GLA_HEADS = 4
GLA_DK = 64
GLA_DV = 128
GLA_RANK = 16
GLA_TAU = 16.0
GLA_CHUNK = 64
KV_HEADS = 2
WINDOW = 128
N_EXPERTS = 64
N_GROUPS = 8
TOPK_GROUPS = 4
TOP_K = 8
EXPERT_HIDDEN = 512
ROUTED_SCALE = 2.5
RMS_EPS = 1e-6
ROPE_THETA = 10000.0
NEG_INF = -1e30

W_A = 2 * GROUP_WIDTH
W_B_RAW = 2 * GLA_HEADS * GLA_DK + 2 * GLA_HEADS * GLA_DV + 2 * GLA_RANK
W_B = 1664
W_C = GROUP_WIDTH + 2 * KV_HEADS * HEAD_DIM
W_D = W_C
Z_COL_BLOCK = (2 * GLA_HEADS * GLA_DK + 2 * GLA_HEADS * GLA_DV) // LANES

TM = 256
MOE_BLK = 512
PACKED_TILES = D_MODEL // 2 // LANES
MOD_ROWS = 16


def _params(sem, vmem=VMEM_LIMIT_BYTES):
    return pltpu.CompilerParams(dimension_semantics=sem, vmem_limit_bytes=vmem)


def _rms(x, g):
    return x * lax.rsqrt(jnp.mean(x * x, axis=-1, keepdims=True) + RMS_EPS) * g


def _silu(x):
    return x * jax.nn.sigmoid(x)


def _pack_bf16_pairs(h):
    n = h.shape[1] // 2
    hb = h.astype(BF16).astype(F32)
    hi = lax.bitcast_convert_type(hb[:, :n], U32)
    lo = lax.bitcast_convert_type(hb[:, n:], U32)
    return hi | (lo >> 16)


def _to_row_tiles(ref, x):
    m = x.shape[0]
    c = x.shape[1] // LANES
    for j in range(c):
        ref[pl.ds(j, m, stride=c), :] = x[:, j * LANES:(j + 1) * LANES]


def _from_row_tiles(ref, c, lead=()):
    m = ref.shape[-2] // c
    return jnp.concatenate([ref[lead + (pl.ds(j, m, stride=c), slice(None))] for j in range(c)], axis=1)


def _unpack_bf16_pairs(p):
    hi = lax.bitcast_convert_type(p & jnp.uint32(0xFFFF0000), F32)
    lo = lax.bitcast_convert_type(p << 16, F32)
    return jnp.concatenate([hi, lo], axis=1).astype(BF16)


def _adaln_kernel(c_ref, w_ref, b_ref, o_ref):
    a = _silu(c_ref[...])
    o_ref[0] = jnp.dot(a, w_ref[0], preferred_element_type=F32, precision=HIGHEST) + b_ref[0]


def _adaln(cc, mod_w, mod_b):
    depth, d, n = mod_w.shape
    tn = 1024
    return pl.pallas_call(
        _adaln_kernel,
        grid=(depth, n // tn),
        in_specs=[pl.BlockSpec((MOD_ROWS, d), lambda l, j: (0, 0)),
                  pl.BlockSpec((1, d, tn), lambda l, j: (l, 0, j)),
                  pl.BlockSpec((1, 1, tn), lambda l, j: (l, 0, j))],
        out_specs=pl.BlockSpec((1, MOD_ROWS, tn), lambda l, j: (l, 0, j)),
        out_shape=jax.ShapeDtypeStruct((depth, MOD_ROWS, n), F32),
        compiler_params=_params(("arbitrary", "arbitrary")),
        name="adaln",
    )(cc, mod_w, mod_b.reshape(depth, 1, n))


class _Layout:
    def __init__(self, bsz, n_lat, n_ctx):
        self.bsz, self.n_lat, self.n_ctx = bsz, n_lat, n_ctx
        self.seq = n_ctx + n_lat
        self.rows = bsz * self.seq
        assert n_ctx % TM == 0 and n_lat % TM == 0
        self.tiles_b = self.seq // TM
        self.ctx_tiles = n_ctx // TM
        self.lat_tiles = n_lat // TM

    def n_tiles(self, lat_only):
        return self.bsz * (self.lat_tiles if lat_only else self.tiles_b)

    def tile(self, i, lat_only):
        if lat_only:
            return (i // self.lat_tiles) * self.tiles_b + self.ctx_tiles + i % self.lat_tiles
        return i

    def mod_row(self, i, lat_only):
        if lat_only:
            return i // self.lat_tiles
        return jnp.where(i % self.tiles_b < self.ctx_tiles, self.bsz, i // self.tiles_b)


def _mod_spec(lay, chunk, lat_only):
    return pl.BlockSpec((1, 1, D_MODEL), lambda i: (lay.mod_row(i, lat_only) * 6 + chunk, 0, 0))


def _inproj_kernel(*refs, tiles_b, ctx_tiles):
    if ctx_tiles:
        x_ref, c_ref, sh_ref, sc_ref, g_ref, w_ref, oa_ref, ob_ref, oc_ref, od_ref, xt_ref = refs
        x = jnp.where(pl.program_id(0) % tiles_b < ctx_tiles, c_ref[...], x_ref[...])
        xt_ref[...] = x
    else:
        x_ref, sh_ref, sc_ref, g_ref, w_ref, oa_ref, ob_ref, oc_ref, od_ref = refs
        x = x_ref[...]
    h = _rms(x, g_ref[...]) * (1.0 + sc_ref[0]) + sh_ref[0]
    hb = h.astype(BF16)
    c0 = 0
    for o_ref in (oa_ref, ob_ref, oc_ref, od_ref):
        width = o_ref.shape[1]
        o_ref[...] = jnp.dot(hb, w_ref[:, c0:c0 + width], preferred_element_type=F32)
        c0 += width


def _inproj(lay, xs, mods, g0, w_cat):
    widths = (W_A, W_B, W_C, W_D)
    row = lambda i: (i, 0)
    split = len(xs) == 2
    x_specs = [pl.BlockSpec((TM, D_MODEL), row)]
    out_specs = [pl.BlockSpec((TM, w), row) for w in widths]
    out_shape = [jax.ShapeDtypeStruct((lay.rows, w), F32) for w in widths]
    if split:
        lat = lambda i: ((i // lay.tiles_b) * lay.lat_tiles + jnp.maximum(i % lay.tiles_b - lay.ctx_tiles, 0), 0)
        ctx = lambda i: ((i // lay.tiles_b) * lay.ctx_tiles + jnp.minimum(i % lay.tiles_b, lay.ctx_tiles - 1), 0)
        x_specs = [pl.BlockSpec((TM, D_MODEL), lat), pl.BlockSpec((TM, D_MODEL), ctx)]
        out_specs.append(pl.BlockSpec((TM, D_MODEL), row))
        out_shape.append(jax.ShapeDtypeStruct((lay.rows, D_MODEL), F32))
    return pl.pallas_call(
        functools.partial(_inproj_kernel, tiles_b=lay.tiles_b, ctx_tiles=lay.ctx_tiles if split else 0),
        grid=(lay.n_tiles(False),),
        in_specs=x_specs + [_mod_spec(lay, 0, False), _mod_spec(lay, 1, False),
                            pl.BlockSpec((1, D_MODEL), lambda i: (0, 0)),
                            pl.BlockSpec((D_MODEL, sum(widths)), lambda i: (0, 0), pipeline_mode=pl.Buffered(1))],
        out_specs=out_specs,
        out_shape=out_shape,
        compiler_params=_params(("arbitrary",)),
        name="inproj",
    )(*xs, mods, mods, g0, w_cat)


def _rglru_kernel(x_ref, g_ref, cw_ref, cb_ref, wr_ref, br_ref, wi_ref, bi_ref, lam_ref, o_ref,
                  xs_ref, a_ref, b_ref, h_ref, *, n_ctx):
    seq = x_ref.shape[0]
    pad = SUBLANES
    n_tiles = seq // SUBLANES
    ctx_tiles = n_ctx // SUBLANES
    row = lax.broadcasted_iota(I32, (seq, LANES), 0)

    xs_ref[0:pad, :] = jnp.zeros((pad, LANES), F32)
    xs_ref[pad + seq:2 * pad + seq, :] = jnp.zeros((pad, LANES), F32)
    xs_ref[pad:pad + seq, :] = x_ref[...]
    u = jnp.zeros((seq, LANES), F32) + cb_ref[...]
    for j in range(CONV_W):
        off = j - 2
        xsh = xs_ref[pad + off:pad + off + seq, :]
        if off < 0:
            xsh = jnp.where((row >= n_ctx) & (row + off < n_ctx), 0.0, xsh)
        elif off > 0:
            xsh = jnp.where((row < n_ctx) & (row + off >= n_ctx), 0.0, xsh)
        u = u + cw_ref[j:j + 1, :] * xsh

    ub = u.astype(BF16)
    tile_shape = (n_tiles, SUBLANES, LANES)
    rmod = lax.broadcasted_iota(I32, tile_shape, 1)
    for d in range(2):
        reverse = d == 1
        r = jax.nn.sigmoid(jnp.dot(ub, wr_ref[d, 0].astype(BF16), preferred_element_type=F32) + br_ref[d:d + 1, :])
        gi = jax.nn.sigmoid(jnp.dot(ub, wi_ref[d, 0].astype(BF16), preferred_element_type=F32) + bi_ref[d:d + 1, :])
        lam = lam_ref[d:d + 1, :]
        softplus_neg = jnp.maximum(-lam, 0.0) + jnp.log(1.0 + jnp.exp(-jnp.abs(lam)))
        a = jnp.exp(-RGLRU_C * r * softplus_neg)
        b = jnp.sqrt(1.0 - a * a) * (gi * u)
        a = a.reshape(tile_shape)
        b = b.reshape(tile_shape)
        for s in (1, 2, 4):
            if reverse:
                ap, bp, ok = pltpu.roll(a, SUBLANES - s, 1), pltpu.roll(b, SUBLANES - s, 1), rmod < SUBLANES - s
            else:
                ap, bp, ok = pltpu.roll(a, s, 1), pltpu.roll(b, s, 1), rmod >= s
            b = jnp.where(ok, a * bp + b, b)
            a = jnp.where(ok, a * ap, a)
        a_ref[...] = a.reshape(seq, LANES)
        b_ref[...] = b.reshape(seq, LANES)

        def step(t, h, reverse=reverse):
            r0 = pl.multiple_of(t * SUBLANES, SUBLANES)
            h8 = a_ref[pl.ds(r0, SUBLANES), :] * h + b_ref[pl.ds(r0, SUBLANES), :]
            if reverse:
                h_ref[pl.ds(r0, SUBLANES), :] = h_ref[pl.ds(r0, SUBLANES), :] + h8
                return jnp.broadcast_to(h8[0:1, :], (SUBLANES, LANES))
            h_ref[pl.ds(r0, SUBLANES), :] = h8
            return jnp.broadcast_to(h8[SUBLANES - 1:SUBLANES, :], (SUBLANES, LANES))

        h0 = jnp.zeros((SUBLANES, LANES), F32)
        if reverse:
            hc = lax.fori_loop(0, ctx_tiles, lambda i, h: step(ctx_tiles - 1 - i, h), h0)
            lax.fori_loop(0, n_tiles - ctx_tiles, lambda i, h: step(n_tiles - 1 - i, h), hc)
        else:
            lax.fori_loop(0, n_tiles, step, h0)

    g = g_ref[...]
    gelu = 0.5 * g * (1.0 + jnp.tanh(np.sqrt(2.0 / np.pi) * (g + 0.044715 * (g * g * g))))
    o_ref[...] = h_ref[...] * gelu


def _rglru(lay, p_a, conv_w, conv_b, wr, br, wi, bi, lam):
    seq = lay.seq
    nb = LRU_BLOCKS
    col = lambda off: (lambda b, j: (b, off + j))
    vec = lambda rows: pl.BlockSpec((rows, LANES), lambda b, j: (0, j))
    wspec = pl.BlockSpec((2, 1, LANES, LANES), lambda b, j: (0, j, 0, 0))
    return pl.pallas_call(
        functools.partial(_rglru_kernel, n_ctx=lay.n_ctx),
        grid=(lay.bsz, nb),
        in_specs=[pl.BlockSpec((seq, LANES), col(0)), pl.BlockSpec((seq, LANES), col(nb)),
                  vec(CONV_W), vec(1), wspec, vec(2), wspec, vec(2), vec(2)],
        out_specs=pl.BlockSpec((seq, LANES), col(0)),
        out_shape=jax.ShapeDtypeStruct((lay.rows, GROUP_WIDTH), F32),
        scratch_shapes=[pltpu.VMEM((seq + 2 * SUBLANES, LANES), F32), pltpu.VMEM((seq, LANES), F32),
                        pltpu.VMEM((seq, LANES), F32), pltpu.VMEM((seq, LANES), F32)],
        compiler_params=_params(("arbitrary", "arbitrary")),
        name="rglru",
    )(p_a, p_a, conv_w, conv_b.reshape(1, -1), wr, br, wi, bi, lam)


def _gla_kernel(q_ref, k_ref, v_ref, g_ref, z_ref, w2_ref, b2_ref, ng_ref, o_ref,
                laf_ref, lab_ref, of_ref, ob_ref, *, n_ctx):
    seq = q_ref.shape[0]
    cs = GLA_CHUNK
    n_chunks = seq // cs
    ctx_chunks = n_ctx // cs
    z = z_ref[...]
    for d, la_ref in ((0, laf_ref), (1, lab_ref)):
        pre = jnp.dot(z, w2_ref[d], preferred_element_type=F32, precision=HIGHEST) + b2_ref[d:d + 1, :]
        log_sig = jnp.minimum(pre, 0.0) - jnp.log(1.0 + jnp.exp(-jnp.abs(pre)))
        la_ref[...] = log_sig * (1.0 / GLA_TAU)

    ri = lax.broadcasted_iota(I32, (cs, cs), 0)
    ci = lax.broadcasted_iota(I32, (cs, cs), 1)
    scale = GLA_DK ** -0.5

    def chunk(c, states, reverse):
        r0 = pl.multiple_of(c * cs, cs)
        rows = pl.ds(r0, cs)
        keep = (ri <= ci) if reverse else (ri >= ci)
        la = (lab_ref if reverse else laf_ref)[rows, :]
        la_hi = la.astype(BF16)
        rest = la - la_hi.astype(F32)
        la_mid = rest.astype(BF16)
        la_lo = (rest - la_mid.astype(F32)).astype(BF16)
        parts = jnp.dot(keep.astype(BF16), jnp.concatenate([la_hi, la_mid, la_lo], axis=1),
                        preferred_element_type=F32)
        cum = parts[:, 0:LANES] + parts[:, LANES:2 * LANES] + parts[:, 2 * LANES:3 * LANES]
        last = cum[0:1, :] if reverse else cum[cs - 1:cs, :]
        q = q_ref[rows, :] * scale
        k = k_ref[rows, :]
        qe = q * jnp.exp(cum)
        ke = k * jnp.exp(-cum)
        kl = k * jnp.exp(last - cum)
        el = jnp.exp(last)
        out_ref = ob_ref if reverse else of_ref
        new_states = []
        for hh in range(2):
            dk = slice(hh * GLA_DK, (hh + 1) * GLA_DK)
            dv = slice(hh * GLA_DV, (hh + 1) * GLA_DV)
            qh = qe[:, dk].astype(BF16)
            vh = v_ref[rows, dv].astype(BF16)
            st = states[hh]
            s = lax.dot_general(qh, ke[:, dk].astype(BF16), NT_DIMS, preferred_element_type=F32)
            s = jnp.where(keep, s, 0.0)
            o = (jnp.dot(s.astype(BF16), vh, preferred_element_type=F32)
                 + lax.dot_general(qh, st.astype(BF16), NT_DIMS, preferred_element_type=F32))
            out_ref[rows, dv] = o
            new_states.append(st * el[:, dk]
                              + lax.dot_general(vh, kl[:, dk].astype(BF16), TN_DIMS, preferred_element_type=F32))
        return tuple(new_states)

    def body(i, carry):
        sf, sb = carry
        sf = chunk(i, sf, False)
        cb = jnp.where(i < ctx_chunks, ctx_chunks - 1 - i, n_chunks - 1 + ctx_chunks - i)
        sb = chunk(cb, sb, True)
        return sf, sb

    zero = jnp.zeros((GLA_DV, GLA_DK), F32)
    lax.fori_loop(0, n_chunks, body, ((zero, zero), (zero, zero)))

    ng = ng_ref[...]
    for hh in range(2):
        dv = slice(hh * GLA_DV, (hh + 1) * GLA_DV)
        o = of_ref[:, dv] + ob_ref[:, dv]
        o_ref[:, dv] = _rms(o, ng) * _silu(g_ref[:, dv])


def _gla(lay, p_b, w2pad, b2, norm_g):
    seq = lay.seq
    hp = GLA_HEADS // 2
    qk_w = 2 * GLA_DK
    v_w = 2 * GLA_DV
    k_off = GLA_HEADS * GLA_DK // qk_w
    v_off = 2 * GLA_HEADS * GLA_DK // v_w
    g_off = v_off + GLA_HEADS * GLA_DV // v_w
    blk = lambda w, off: pl.BlockSpec((seq, w), lambda b, j: (b, off + j))
    return pl.pallas_call(
        functools.partial(_gla_kernel, n_ctx=lay.n_ctx),
        grid=(lay.bsz, hp),
        in_specs=[blk(qk_w, 0), blk(qk_w, k_off), blk(v_w, v_off), blk(v_w, g_off),
                  pl.BlockSpec((seq, LANES), lambda b, j: (b, Z_COL_BLOCK)),
                  pl.BlockSpec((2, LANES, qk_w), lambda b, j: (0, 0, j)),
                  pl.BlockSpec((2, qk_w), lambda b, j: (0, j)),
                  pl.BlockSpec((1, GLA_DV), lambda b, j: (0, 0))],
        out_specs=blk(v_w, 0),
        out_shape=jax.ShapeDtypeStruct((lay.rows, GROUP_WIDTH), F32),
        scratch_shapes=[pltpu.VMEM((seq, qk_w), F32), pltpu.VMEM((seq, qk_w), F32),
                        pltpu.VMEM((seq, v_w), F32), pltpu.VMEM((seq, v_w), F32)],
        compiler_params=_params(("arbitrary", "arbitrary")),
        name="gla",
    )(p_b, p_b, p_b, p_b, p_b, w2pad, b2, norm_g.reshape(1, -1))


def _rope(x, cos, sin_signed):
    lane = lax.broadcasted_iota(I32, x.shape, 1)
    partner = jnp.where((lane & 32) != 0, pltpu.roll(x, 32, 1), pltpu.roll(x, HEAD_DIM - 32, 1))
    return x * cos + partner * sin_signed


def _attn_kernel(*refs, window, need_ctx, tq, n_ctx):
    if window:
        sink_ref, q_ref, k_ref, v_ref, cos_ref, sin_ref, o_ref, ks_ref, vs_ref = refs
        qkg_ref = None
    else:
        q_ref, k_ref, v_ref, cos_ref, sin_ref, qkg_ref, o_ref, ks_ref, vs_ref = refs
    seq = k_ref.shape[0]
    n_lat = seq - n_ctx
    g = pl.program_id(1)
    qi = pl.program_id(2)
    jq = qi if need_ctx else qi + n_ctx // tq

    @pl.when(qi == 0)
    def _():
        k = k_ref[...]
        if not window:
            k = _rms(k, qkg_ref[1:2, :])
        ks_ref[...] = _rope(k, cos_ref[...], sin_ref[...]).astype(BF16)
        vs_ref[...] = v_ref[...].astype(BF16)

    r0 = pl.multiple_of(jq * tq, tq)
    cos = cos_ref[pl.ds(r0, tq), :]
    sin = sin_ref[pl.ds(r0, tq), :]
    heads = []
    for hh in range(2):
        qh = q_ref[:, hh * HEAD_DIM:(hh + 1) * HEAD_DIM]
        if not window:
            qh = _rms(qh, qkg_ref[0:1, :])
        heads.append((_rope(qh, cos, sin) * HEAD_DIM ** -0.5).astype(BF16))
    q2 = jnp.concatenate(heads, axis=0)

    def finish(o):
        o_ref[:, 0:HEAD_DIM] = o[0:tq]
        o_ref[:, HEAD_DIM:2 * HEAD_DIM] = o[tq:2 * tq]

    def scores(lo, n):
        return lax.dot_general(q2, ks_ref[pl.ds(lo, n), :], NT_DIMS, preferred_element_type=F32)

    def pv(p, lo, n):
        return jnp.dot(p.astype(BF16), vs_ref[pl.ds(lo, n), :], preferred_element_type=F32)

    if window:
        rowi = lax.broadcasted_iota(I32, (2 * tq, 1), 0)
        sink = jnp.where(rowi < tq, sink_ref[2 * g], sink_ref[2 * g + 1])

    def ctx_tile():
        s = scores(0, n_ctx)
        m = jnp.max(s, axis=-1, keepdims=True)
        if window:
            m = jnp.maximum(m, sink)
        p = jnp.exp(s - m)
        l = jnp.sum(p, axis=-1, keepdims=True)
        if window:
            l = l + jnp.exp(sink - m)
        finish(pv(p, 0, n_ctx) / l)

    def lat_tile():
        if not window:
            s = scores(0, seq)
            m = jnp.max(s, axis=-1, keepdims=True)
            p = jnp.exp(s - m)
            finish(pv(p, 0, seq) / jnp.sum(p, axis=-1, keepdims=True))
            return
        band = tq + 2 * WINDOW
        jl = jq - n_ctx // tq
        start = jnp.clip(jl * tq - WINDOW, 0, n_lat - band)
        lo = pl.multiple_of(n_ctx + start, WINDOW)
        sb = scores(lo, band)
        q_pos = jl * tq + lax.broadcasted_iota(I32, (2 * tq, band), 0) % tq
        k_pos = start + lax.broadcasted_iota(I32, (2 * tq, band), 1)
        sb = jnp.where(jnp.abs(q_pos - k_pos) <= WINDOW, sb, NEG_INF)
        sc = scores(0, n_ctx)
        m = jnp.maximum(jnp.maximum(jnp.max(sb, axis=-1, keepdims=True), jnp.max(sc, axis=-1, keepdims=True)), sink)
        pb = jnp.exp(sb - m)
        pc = jnp.exp(sc - m)
        l = jnp.sum(pb, axis=-1, keepdims=True) + jnp.sum(pc, axis=-1, keepdims=True) + jnp.exp(sink - m)
        finish((pv(pb, lo, band) + pv(pc, 0, n_ctx)) / l)

    if need_ctx:
        pl.when(jq * tq < n_ctx)(ctx_tile)
        pl.when(jq * tq >= n_ctx)(lat_tile)
    else:
        lat_tile()


def _attention(lay, p, cos, sin, extra, *, window, need_ctx):
    seq = lay.seq
    tq = TM
    q_tiles_b = seq // tq
    nq = q_tiles_b if need_ctx else lay.n_lat // tq
    skip = 0 if need_ctx else lay.n_ctx // tq
    q_heads_w = 2 * HEAD_DIM
    k_off = GROUP_WIDTH // HEAD_DIM
    v_off = k_off + KV_HEADS

    def im(f):
        return (lambda b, g, qi, *_: f(b, g, qi))

    in_specs = [pl.BlockSpec((tq, q_heads_w), im(lambda b, g, qi: (b * q_tiles_b + skip + qi, g))),
                pl.BlockSpec((seq, HEAD_DIM), im(lambda b, g, qi: (b, k_off + g))),
                pl.BlockSpec((seq, HEAD_DIM), im(lambda b, g, qi: (b, v_off + g))),
                pl.BlockSpec((seq, HEAD_DIM), im(lambda b, g, qi: (0, 0))),
                pl.BlockSpec((seq, HEAD_DIM), im(lambda b, g, qi: (0, 0)))]
    args = [p, p, p, cos, sin]
    if not window:
        in_specs.append(pl.BlockSpec((2, HEAD_DIM), im(lambda b, g, qi: (0, 0))))
        args.append(extra)
    out_spec = pl.BlockSpec((tq, q_heads_w), im(lambda b, g, qi: (b * q_tiles_b + skip + qi, g)))
    kern = functools.partial(_attn_kernel, window=window, need_ctx=need_ctx, tq=tq, n_ctx=lay.n_ctx)
    scratch = [pltpu.VMEM((seq, HEAD_DIM), BF16), pltpu.VMEM((seq, HEAD_DIM), BF16)]
    out_shape = jax.ShapeDtypeStruct((lay.rows, GROUP_WIDTH), F32)
    sem = ("arbitrary", "arbitrary", "arbitrary")
    grid = (lay.bsz, KV_HEADS, nq)
    if window:
        return pl.pallas_call(
            kern,
            grid_spec=pltpu.PrefetchScalarGridSpec(num_scalar_prefetch=1, grid=grid, in_specs=in_specs,
                                                   out_specs=out_spec, scratch_shapes=scratch),
            out_shape=out_shape, compiler_params=_params(sem), name="win_attn",
        )(extra, *args)
    return pl.pallas_call(kern, grid=grid, in_specs=in_specs, out_specs=out_spec, scratch_shapes=scratch,
                          out_shape=out_shape, compiler_params=_params(sem), name="glob_attn")(*args)


def _outproj_kernel(ya_ref, yb_ref, yc_ref, yd_ref, w_ref, x_ref, gm_ref, sh_ref, sc_ref, g1_ref, g2_ref,
                    xo_ref, h_ref, hp_ref):
    acc = None
    for i, y_ref in enumerate((ya_ref, yb_ref, yc_ref, yd_ref)):
        part = jnp.dot(y_ref[...].astype(BF16), w_ref[i * GROUP_WIDTH:(i + 1) * GROUP_WIDTH, :],
                       preferred_element_type=F32)
        acc = part if acc is None else acc + part
    x_new = x_ref[...] + gm_ref[0] * _rms(acc, g1_ref[...])
    xo_ref[...] = x_new
    h = _rms(x_new, g2_ref[...]) * (1.0 + sc_ref[0]) + sh_ref[0]
    h_ref[...] = h
    _to_row_tiles(hp_ref, _pack_bf16_pairs(h))


def _outproj(lay, ys, w_out, xt, mods, g1, g2, lat_only):
    row = lambda i: (lay.tile(i, lat_only), 0)
    full = lambda shape: pl.BlockSpec(shape, lambda i: (0, 0))
    yspec = pl.BlockSpec((TM, GROUP_WIDTH), row)
    xspec = pl.BlockSpec((TM, D_MODEL), row)
    return pl.pallas_call(
        _outproj_kernel,
        grid=(lay.n_tiles(lat_only),),
        in_specs=[yspec, yspec, yspec, yspec, full((D_MODEL, D_MODEL)), xspec,
                  _mod_spec(lay, 2, lat_only), _mod_spec(lay, 3, lat_only), _mod_spec(lay, 4, lat_only),
                  full((1, D_MODEL)), full((1, D_MODEL))],
        out_specs=[xspec, xspec, pl.BlockSpec((TM * PACKED_TILES, LANES), row)],
        out_shape=[jax.ShapeDtypeStruct((lay.rows, D_MODEL), F32), jax.ShapeDtypeStruct((lay.rows, D_MODEL), F32),
                   jax.ShapeDtypeStruct((lay.rows * PACKED_TILES, LANES), U32)],
        compiler_params=_params(("arbitrary",)),
        name="outproj",
    )(*ys, w_out, xt, mods, mods, mods, g1, g2)


def _route(h, wt_ref, bias_ref, e_ref, gate_ref, rank_ref, cnt_ref, carry_ref):
    i = pl.program_id(0)
    tm = h.shape[0]
    ne = N_EXPERTS
    gsz = ne // N_GROUPS

    @pl.when(i == 0)
    def _():
        carry_ref[...] = jnp.zeros_like(carry_ref)

    logits = lax.dot_general(wt_ref[...], h, NT_DIMS, preferred_element_type=F32, precision=HIGHEST)
    scores = jax.nn.sigmoid(logits)
    biased = scores + bias_ref[...]

    sub = lax.broadcasted_iota(I32, (gsz, tm), 0)
    gs_rows = []
    for grp in range(N_GROUPS):
        blk = biased[grp * gsz:(grp + 1) * gsz, :]
        m1 = jnp.max(blk, axis=0, keepdims=True)
        first = jnp.min(jnp.where(blk == m1, sub, gsz), axis=0, keepdims=True)
        m2 = jnp.max(jnp.where(sub == first, -jnp.inf, blk), axis=0, keepdims=True)
        gs_rows.append(m1 + m2)
    gs = jnp.concatenate(gs_rows, axis=0)

    gidx = lax.broadcasted_iota(I32, (N_GROUPS, tm), 0)
    chosen = jnp.zeros((N_GROUPS, tm), jnp.bool_)
    cur = gs
    for _ in range(TOPK_GROUPS):
        m = jnp.max(cur, axis=0, keepdims=True)
        first = jnp.min(jnp.where(cur == m, gidx, N_GROUPS), axis=0, keepdims=True)
        sel = gidx == first
        chosen = chosen | sel
        cur = jnp.where(sel, -jnp.inf, cur)

    eidx = lax.broadcasted_iota(I32, (ne, tm), 0)
    eligible = jnp.concatenate(
        [jnp.broadcast_to(chosen[grp:grp + 1, :], (gsz, tm)) for grp in range(N_GROUPS)], axis=0)
    cur = jnp.where(eligible, biased, -jnp.inf)
    sels, gates = [], []
    onehot = jnp.zeros((ne, tm), jnp.bool_)
    for k in range(TOP_K):
        m = jnp.max(cur, axis=0, keepdims=True)
        first = jnp.min(jnp.where(cur == m, eidx, ne), axis=0, keepdims=True)
        sel = eidx == first
        e_ref[k:k + 1, :] = first
        gates.append(jnp.sum(jnp.where(sel, scores, 0.0), axis=0, keepdims=True))
        sels.append(sel)
        onehot = onehot | sel
        cur = jnp.where(sel, -jnp.inf, cur)
    total = gates[0]
    for gk in gates[1:]:
        total = total + gk
    for k in range(TOP_K):
        gate_ref[k:k + 1, :] = gates[k] / total * ROUTED_SCALE

    tr = lax.broadcasted_iota(I32, (tm, tm), 0)
    tc = lax.broadcasted_iota(I32, (tm, tm), 1)
    before = (tr < tc).astype(BF16)
    oh = onehot.astype(F32)
    prefix = jnp.dot(oh.astype(BF16), before, preferred_element_type=F32) + carry_ref[...]
    for k in range(TOP_K):
        rank_ref[k:k + 1, :] = jnp.sum(jnp.where(sels[k], prefix, 0.0), axis=0, keepdims=True).astype(I32)
    carry = carry_ref[...] + jnp.sum(oh, axis=1, keepdims=True)
    carry_ref[...] = carry
    cnt_ref[...] = carry.astype(I32)


def _router_kernel(h_ref, wt_ref, bias_ref, e_ref, gate_ref, rank_ref, cnt_ref, carry_ref):
    _route(h_ref[...], wt_ref, bias_ref, e_ref, gate_ref, rank_ref, cnt_ref, carry_ref)


def _router(lay, h, router_wt, bias, lat_only):
    col = lambda i: (0, lay.tile(i, lat_only))
    kt = pl.BlockSpec((TOP_K, TM), col)
    return pl.pallas_call(
        _router_kernel,
        grid=(lay.n_tiles(lat_only),),
        in_specs=[pl.BlockSpec((TM, D_MODEL), lambda i: (lay.tile(i, lat_only), 0)),
                  pl.BlockSpec((N_EXPERTS, D_MODEL), lambda i: (0, 0)),
                  pl.BlockSpec((N_EXPERTS, 1), lambda i: (0, 0))],
        out_specs=[kt, kt, kt, pl.BlockSpec((N_EXPERTS, 1), lambda i: (0, 0))],
        out_shape=[jax.ShapeDtypeStruct((TOP_K, lay.rows), I32), jax.ShapeDtypeStruct((TOP_K, lay.rows), F32),
                   jax.ShapeDtypeStruct((TOP_K, lay.rows), I32), jax.ShapeDtypeStruct((N_EXPERTS, 1), I32)],
        scratch_shapes=[pltpu.VMEM((N_EXPERTS, 1), F32)],
        compiler_params=_params(("arbitrary",)),
        name="router",
    )(h, router_wt, bias.reshape(-1, 1))


def _dispatch_kernel(dest_ref, hp_ref, xs_ref, sem):
    c = PACKED_TILES
    tm = hp_ref.shape[0] // c

    def issue(t, carry):
        src = hp_ref.at[pl.ds(pl.multiple_of(t * c, c), c)]
        for k in range(TOP_K):
            dest = pl.multiple_of(dest_ref[0, 0, t * TOP_K + k], c)
            pltpu.make_async_copy(src, xs_ref.at[pl.ds(dest, c)], sem).start(priority=k % 2)
        return carry

    lax.fori_loop(0, tm, issue, 0, unroll=2)
    for k in range(TOP_K):
        pltpu.make_async_copy(hp_ref, xs_ref.at[pl.ds(0, tm * c)], sem).wait()


def _dispatch(lay, dest_flat, hp, n_rows, lat_only):
    return pl.pallas_call(
        _dispatch_kernel,
        grid=(lay.n_tiles(lat_only),),
        in_specs=[pl.BlockSpec((1, 1, TM * TOP_K), lambda i: (lay.tile(i, lat_only), 0, 0), memory_space=pltpu.SMEM),
                  pl.BlockSpec((TM * PACKED_TILES, LANES), lambda i: (lay.tile(i, lat_only), 0))],
        out_specs=pl.BlockSpec(memory_space=pl.ANY),
        scratch_shapes=[pltpu.SemaphoreType.DMA],
        out_shape=jax.ShapeDtypeStruct((n_rows * PACKED_TILES, LANES), U32),
        compiler_params=_params(("arbitrary",)),
        name="dispatch",
    )(dest_flat, hp)


def _experts_kernel(be_ref, bv_ref, nu_ref, xs_ref, wg_ref, wu_ref, wd_ref, ys_ref, wgb_ref, wub_ref, wdb_ref):
    j = pl.program_id(0)
    live = j < nu_ref[0]
    prev = be_ref[jnp.maximum(j - 1, 0)]

    @pl.when(live & ((j == 0) | (be_ref[j] != prev)))
    def _():
        wgb_ref[...] = wg_ref[0, 0].astype(BF16)
        wub_ref[...] = wu_ref[0, 0].astype(BF16)
        wdb_ref[...] = wd_ref[0, 0].astype(BF16)

    @pl.when(live)
    def _():
        packed = _from_row_tiles(xs_ref, PACKED_TILES)
        rowi = lax.broadcasted_iota(I32, packed.shape, 0)
        packed = jnp.where(rowi < bv_ref[j], packed, jnp.uint32(0))
        x = _unpack_bf16_pairs(packed)
        hid = (_silu(jnp.dot(x, wgb_ref[...], preferred_element_type=F32))
               * jnp.dot(x, wub_ref[...], preferred_element_type=F32))
        y = jnp.dot(hid.astype(BF16), wdb_ref[...], preferred_element_type=F32)
        _to_row_tiles(ys_ref, _pack_bf16_pairs(y))


def _experts(block_e, block_valid, n_used, xs, w_gate, w_up, w_down, layer):
    n_blocks = xs.shape[0] // (MOE_BLK * PACKED_TILES)
    last = lambda j, nu: jnp.minimum(j, nu[0] - 1)
    wmap = lambda j, be, bv, nu: (layer, be[last(j, nu)], 0, 0)
    return pl.pallas_call(
        _experts_kernel,
        grid_spec=pltpu.PrefetchScalarGridSpec(
            num_scalar_prefetch=3, grid=(n_blocks,),
            in_specs=[pl.BlockSpec((MOE_BLK * PACKED_TILES, LANES), lambda j, be, bv, nu: (last(j, nu), 0)),
                      pl.BlockSpec((1, 1, D_MODEL, EXPERT_HIDDEN), wmap),
                      pl.BlockSpec((1, 1, D_MODEL, EXPERT_HIDDEN), wmap),
                      pl.BlockSpec((1, 1, EXPERT_HIDDEN, D_MODEL), wmap)],
            out_specs=pl.BlockSpec((MOE_BLK * PACKED_TILES, LANES), lambda j, be, bv, nu: (last(j, nu), 0)),
            scratch_shapes=[pltpu.VMEM((D_MODEL, EXPERT_HIDDEN), BF16), pltpu.VMEM((D_MODEL, EXPERT_HIDDEN), BF16),
                            pltpu.VMEM((EXPERT_HIDDEN, D_MODEL), BF16)]),
        out_shape=jax.ShapeDtypeStruct((n_blocks * MOE_BLK * PACKED_TILES, LANES), U32),
        compiler_params=_params(("arbitrary",)),
        name="experts",
    )(block_e, block_valid, n_used, xs, w_gate, w_up, w_down)


def _combine_kernel(dest_ref, next_ref, gate_ref, hp_ref, x_ref, gf_ref, g3_ref, sg_ref, su_ref, sd_ref,
                    ys_ref, o_ref, buf_ref, acc_ref, sem):
    c = PACKED_TILES
    tm = hp_ref.shape[0] // c
    i = pl.program_id(0)
    cur = i % 2

    parts = 4
    has_next = i + 1 < pl.num_programs(0)

    def gather(idx_ref, half, part):
        def issue(t, carry):
            slab = pl.ds(pl.multiple_of(t * c, c), c)
            for k in range(TOP_K):
                src = pl.multiple_of(idx_ref[0, 0, t * TOP_K + k], c)
                pltpu.make_async_copy(ys_ref.at[pl.ds(src, c)], buf_ref.at[half, k, slab],
                                      sem.at[half]).start(priority=k % 2)
            return carry
        lax.fori_loop(part * (tm // parts), (part + 1) * (tm // parts), issue, 0)

    def gather_next(part):
        pl.when(has_next)(lambda: gather(next_ref, 1 - cur, part))

    @pl.when(i == 0)
    def _():
        for part in range(parts):
            gather(dest_ref, 0, part)

    gather_next(0)
    hb = _unpack_bf16_pairs(_from_row_tiles(hp_ref, c))
    hid = (_silu(jnp.dot(hb, sg_ref[...], preferred_element_type=F32))
           * jnp.dot(hb, su_ref[...], preferred_element_type=F32))
    f = jnp.dot(hid.astype(BF16), sd_ref[...], preferred_element_type=F32)
    gather_next(1)
    for k in range(TOP_K):
        pltpu.make_async_copy(ys_ref.at[pl.ds(0, tm * c)], buf_ref.at[cur, k], sem.at[cur]).wait()

    def weighted(t, carry):
        slab = pl.ds(pl.multiple_of(t * c, c), c)
        acc_hi = jnp.zeros((c, LANES), F32)
        acc_lo = jnp.zeros((c, LANES), F32)
        for k in range(TOP_K):
            packed = buf_ref[cur, k, slab, :]
            gk = gate_ref[0, 0, t * TOP_K + k]
            acc_hi = acc_hi + gk * lax.bitcast_convert_type(packed & jnp.uint32(0xFFFF0000), F32)
            acc_lo = acc_lo + gk * lax.bitcast_convert_type(packed << 16, F32)
        acc_ref[0, slab, :] = acc_hi
        acc_ref[1, slab, :] = acc_lo
        return carry

    lax.fori_loop(0, tm // 2, weighted, 0, unroll=4)
    gather_next(2)
    lax.fori_loop(tm // 2, tm, weighted, 0, unroll=4)
    gather_next(3)
    f = f + jnp.concatenate([_from_row_tiles(acc_ref, c, lead=(0,)), _from_row_tiles(acc_ref, c, lead=(1,))], axis=1)
    o_ref[...] = x_ref[...] + gf_ref[0] * _rms(f, g3_ref[...])


def _combine(lay, dest_flat, gate_flat, hp, x_mid, mods, g3, s_gate, s_up, s_down, ys, lat_only, compact_out):
    row = lambda i: (lay.tile(i, lat_only), 0)
    full = lambda shape: pl.BlockSpec(shape, lambda i: (0, 0))
    out_rows = lay.n_tiles(lat_only) * TM if compact_out else lay.rows
    out_map = (lambda i: (i, 0)) if compact_out else row
    n_steps = lay.n_tiles(lat_only)
    flat = lambda f: pl.BlockSpec((1, 1, TM * TOP_K), lambda i: (lay.tile(f(i), lat_only), 0, 0),
                                  memory_space=pltpu.SMEM)
    return pl.pallas_call(
        _combine_kernel,
        grid=(n_steps,),
        in_specs=[flat(lambda i: i), flat(lambda i: jnp.minimum(i + 1, n_steps - 1)), flat(lambda i: i),
                  pl.BlockSpec((TM * PACKED_TILES, LANES), row),
                  pl.BlockSpec((TM, D_MODEL), row), _mod_spec(lay, 5, lat_only),
                  full((1, D_MODEL)), full((D_MODEL, EXPERT_HIDDEN)), full((D_MODEL, EXPERT_HIDDEN)),
                  full((EXPERT_HIDDEN, D_MODEL)), pl.BlockSpec(memory_space=pl.ANY)],
        out_specs=pl.BlockSpec((TM, D_MODEL), out_map),
        scratch_shapes=[pltpu.VMEM((2, TOP_K, TM * PACKED_TILES, LANES), U32),
                        pltpu.VMEM((2, TM * PACKED_TILES, LANES), F32), pltpu.SemaphoreType.DMA((2,))],
        out_shape=jax.ShapeDtypeStruct((out_rows, D_MODEL), F32),
        compiler_params=_params(("arbitrary",)),
        name="combine",
    )(dest_flat, dest_flat, gate_flat, hp, x_mid, mods, g3, s_gate, s_up, s_down, ys)


def _rope_tables(lay):
    n_freq = HEAD_DIM // 4
    inv_freq = ROPE_THETA ** (-jnp.arange(n_freq, dtype=F32) / n_freq)
    t = jnp.arange(lay.n_lat)
    pos = jnp.stack([(t // GRID_W).astype(F32), (t % GRID_W).astype(F32)], axis=-1)
    ang = pos[:, :, None, None] * inv_freq
    ang = jnp.broadcast_to(ang, (lay.n_lat, 2, 2, n_freq)).reshape(lay.n_lat, HEAD_DIM)
    sign = jnp.tile(jnp.concatenate([-jnp.ones(n_freq, F32), jnp.ones(n_freq, F32)]), 2)
    cos = jnp.concatenate([jnp.ones((lay.n_ctx, HEAD_DIM), F32), jnp.cos(ang)], axis=0)
    sin = jnp.concatenate([jnp.zeros((lay.n_ctx, HEAD_DIM), F32), jnp.sin(ang) * sign], axis=0)
    return cos, sin


def _moe_plan(counts, n_blocks):
    counts = counts.reshape(-1)
    padded = (counts + MOE_BLK - 1) // MOE_BLK * MOE_BLK
    pad_end = jnp.cumsum(padded)
    pad_start = (pad_end - padded).astype(I32)
    blk_row = jnp.arange(n_blocks, dtype=I32) * MOE_BLK
    block_e = jnp.minimum(jnp.sum(blk_row[:, None] >= pad_end[None, :], axis=1), N_EXPERTS - 1).astype(I32)
    own = block_e[:, None] == jnp.arange(N_EXPERTS, dtype=I32)[None, :]
    count_b = jnp.sum(jnp.where(own, counts[None, :], 0), axis=1)
    start_b = jnp.sum(jnp.where(own, pad_start[None, :], 0), axis=1)
    block_valid = jnp.clip(count_b - (blk_row - start_b), 0, MOE_BLK).astype(I32)
    n_used = jnp.maximum(pad_end[-1:] // MOE_BLK, 1).astype(I32)
    return pad_start, block_e, block_valid, n_used


def kernel(x, c, ctx, c_ctx, mod_w, mod_b, norm_g, w_in, w_out, conv_w, conv_b, lru_wr, lru_br, lru_wi, lru_bi,
           lru_lambda, gla_w2, gla_b2, gla_norm_g, qk_norm_g, win_sink, router_w, router_bias, exp_w_gate,
           exp_w_up, exp_w_down, sh_w_gate, sh_w_up, sh_w_down):
    bsz, n_lat, d = x.shape
    n_ctx = ctx.shape[1]
    depth = mod_w.shape[0]
    assert d == D_MODEL and bsz < MOD_ROWS
    lay = _Layout(bsz, n_lat, n_ctx)

    cc = jnp.zeros((MOD_ROWS, d), F32).at[:bsz].set(c).at[bsz].set(c_ctx)
    mods_all = _adaln(cc, mod_w, mod_b)
    cos, sin = _rope_tables(lay)
    xt = None

    splits = np.cumsum([0, W_A, W_B_RAW, W_C, W_D])
    for layer in range(depth):
        last = layer == depth - 1
        need_ctx = not last
        lat_only = last
        mods = mods_all[layer].reshape(MOD_ROWS * 6, 1, d)
        g = norm_g[layer]
        wl = w_in[layer]
        w_cat = jnp.concatenate(
            [wl[:, splits[0]:splits[1]], wl[:, splits[1]:splits[2]], jnp.zeros((d, W_B - W_B_RAW), F32),
             wl[:, splits[2]:splits[3]], wl[:, splits[3]:splits[4]]], axis=1).astype(BF16)
        if xt is None:
            p_a, p_b, p_c, p_d, xt = _inproj(lay, (x.reshape(bsz * n_lat, d), ctx.reshape(bsz * n_ctx, d)), mods,
                                             g[0:1], w_cat)
        else:
            p_a, p_b, p_c, p_d = _inproj(lay, (xt,), mods, g[0:1], w_cat)

        y_a = _rglru(lay, p_a, conv_w[layer], conv_b[layer], lru_wr[layer], lru_br[layer], lru_wi[layer],
                     lru_bi[layer], lru_lambda[layer])
        w2pad = jnp.zeros((2, LANES, GLA_HEADS * GLA_DK), F32)
        w2pad = w2pad.at[0, :GLA_RANK].set(gla_w2[layer, 0]).at[1, GLA_RANK:2 * GLA_RANK].set(gla_w2[layer, 1])
        y_b = _gla(lay, p_b, w2pad, gla_b2[layer], gla_norm_g[layer])
        y_c = _attention(lay, p_c, cos, sin, qk_norm_g[layer], window=False, need_ctx=need_ctx)
        y_d = _attention(lay, p_d, cos, sin, win_sink[layer], window=True, need_ctx=need_ctx)

        x_mid, h, hp = _outproj(lay, (y_a, y_b, y_c, y_d), w_out[layer].astype(BF16), xt, mods, g[1:2], g[2:3],
                                lat_only)
        top_e, gate, rank, counts = _router(lay, h, router_w[layer].T, router_bias[layer], lat_only)
        n_assign = lay.n_tiles(lat_only) * TM * TOP_K
        n_blocks = -(-(n_assign + N_EXPERTS * (MOE_BLK - 1)) // MOE_BLK)
        pad_start, block_e, block_valid, n_used = _moe_plan(counts, n_blocks)
        owner = top_e[:, :, None] == jnp.arange(N_EXPERTS, dtype=I32)
        dest = jnp.sum(jnp.where(owner, pad_start, 0), axis=-1) + rank
        dest_flat = (dest * PACKED_TILES).T.reshape(lay.rows // TM, 1, TM * TOP_K)
        xs = _dispatch(lay, dest_flat, hp, n_blocks * MOE_BLK, lat_only)
        ys = _experts(block_e, block_valid, n_used, xs, exp_w_gate, exp_w_up, exp_w_down, layer)
        gate_flat = gate.T.reshape(lay.rows // TM, 1, TM * TOP_K)
        xt = _combine(lay, dest_flat, gate_flat, hp, x_mid, mods, g[3:4], sh_w_gate[layer].astype(BF16),
                      sh_w_up[layer].astype(BF16), sh_w_down[layer].astype(BF16), ys, lat_only, compact_out=last)
    return xt.reshape(bsz, n_lat, d)
```

```python
import functools

import numpy as np
import jax
import jax.numpy as jnp
from jax import lax
from jax.experimental import pallas as pl
from jax.experimental.pallas import tpu as pltpu

F32 = jnp.float32
BF16 = jnp.bfloat16
I32 = jnp.int32
U32 = jnp.uint32
HIGHEST = lax.Precision.HIGHEST
NT_DIMS = (((1,), (1,)), ((), ()))
TN_DIMS = (((0,), (0,)), ((), ()))

LANES = 128
SUBLANES = 8
VMEM_LIMIT_BYTES = 56 * 1024 * 1024

D_MODEL = 2048
GRID_W = 64
HEAD_DIM = 128
GROUP_WIDTH = D_MODEL // 4
LRU_BLOCKS = 4
CONV_W = 4
RGLRU_C = 8.0
GLA_HEADS = 4
GLA_DK = 64
GLA_DV = 128
GLA_RANK = 16
GLA_TAU = 16.0
GLA_CHUNK = 64
GLA_HEADS_PER_STEP = 4
KV_HEADS = 2
WINDOW = 128
N_EXPERTS = 64
N_GROUPS = 8
TOPK_GROUPS = 4
TOP_K = 8
EXPERT_HIDDEN = 512
ROUTED_SCALE = 2.5
RMS_EPS = 1e-6
ROPE_THETA = 10000.0
NEG_INF = -1e30

W_A = 2 * GROUP_WIDTH
W_B_RAW = 2 * GLA_HEADS * GLA_DK + 2 * GLA_HEADS * GLA_DV + 2 * GLA_RANK
W_B = 1664
W_C = GROUP_WIDTH + 2 * KV_HEADS * HEAD_DIM
W_D = W_C
Z_COL_BLOCK = (2 * GLA_HEADS * GLA_DK + 2 * GLA_HEADS * GLA_DV) // LANES

TM = 256
MOE_BLK = 512
PACKED_TILES = D_MODEL // 2 // LANES
MOD_ROWS = 16


def _params(sem, vmem=VMEM_LIMIT_BYTES):
    return pltpu.CompilerParams(dimension_semantics=sem, vmem_limit_bytes=vmem)


def _rms(x, g):
    return x * lax.rsqrt(jnp.mean(x * x, axis=-1, keepdims=True) + RMS_EPS) * g


def _silu(x):
    return x * jax.nn.sigmoid(x)


def _pack_bf16_pairs(h):
    n = h.shape[1] // 2
    hb = h.astype(BF16).astype(F32)
    hi = lax.bitcast_convert_type(hb[:, :n], U32)
    lo = lax.bitcast_convert_type(hb[:, n:], U32)
    return hi | (lo >> 16)


def _to_row_tiles(ref, x):
    m = x.shape[0]
    c = x.shape[1] // LANES
    for j in range(c):
        ref[pl.ds(j, m, stride=c), :] = x[:, j * LANES:(j + 1) * LANES]


def _from_row_tiles(ref, c, lead=()):
    m = ref.shape[-2] // c
    return jnp.concatenate([ref[lead + (pl.ds(j, m, stride=c), slice(None))] for j in range(c)], axis=1)


def _unpack_bf16_pairs(p):
    hi = lax.bitcast_convert_type(p & jnp.uint32(0xFFFF0000), F32)
    lo = lax.bitcast_convert_type(p << 16, F32)
    return jnp.concatenate([hi, lo], axis=1).astype(BF16)


def _adaln_kernel(c_ref, w_ref, b_ref, o_ref):
    a = _silu(c_ref[...])
    o_ref[0] = jnp.dot(a, w_ref[0], preferred_element_type=F32, precision=HIGHEST) + b_ref[0]


def _adaln(cc, mod_w, mod_b):
    depth, d, n = mod_w.shape
    tn = 1024
    return pl.pallas_call(
        _adaln_kernel,
        grid=(depth, n // tn),
        in_specs=[pl.BlockSpec((MOD_ROWS, d), lambda l, j: (0, 0)),
                  pl.BlockSpec((1, d, tn), lambda l, j: (l, 0, j)),
                  pl.BlockSpec((1, 1, tn), lambda l, j: (l, 0, j))],
        out_specs=pl.BlockSpec((1, MOD_ROWS, tn), lambda l, j: (l, 0, j)),
        out_shape=jax.ShapeDtypeStruct((depth, MOD_ROWS, n), F32),
        compiler_params=_params(("arbitrary", "arbitrary")),
        name="adaln",
    )(cc, mod_w, mod_b.reshape(depth, 1, n))


class _Layout:
    def __init__(self, bsz, n_lat, n_ctx):
        self.bsz, self.n_lat, self.n_ctx = bsz, n_lat, n_ctx
        self.seq = n_ctx + n_lat
        self.rows = bsz * self.seq
        assert n_ctx % TM == 0 and n_lat % TM == 0
        self.tiles_b = self.seq // TM
        self.ctx_tiles = n_ctx // TM
        self.lat_tiles = n_lat // TM

    def n_tiles(self, lat_only):
        return self.bsz * (self.lat_tiles if lat_only else self.tiles_b)

    def tile(self, i, lat_only):
        if lat_only:
            return (i // self.lat_tiles) * self.tiles_b + self.ctx_tiles + i % self.lat_tiles
        return i

    def mod_row(self, i, lat_only):
        if lat_only:
            return i // self.lat_tiles
        return jnp.where(i % self.tiles_b < self.ctx_tiles, self.bsz, i // self.tiles_b)


def _mod_spec(lay, chunk, lat_only):
    return pl.BlockSpec((1, 1, D_MODEL), lambda i: (lay.mod_row(i, lat_only) * 6 + chunk, 0, 0))


def _inproj_kernel(*refs, tiles_b, ctx_tiles):
    if ctx_tiles:
        x_ref, c_ref, sh_ref, sc_ref, g_ref, w_ref, oa_ref, ob_ref, oc_ref, od_ref, xt_ref = refs
        x = jnp.where(pl.program_id(0) % tiles_b < ctx_tiles, c_ref[...], x_ref[...])
        xt_ref[...] = x
    else:
        x_ref, sh_ref, sc_ref, g_ref, w_ref, oa_ref, ob_ref, oc_ref, od_ref = refs
        x = x_ref[...]
    h = _rms(x, g_ref[...]) * (1.0 + sc_ref[0]) + sh_ref[0]
    hb = h.astype(BF16)
    c0 = 0
    for o_ref in (oa_ref, ob_ref, oc_ref, od_ref):
        width = o_ref.shape[1]
        o_ref[...] = jnp.dot(hb, w_ref[:, c0:c0 + width], preferred_element_type=F32)
        c0 += width


def _inproj(lay, xs, mods, g0, w_cat):
    widths = (W_A, W_B, W_C, W_D)
    row = lambda i: (i, 0)
    split = len(xs) == 2
    x_specs = [pl.BlockSpec((TM, D_MODEL), row)]
    out_specs = [pl.BlockSpec((TM, w), row) for w in widths]
    out_shape = [jax.ShapeDtypeStruct((lay.rows, w), F32) for w in widths]
    if split:
        lat = lambda i: ((i // lay.tiles_b) * lay.lat_tiles + jnp.maximum(i % lay.tiles_b - lay.ctx_tiles, 0), 0)
        ctx = lambda i: ((i // lay.tiles_b) * lay.ctx_tiles + jnp.minimum(i % lay.tiles_b, lay.ctx_tiles - 1), 0)
        x_specs = [pl.BlockSpec((TM, D_MODEL), lat), pl.BlockSpec((TM, D_MODEL), ctx)]
        out_specs.append(pl.BlockSpec((TM, D_MODEL), row))
        out_shape.append(jax.ShapeDtypeStruct((lay.rows, D_MODEL), F32))
    return pl.pallas_call(
        functools.partial(_inproj_kernel, tiles_b=lay.tiles_b, ctx_tiles=lay.ctx_tiles if split else 0),
        grid=(lay.n_tiles(False),),
        in_specs=x_specs + [_mod_spec(lay, 0, False), _mod_spec(lay, 1, False),
                            pl.BlockSpec((1, D_MODEL), lambda i: (0, 0)),
                            pl.BlockSpec((D_MODEL, sum(widths)), lambda i: (0, 0), pipeline_mode=pl.Buffered(1))],
        out_specs=out_specs,
        out_shape=out_shape,
        compiler_params=_params(("arbitrary",)),
        name="inproj",
    )(*xs, mods, mods, g0, w_cat)


def _rglru_kernel(x_ref, g_ref, cw_ref, cb_ref, wr_ref, br_ref, wi_ref, bi_ref, lam_ref, o_ref,
                  xs_ref, a_ref, b_ref, h_ref, *, n_ctx):
    seq = x_ref.shape[0]
    pad = SUBLANES
    n_tiles = seq // SUBLANES
    ctx_tiles = n_ctx // SUBLANES
    row = lax.broadcasted_iota(I32, (seq, LANES), 0)

    xs_ref[0:pad, :] = jnp.zeros((pad, LANES), F32)
    xs_ref[pad + seq:2 * pad + seq, :] = jnp.zeros((pad, LANES), F32)
    xs_ref[pad:pad + seq, :] = x_ref[...]
    u = jnp.zeros((seq, LANES), F32) + cb_ref[...]
    for j in range(CONV_W):
        off = j - 2
        xsh = xs_ref[pad + off:pad + off + seq, :]
        if off < 0:
            xsh = jnp.where((row >= n_ctx) & (row + off < n_ctx), 0.0, xsh)
        elif off > 0:
            xsh = jnp.where((row < n_ctx) & (row + off >= n_ctx), 0.0, xsh)
        u = u + cw_ref[j:j + 1, :] * xsh

    ub = u.astype(BF16)
    tile_shape = (n_tiles, SUBLANES, LANES)
    rmod = lax.broadcasted_iota(I32, tile_shape, 1)
    for d in range(2):
        reverse = d == 1
        r = jax.nn.sigmoid(jnp.dot(ub, wr_ref[d, 0].astype(BF16), preferred_element_type=F32) + br_ref[d:d + 1, :])
        gi = jax.nn.sigmoid(jnp.dot(ub, wi_ref[d, 0].astype(BF16), preferred_element_type=F32) + bi_ref[d:d + 1, :])
        lam = lam_ref[d:d + 1, :]
        softplus_neg = jnp.maximum(-lam, 0.0) + jnp.log(1.0 + jnp.exp(-jnp.abs(lam)))
        a = jnp.exp(-RGLRU_C * r * softplus_neg)
        b = jnp.sqrt(1.0 - a * a) * (gi * u)
        a = a.reshape(tile_shape)
        b = b.reshape(tile_shape)
        for s in (1, 2, 4):
            if reverse:
                ap, bp, ok = pltpu.roll(a, SUBLANES - s, 1), pltpu.roll(b, SUBLANES - s, 1), rmod < SUBLANES - s
            else:
                ap, bp, ok = pltpu.roll(a, s, 1), pltpu.roll(b, s, 1), rmod >= s
            b = jnp.where(ok, a * bp + b, b)
            a = jnp.where(ok, a * ap, a)
        a_ref[...] = a.reshape(seq, LANES)
        b_ref[...] = b.reshape(seq, LANES)

        def step(t, h, reverse=reverse):
            r0 = pl.multiple_of(t * SUBLANES, SUBLANES)
            h8 = a_ref[pl.ds(r0, SUBLANES), :] * h + b_ref[pl.ds(r0, SUBLANES), :]
            if reverse:
                h_ref[pl.ds(r0, SUBLANES), :] = h_ref[pl.ds(r0, SUBLANES), :] + h8
                return jnp.broadcast_to(h8[0:1, :], (SUBLANES, LANES))
            h_ref[pl.ds(r0, SUBLANES), :] = h8
            return jnp.broadcast_to(h8[SUBLANES - 1:SUBLANES, :], (SUBLANES, LANES))

        h0 = jnp.zeros((SUBLANES, LANES), F32)
        if reverse:
            hc = lax.fori_loop(0, ctx_tiles, lambda i, h: step(ctx_tiles - 1 - i, h), h0)
            lax.fori_loop(0, n_tiles - ctx_tiles, lambda i, h: step(n_tiles - 1 - i, h), hc)
        else:
            lax.fori_loop(0, n_tiles, step, h0)

    g = g_ref[...]
    gelu = 0.5 * g * (1.0 + jnp.tanh(np.sqrt(2.0 / np.pi) * (g + 0.044715 * (g * g * g))))
    o_ref[...] = h_ref[...] * gelu


def _rglru(lay, p_a, conv_w, conv_b, wr, br, wi, bi, lam):
    seq = lay.seq
    nb = LRU_BLOCKS
    col = lambda off: (lambda b, j: (b, off + j))
    vec = lambda rows: pl.BlockSpec((rows, LANES), lambda b, j: (0, j))
    wspec = pl.BlockSpec((2, 1, LANES, LANES), lambda b, j: (0, j, 0, 0))
    return pl.pallas_call(
        functools.partial(_rglru_kernel, n_ctx=lay.n_ctx),
        grid=(lay.bsz, nb),
        in_specs=[pl.BlockSpec((seq, LANES), col(0)), pl.BlockSpec((seq, LANES), col(nb)),
                  vec(CONV_W), vec(1), wspec, vec(2), wspec, vec(2), vec(2)],
        out_specs=pl.BlockSpec((seq, LANES), col(0)),
        out_shape=jax.ShapeDtypeStruct((lay.rows, GROUP_WIDTH), F32),
        scratch_shapes=[pltpu.VMEM((seq + 2 * SUBLANES, LANES), F32), pltpu.VMEM((seq, LANES), F32),
                        pltpu.VMEM((seq, LANES), F32), pltpu.VMEM((seq, LANES), F32)],
        compiler_params=_params(("arbitrary", "arbitrary")),
        name="rglru",
    )(p_a, p_a, conv_w, conv_b.reshape(1, -1), wr, br, wi, bi, lam)


def _gla_kernel(q_ref, k_ref, v_ref, g_ref, z_ref, w2_ref, b2_ref, ng_ref, o_ref,
                laf_ref, lab_ref, of_ref, ob_ref, *, n_ctx):
    seq = q_ref.shape[0]
    nh = q_ref.shape[1] // GLA_DK
    qk_w = nh * GLA_DK
    cs = GLA_CHUNK
    n_chunks = seq // cs
    ctx_chunks = n_ctx // cs
    z = z_ref[...]
    for d, la_ref in ((0, laf_ref), (1, lab_ref)):
        pre = jnp.dot(z, w2_ref[d], preferred_element_type=F32, precision=HIGHEST) + b2_ref[d:d + 1, :]
        log_sig = jnp.minimum(pre, 0.0) - jnp.log(1.0 + jnp.exp(-jnp.abs(pre)))
        la_ref[...] = log_sig * (1.0 / GLA_TAU)

    ri = lax.broadcasted_iota(I32, (cs, cs), 0)
    ci = lax.broadcasted_iota(I32, (cs, cs), 1)
    scale = GLA_DK ** -0.5

    def chunk(c, states, reverse):
        r0 = pl.multiple_of(c * cs, cs)
        rows = pl.ds(r0, cs)
        keep = (ri <= ci) if reverse else (ri >= ci)
        la = (lab_ref if reverse else laf_ref)[rows, :]
        la_hi = la.astype(BF16)
        rest = la - la_hi.astype(F32)
        la_mid = rest.astype(BF16)
        la_lo = (rest - la_mid.astype(F32)).astype(BF16)
        parts = jnp.dot(keep.astype(BF16), jnp.concatenate([la_hi, la_mid, la_lo], axis=1),
                        preferred_element_type=F32)
        cum = parts[:, 0:qk_w] + parts[:, qk_w:2 * qk_w] + parts[:, 2 * qk_w:3 * qk_w]
        last = cum[0:1, :] if reverse else cum[cs - 1:cs, :]
        q = q_ref[rows, :] * scale
        k = k_ref[rows, :]
        qe = q * jnp.exp(cum)
        ke = k * jnp.exp(-cum)
        kl = k * jnp.exp(last - cum)
        el = jnp.exp(last)
        out_ref = ob_ref if reverse else of_ref
        new_states = []
        for hh in range(nh):
            dk = slice(hh * GLA_DK, (hh + 1) * GLA_DK)
            dv = slice(hh * GLA_DV, (hh + 1) * GLA_DV)
            qh = qe[:, dk].astype(BF16)
            vh = v_ref[rows, dv].astype(BF16)
            st = states[hh]
            s = lax.dot_general(qh, ke[:, dk].astype(BF16), NT_DIMS, preferred_element_type=F32)
            s = jnp.where(keep, s, 0.0)
            o = (jnp.dot(s.astype(BF16), vh, preferred_element_type=F32)
                 + lax.dot_general(qh, st.astype(BF16), NT_DIMS, preferred_element_type=F32))
            out_ref[rows, dv] = o
            new_states.append(st * el[:, dk]
                              + lax.dot_general(vh, kl[:, dk].astype(BF16), TN_DIMS, preferred_element_type=F32))
        return tuple(new_states)

    def body(i, carry):
        sf, sb = carry
        sf = chunk(i, sf, False)
        cb = jnp.where(i < ctx_chunks, ctx_chunks - 1 - i, n_chunks - 1 + ctx_chunks - i)
        sb = chunk(cb, sb, True)
        return sf, sb

    zero = (jnp.zeros((GLA_DV, GLA_DK), F32),) * nh
    lax.fori_loop(0, n_chunks, body, (zero, zero))

    ng = ng_ref[...]
    for hh in range(nh):
        dv = slice(hh * GLA_DV, (hh + 1) * GLA_DV)
        o = of_ref[:, dv] + ob_ref[:, dv]
        o_ref[:, dv] = _rms(o, ng) * _silu(g_ref[:, dv])


def _gla(lay, p_b, w2pad, b2, norm_g):
    seq = lay.seq
    hp = GLA_HEADS // GLA_HEADS_PER_STEP
    qk_w = GLA_HEADS_PER_STEP * GLA_DK
    v_w = GLA_HEADS_PER_STEP * GLA_DV
    k_off = GLA_HEADS * GLA_DK // qk_w
    v_off = 2 * GLA_HEADS * GLA_DK // v_w
    g_off = v_off + GLA_HEADS * GLA_DV // v_w
    once = pl.Buffered(1)
    blk = lambda w, off, mode=None: pl.BlockSpec((seq, w), lambda b, j: (b, off + j), pipeline_mode=mode)
    return pl.pallas_call(
        functools.partial(_gla_kernel, n_ctx=lay.n_ctx),
        grid=(lay.bsz, hp),
        in_specs=[blk(qk_w, 0, once), blk(qk_w, k_off, once), blk(v_w, v_off, once), blk(v_w, g_off, once),
                  pl.BlockSpec((seq, LANES), lambda b, j: (b, Z_COL_BLOCK), pipeline_mode=once),
                  pl.BlockSpec((2, LANES, qk_w), lambda b, j: (0, 0, j)),
                  pl.BlockSpec((2, qk_w), lambda b, j: (0, j)),
                  pl.BlockSpec((1, GLA_DV), lambda b, j: (0, 0))],
        out_specs=blk(v_w, 0),
        out_shape=jax.ShapeDtypeStruct((lay.rows, GROUP_WIDTH), F32),
        scratch_shapes=[pltpu.VMEM((seq, qk_w), F32), pltpu.VMEM((seq, qk_w), F32),
                        pltpu.VMEM((seq, v_w), F32), pltpu.VMEM((seq, v_w), F32)],
        compiler_params=_params(("arbitrary", "arbitrary")),
        name="gla",
    )(p_b, p_b, p_b, p_b, p_b, w2pad, b2, norm_g.reshape(1, -1))


def _rope(x, cos, sin_signed):
    lane = lax.broadcasted_iota(I32, x.shape, 1)
    partner = jnp.where((lane & 32) != 0, pltpu.roll(x, 32, 1), pltpu.roll(x, HEAD_DIM - 32, 1))
    return x * cos + partner * sin_signed


def _attn_kernel(*refs, window, need_ctx, tq, n_ctx):
    if window:
        sink_ref, q_ref, k_ref, v_ref, cos_ref, sin_ref, o_ref, ks_ref, vs_ref = refs
        qkg_ref = None
    else:
        q_ref, k_ref, v_ref, cos_ref, sin_ref, qkg_ref, o_ref, ks_ref, vs_ref = refs
    seq = k_ref.shape[0]
    n_lat = seq - n_ctx
    g = pl.program_id(1)
    qi = pl.program_id(2)
    jq = qi if need_ctx else qi + n_ctx // tq

    @pl.when(qi == 0)
    def _():
        k = k_ref[...]
        if not window:
            k = _rms(k, qkg_ref[1:2, :])
        ks_ref[...] = _rope(k, cos_ref[...], sin_ref[...]).astype(BF16)
        vs_ref[...] = v_ref[...].astype(BF16)

    r0 = pl.multiple_of(jq * tq, tq)
    cos = cos_ref[pl.ds(r0, tq), :]
    sin = sin_ref[pl.ds(r0, tq), :]
    heads = []
    for hh in range(2):
        qh = q_ref[:, hh * HEAD_DIM:(hh + 1) * HEAD_DIM]
        if not window:
            qh = _rms(qh, qkg_ref[0:1, :])
        heads.append((_rope(qh, cos, sin) * HEAD_DIM ** -0.5).astype(BF16))
    q2 = jnp.concatenate(heads, axis=0)

    def finish(o):
        o_ref[:, 0:HEAD_DIM] = o[0:tq]
        o_ref[:, HEAD_DIM:2 * HEAD_DIM] = o[tq:2 * tq]

    def scores(lo, n):
        return lax.dot_general(q2, ks_ref[pl.ds(lo, n), :], NT_DIMS, preferred_element_type=F32)

    def pv(p, lo, n):
        return jnp.dot(p.astype(BF16), vs_ref[pl.ds(lo, n), :], preferred_element_type=F32)

    if window:
        rowi = lax.broadcasted_iota(I32, (2 * tq, 1), 0)
        sink = jnp.where(rowi < tq, sink_ref[2 * g], sink_ref[2 * g + 1])

    def ctx_tile():
        s = scores(0, n_ctx)
        m = jnp.max(s, axis=-1, keepdims=True)
        if window:
            m = jnp.maximum(m, sink)
        p = jnp.exp(s - m)
        l = jnp.sum(p, axis=-1, keepdims=True)
        if window:
            l = l + jnp.exp(sink - m)
        finish(pv(p, 0, n_ctx) / l)

    def lat_tile():
        if not window:
            s = scores(0, seq)
            m = jnp.max(s, axis=-1, keepdims=True)
            p = jnp.exp(s - m)
            finish(pv(p, 0, seq) / jnp.sum(p, axis=-1, keepdims=True))
            return
        band = tq + 2 * WINDOW
        jl = jq - n_ctx // tq
        start = jnp.clip(jl * tq - WINDOW, 0, n_lat - band)
        lo = pl.multiple_of(n_ctx + start, WINDOW)
        sb = scores(lo, band)
        q_pos = jl * tq + lax.broadcasted_iota(I32, (2 * tq, band), 0) % tq
        k_pos = start + lax.broadcasted_iota(I32, (2 * tq, band), 1)
        sb = jnp.where(jnp.abs(q_pos - k_pos) <= WINDOW, sb, NEG_INF)
        sc = scores(0, n_ctx)
        m = jnp.maximum(jnp.maximum(jnp.max(sb, axis=-1, keepdims=True), jnp.max(sc, axis=-1, keepdims=True)), sink)
        pb = jnp.exp(sb - m)
        pc = jnp.exp(sc - m)
        l = jnp.sum(pb, axis=-1, keepdims=True) + jnp.sum(pc, axis=-1, keepdims=True) + jnp.exp(sink - m)
        finish((pv(pb, lo, band) + pv(pc, 0, n_ctx)) / l)

    if need_ctx:
        pl.when(jq * tq < n_ctx)(ctx_tile)
        pl.when(jq * tq >= n_ctx)(lat_tile)
    else:
        lat_tile()


def _attention(lay, p, cos, sin, extra, *, window, need_ctx):
    seq = lay.seq
    tq = TM
    q_tiles_b = seq // tq
    nq = q_tiles_b if need_ctx else lay.n_lat // tq
    skip = 0 if need_ctx else lay.n_ctx // tq
    q_heads_w = 2 * HEAD_DIM
    k_off = GROUP_WIDTH // HEAD_DIM
    v_off = k_off + KV_HEADS

    def im(f):
        return (lambda b, g, qi, *_: f(b, g, qi))

    in_specs = [pl.BlockSpec((tq, q_heads_w), im(lambda b, g, qi: (b * q_tiles_b + skip + qi, g))),
                pl.BlockSpec((seq, HEAD_DIM), im(lambda b, g, qi: (b, k_off + g))),
                pl.BlockSpec((seq, HEAD_DIM), im(lambda b, g, qi: (b, v_off + g))),
                pl.BlockSpec((seq, HEAD_DIM), im(lambda b, g, qi: (0, 0))),
                pl.BlockSpec((seq, HEAD_DIM), im(lambda b, g, qi: (0, 0)))]
    args = [p, p, p, cos, sin]
    if not window:
        in_specs.append(pl.BlockSpec((2, HEAD_DIM), im(lambda b, g, qi: (0, 0))))
        args.append(extra)
    out_spec = pl.BlockSpec((tq, q_heads_w), im(lambda b, g, qi: (b * q_tiles_b + skip + qi, g)))
    kern = functools.partial(_attn_kernel, window=window, need_ctx=need_ctx, tq=tq, n_ctx=lay.n_ctx)
    scratch = [pltpu.VMEM((seq, HEAD_DIM), BF16), pltpu.VMEM((seq, HEAD_DIM), BF16)]
    out_shape = jax.ShapeDtypeStruct((lay.rows, GROUP_WIDTH), F32)
    sem = ("arbitrary", "arbitrary", "arbitrary")
    grid = (lay.bsz, KV_HEADS, nq)
    if window:
        return pl.pallas_call(
            kern,
            grid_spec=pltpu.PrefetchScalarGridSpec(num_scalar_prefetch=1, grid=grid, in_specs=in_specs,
                                                   out_specs=out_spec, scratch_shapes=scratch),
            out_shape=out_shape, compiler_params=_params(sem), name="win_attn",
        )(extra, *args)
    return pl.pallas_call(kern, grid=grid, in_specs=in_specs, out_specs=out_spec, scratch_shapes=scratch,
                          out_shape=out_shape, compiler_params=_params(sem), name="glob_attn")(*args)


def _outproj_kernel(ya_ref, yb_ref, yc_ref, yd_ref, w_ref, x_ref, gm_ref, sh_ref, sc_ref, g1_ref, g2_ref,
                    xo_ref, h_ref, hp_ref):
    acc = None
    for i, y_ref in enumerate((ya_ref, yb_ref, yc_ref, yd_ref)):
        part = jnp.dot(y_ref[...].astype(BF16), w_ref[i * GROUP_WIDTH:(i + 1) * GROUP_WIDTH, :],
                       preferred_element_type=F32)
        acc = part if acc is None else acc + part
    x_new = x_ref[...] + gm_ref[0] * _rms(acc, g1_ref[...])
    xo_ref[...] = x_new
    h = _rms(x_new, g2_ref[...]) * (1.0 + sc_ref[0]) + sh_ref[0]
    h_ref[...] = h
    _to_row_tiles(hp_ref, _pack_bf16_pairs(h))


def _outproj(lay, ys, w_out, xt, mods, g1, g2, lat_only):
    row = lambda i: (lay.tile(i, lat_only), 0)
    full = lambda shape: pl.BlockSpec(shape, lambda i: (0, 0))
    yspec = pl.BlockSpec((TM, GROUP_WIDTH), row)
    xspec = pl.BlockSpec((TM, D_MODEL), row)
    return pl.pallas_call(
        _outproj_kernel,
        grid=(lay.n_tiles(lat_only),),
        in_specs=[yspec, yspec, yspec, yspec, full((D_MODEL, D_MODEL)), xspec,
                  _mod_spec(lay, 2, lat_only), _mod_spec(lay, 3, lat_only), _mod_spec(lay, 4, lat_only),
                  full((1, D_MODEL)), full((1, D_MODEL))],
        out_specs=[xspec, xspec, pl.BlockSpec((TM * PACKED_TILES, LANES), row)],
        out_shape=[jax.ShapeDtypeStruct((lay.rows, D_MODEL), F32), jax.ShapeDtypeStruct((lay.rows, D_MODEL), F32),
                   jax.ShapeDtypeStruct((lay.rows * PACKED_TILES, LANES), U32)],
        compiler_params=_params(("arbitrary",)),
        name="outproj",
    )(*ys, w_out, xt, mods, mods, mods, g1, g2)


def _route(h, wt_ref, bias_ref, e_ref, gate_ref, rank_ref, cnt_ref, carry_ref):
    i = pl.program_id(0)
    tm = h.shape[0]
    ne = N_EXPERTS
    gsz = ne // N_GROUPS

    @pl.when(i == 0)
    def _():
        carry_ref[...] = jnp.zeros_like(carry_ref)

    logits = lax.dot_general(wt_ref[...], h, NT_DIMS, preferred_element_type=F32, precision=HIGHEST)
    scores = jax.nn.sigmoid(logits)
    biased = scores + bias_ref[...]

    sub = lax.broadcasted_iota(I32, (gsz, tm), 0)
    gs_rows = []
    for grp in range(N_GROUPS):
        blk = biased[grp * gsz:(grp + 1) * gsz, :]
        m1 = jnp.max(blk, axis=0, keepdims=True)
        first = jnp.min(jnp.where(blk == m1, sub, gsz), axis=0, keepdims=True)
        m2 = jnp.max(jnp.where(sub == first, -jnp.inf, blk), axis=0, keepdims=True)
        gs_rows.append(m1 + m2)
    gs = jnp.concatenate(gs_rows, axis=0)

    gidx = lax.broadcasted_iota(I32, (N_GROUPS, tm), 0)
    chosen = jnp.zeros((N_GROUPS, tm), jnp.bool_)
    cur = gs
    for _ in range(TOPK_GROUPS):
        m = jnp.max(cur, axis=0, keepdims=True)
        first = jnp.min(jnp.where(cur == m, gidx, N_GROUPS), axis=0, keepdims=True)
        sel = gidx == first
        chosen = chosen | sel
        cur = jnp.where(sel, -jnp.inf, cur)

    eidx = lax.broadcasted_iota(I32, (ne, tm), 0)
    eligible = jnp.concatenate(
        [jnp.broadcast_to(chosen[grp:grp + 1, :], (gsz, tm)) for grp in range(N_GROUPS)], axis=0)
    cur = jnp.where(eligible, biased, -jnp.inf)
    sels, gates = [], []
    onehot = jnp.zeros((ne, tm), jnp.bool_)
    for k in range(TOP_K):
        m = jnp.max(cur, axis=0, keepdims=True)
        first = jnp.min(jnp.where(cur == m, eidx, ne), axis=0, keepdims=True)
        sel = eidx == first
        e_ref[k:k + 1, :] = first
        gates.append(jnp.sum(jnp.where(sel, scores, 0.0), axis=0, keepdims=True))
        sels.append(sel)
        onehot = onehot | sel
        cur = jnp.where(sel, -jnp.inf, cur)
    total = gates[0]
    for gk in gates[1:]:
        total = total + gk
    for k in range(TOP_K):
        gate_ref[k:k + 1, :] = gates[k] / total * ROUTED_SCALE

    tr = lax.broadcasted_iota(I32, (tm, tm), 0)
    tc = lax.broadcasted_iota(I32, (tm, tm), 1)
    before = (tr < tc).astype(BF16)
    oh = onehot.astype(F32)
    prefix = jnp.dot(oh.astype(BF16), before, preferred_element_type=F32) + carry_ref[...]
    for k in range(TOP_K):
        rank_ref[k:k + 1, :] = jnp.sum(jnp.where(sels[k], prefix, 0.0), axis=0, keepdims=True).astype(I32)
    carry = carry_ref[...] + jnp.sum(oh, axis=1, keepdims=True)
    carry_ref[...] = carry
    cnt_ref[...] = carry.astype(I32)


def _router_kernel(h_ref, wt_ref, bias_ref, e_ref, gate_ref, rank_ref, cnt_ref, carry_ref):
    _route(h_ref[...], wt_ref, bias_ref, e_ref, gate_ref, rank_ref, cnt_ref, carry_ref)


def _router(lay, h, router_wt, bias, lat_only):
    col = lambda i: (0, lay.tile(i, lat_only))
    kt = pl.BlockSpec((TOP_K, TM), col)
    return pl.pallas_call(
        _router_kernel,
        grid=(lay.n_tiles(lat_only),),
        in_specs=[pl.BlockSpec((TM, D_MODEL), lambda i: (lay.tile(i, lat_only), 0)),
                  pl.BlockSpec((N_EXPERTS, D_MODEL), lambda i: (0, 0)),
                  pl.BlockSpec((N_EXPERTS, 1), lambda i: (0, 0))],
        out_specs=[kt, kt, kt, pl.BlockSpec((N_EXPERTS, 1), lambda i: (0, 0))],
        out_shape=[jax.ShapeDtypeStruct((TOP_K, lay.rows), I32), jax.ShapeDtypeStruct((TOP_K, lay.rows), F32),
                   jax.ShapeDtypeStruct((TOP_K, lay.rows), I32), jax.ShapeDtypeStruct((N_EXPERTS, 1), I32)],
        scratch_shapes=[pltpu.VMEM((N_EXPERTS, 1), F32)],
        compiler_params=_params(("arbitrary",)),
        name="router",
    )(h, router_wt, bias.reshape(-1, 1))


def _dispatch_kernel(dest_ref, hp_ref, xs_ref, sem):
    c = PACKED_TILES
    tm = hp_ref.shape[0] // c

    def issue(t, carry):
        src = hp_ref.at[pl.ds(pl.multiple_of(t * c, c), c)]
        for k in range(TOP_K):
            dest = pl.multiple_of(dest_ref[0, 0, t * TOP_K + k], c)
            pltpu.make_async_copy(src, xs_ref.at[pl.ds(dest, c)], sem).start(priority=k % 2)
        return carry

    lax.fori_loop(0, tm, issue, 0, unroll=2)
    for k in range(TOP_K):
        pltpu.make_async_copy(hp_ref, xs_ref.at[pl.ds(0, tm * c)], sem).wait()


def _dispatch(lay, dest_flat, hp, n_rows, lat_only):
    return pl.pallas_call(
        _dispatch_kernel,
        grid=(lay.n_tiles(lat_only),),
        in_specs=[pl.BlockSpec((1, 1, TM * TOP_K), lambda i: (lay.tile(i, lat_only), 0, 0), memory_space=pltpu.SMEM),
                  pl.BlockSpec((TM * PACKED_TILES, LANES), lambda i: (lay.tile(i, lat_only), 0))],
        out_specs=pl.BlockSpec(memory_space=pl.ANY),
        scratch_shapes=[pltpu.SemaphoreType.DMA],
        out_shape=jax.ShapeDtypeStruct((n_rows * PACKED_TILES, LANES), U32),
        compiler_params=_params(("arbitrary",)),
        name="dispatch",
    )(dest_flat, hp)


def _experts_kernel(be_ref, bv_ref, nu_ref, xs_ref, wg_ref, wu_ref, wd_ref, ys_ref, wgb_ref, wub_ref, wdb_ref):
    j = pl.program_id(0)
    live = j < nu_ref[0]
    prev = be_ref[jnp.maximum(j - 1, 0)]

    @pl.when(live & ((j == 0) | (be_ref[j] != prev)))
    def _():
        wgb_ref[...] = wg_ref[0, 0].astype(BF16)
        wub_ref[...] = wu_ref[0, 0].astype(BF16)
        wdb_ref[...] = wd_ref[0, 0].astype(BF16)

    @pl.when(live)
    def _():
        packed = _from_row_tiles(xs_ref, PACKED_TILES)
        rowi = lax.broadcasted_iota(I32, packed.shape, 0)
        packed = jnp.where(rowi < bv_ref[j], packed, jnp.uint32(0))
        x = _unpack_bf16_pairs(packed)
        hid = (_silu(jnp.dot(x, wgb_ref[...], preferred_element_type=F32))
               * jnp.dot(x, wub_ref[...], preferred_element_type=F32))
        y = jnp.dot(hid.astype(BF16), wdb_ref[...], preferred_element_type=F32)
        _to_row_tiles(ys_ref, _pack_bf16_pairs(y))


def _experts(block_e, block_valid, n_used, xs, w_gate, w_up, w_down, layer):
    n_blocks = xs.shape[0] // (MOE_BLK * PACKED_TILES)
    last = lambda j, nu: jnp.minimum(j, nu[0] - 1)
    wmap = lambda j, be, bv, nu: (layer, be[last(j, nu)], 0, 0)
    return pl.pallas_call(
        _experts_kernel,
        grid_spec=pltpu.PrefetchScalarGridSpec(
            num_scalar_prefetch=3, grid=(n_blocks,),
            in_specs=[pl.BlockSpec((MOE_BLK * PACKED_TILES, LANES), lambda j, be, bv, nu: (last(j, nu), 0)),
                      pl.BlockSpec((1, 1, D_MODEL, EXPERT_HIDDEN), wmap),
                      pl.BlockSpec((1, 1, D_MODEL, EXPERT_HIDDEN), wmap),
                      pl.BlockSpec((1, 1, EXPERT_HIDDEN, D_MODEL), wmap)],
            out_specs=pl.BlockSpec((MOE_BLK * PACKED_TILES, LANES), lambda j, be, bv, nu: (last(j, nu), 0)),
            scratch_shapes=[pltpu.VMEM((D_MODEL, EXPERT_HIDDEN), BF16), pltpu.VMEM((D_MODEL, EXPERT_HIDDEN), BF16),
                            pltpu.VMEM((EXPERT_HIDDEN, D_MODEL), BF16)]),
        out_shape=jax.ShapeDtypeStruct((n_blocks * MOE_BLK * PACKED_TILES, LANES), U32),
        compiler_params=_params(("arbitrary",)),
        name="experts",
    )(block_e, block_valid, n_used, xs, w_gate, w_up, w_down)


def _combine_kernel(dest_ref, next_ref, gate_ref, hp_ref, x_ref, gf_ref, g3_ref, sg_ref, su_ref, sd_ref,
                    ys_ref, o_ref, buf_ref, acc_ref, sem):
    c = PACKED_TILES
    tm = hp_ref.shape[0] // c
    i = pl.program_id(0)
    cur = i % 2

    parts = 4
    has_next = i + 1 < pl.num_programs(0)

    def gather(idx_ref, half, part):
        def issue(t, carry):
            slab = pl.ds(pl.multiple_of(t * c, c), c)
            for k in range(TOP_K):
                src = pl.multiple_of(idx_ref[0, 0, t * TOP_K + k], c)
                pltpu.make_async_copy(ys_ref.at[pl.ds(src, c)], buf_ref.at[half, k, slab],
                                      sem.at[half]).start(priority=k % 2)
            return carry
        lax.fori_loop(part * (tm // parts), (part + 1) * (tm // parts), issue, 0)

    def gather_next(part):
        pl.when(has_next)(lambda: gather(next_ref, 1 - cur, part))

    @pl.when(i == 0)
    def _():
        for part in range(parts):
            gather(dest_ref, 0, part)

    gather_next(0)
    hb = _unpack_bf16_pairs(_from_row_tiles(hp_ref, c))
    hid = (_silu(jnp.dot(hb, sg_ref[...], preferred_element_type=F32))
           * jnp.dot(hb, su_ref[...], preferred_element_type=F32))
    f = jnp.dot(hid.astype(BF16), sd_ref[...], preferred_element_type=F32)
    gather_next(1)
    for k in range(TOP_K):
        pltpu.make_async_copy(ys_ref.at[pl.ds(0, tm * c)], buf_ref.at[cur, k], sem.at[cur]).wait()

    def weighted(t, carry):
        slab = pl.ds(pl.multiple_of(t * c, c), c)
        acc_hi = jnp.zeros((c, LANES), F32)
        acc_lo = jnp.zeros((c, LANES), F32)
        for k in range(TOP_K):
            packed = buf_ref[cur, k, slab, :]
            gk = gate_ref[0, 0, t * TOP_K + k]
            acc_hi = acc_hi + gk * lax.bitcast_convert_type(packed & jnp.uint32(0xFFFF0000), F32)
            acc_lo = acc_lo + gk * lax.bitcast_convert_type(packed << 16, F32)
        acc_ref[0, slab, :] = acc_hi
        acc_ref[1, slab, :] = acc_lo
        return carry

    lax.fori_loop(0, tm // 2, weighted, 0, unroll=4)
    gather_next(2)
    lax.fori_loop(tm // 2, tm, weighted, 0, unroll=4)
    gather_next(3)
    f = f + jnp.concatenate([_from_row_tiles(acc_ref, c, lead=(0,)), _from_row_tiles(acc_ref, c, lead=(1,))], axis=1)
    o_ref[...] = x_ref[...] + gf_ref[0] * _rms(f, g3_ref[...])


def _combine(lay, dest_flat, gate_flat, hp, x_mid, mods, g3, s_gate, s_up, s_down, ys, lat_only, compact_out):
    row = lambda i: (lay.tile(i, lat_only), 0)
    full = lambda shape: pl.BlockSpec(shape, lambda i: (0, 0))
    out_rows = lay.n_tiles(lat_only) * TM if compact_out else lay.rows
    out_map = (lambda i: (i, 0)) if compact_out else row
    n_steps = lay.n_tiles(lat_only)
    flat = lambda f: pl.BlockSpec((1, 1, TM * TOP_K), lambda i: (lay.tile(f(i), lat_only), 0, 0),
                                  memory_space=pltpu.SMEM)
    return pl.pallas_call(
        _combine_kernel,
        grid=(n_steps,),
        in_specs=[flat(lambda i: i), flat(lambda i: jnp.minimum(i + 1, n_steps - 1)), flat(lambda i: i),
                  pl.BlockSpec((TM * PACKED_TILES, LANES), row),
                  pl.BlockSpec((TM, D_MODEL), row), _mod_spec(lay, 5, lat_only),
                  full((1, D_MODEL)), full((D_MODEL, EXPERT_HIDDEN)), full((D_MODEL, EXPERT_HIDDEN)),
                  full((EXPERT_HIDDEN, D_MODEL)), pl.BlockSpec(memory_space=pl.ANY)],
        out_specs=pl.BlockSpec((TM, D_MODEL), out_map),
        scratch_shapes=[pltpu.VMEM((2, TOP_K, TM * PACKED_TILES, LANES), U32),
                        pltpu.VMEM((2, TM * PACKED_TILES, LANES), F32), pltpu.SemaphoreType.DMA((2,))],
        out_shape=jax.ShapeDtypeStruct((out_rows, D_MODEL), F32),
        compiler_params=_params(("arbitrary",)),
        name="combine",
    )(dest_flat, dest_flat, gate_flat, hp, x_mid, mods, g3, s_gate, s_up, s_down, ys)


def _rope_tables(lay):
    n_freq = HEAD_DIM // 4
    inv_freq = ROPE_THETA ** (-jnp.arange(n_freq, dtype=F32) / n_freq)
    t = jnp.arange(lay.n_lat)
    pos = jnp.stack([(t // GRID_W).astype(F32), (t % GRID_W).astype(F32)], axis=-1)
    ang = pos[:, :, None, None] * inv_freq
    ang = jnp.broadcast_to(ang, (lay.n_lat, 2, 2, n_freq)).reshape(lay.n_lat, HEAD_DIM)
    sign = jnp.tile(jnp.concatenate([-jnp.ones(n_freq, F32), jnp.ones(n_freq, F32)]), 2)
    cos = jnp.concatenate([jnp.ones((lay.n_ctx, HEAD_DIM), F32), jnp.cos(ang)], axis=0)
    sin = jnp.concatenate([jnp.zeros((lay.n_ctx, HEAD_DIM), F32), jnp.sin(ang) * sign], axis=0)
    return cos, sin


def _moe_plan(counts, n_blocks):
    counts = counts.reshape(-1)
    padded = (counts + MOE_BLK - 1) // MOE_BLK * MOE_BLK
    pad_end = jnp.cumsum(padded)
    pad_start = (pad_end - padded).astype(I32)
    blk_row = jnp.arange(n_blocks, dtype=I32) * MOE_BLK
    block_e = jnp.minimum(jnp.sum(blk_row[:, None] >= pad_end[None, :], axis=1), N_EXPERTS - 1).astype(I32)
    own = block_e[:, None] == jnp.arange(N_EXPERTS, dtype=I32)[None, :]
    count_b = jnp.sum(jnp.where(own, counts[None, :], 0), axis=1)
    start_b = jnp.sum(jnp.where(own, pad_start[None, :], 0), axis=1)
    block_valid = jnp.clip(count_b - (blk_row - start_b), 0, MOE_BLK).astype(I32)
    n_used = jnp.maximum(pad_end[-1:] // MOE_BLK, 1).astype(I32)
    return pad_start, block_e, block_valid, n_used


def kernel(x, c, ctx, c_ctx, mod_w, mod_b, norm_g, w_in, w_out, conv_w, conv_b, lru_wr, lru_br, lru_wi, lru_bi,
           lru_lambda, gla_w2, gla_b2, gla_norm_g, qk_norm_g, win_sink, router_w, router_bias, exp_w_gate,
           exp_w_up, exp_w_down, sh_w_gate, sh_w_up, sh_w_down):
    bsz, n_lat, d = x.shape
    n_ctx = ctx.shape[1]
    depth = mod_w.shape[0]
    assert d == D_MODEL and bsz < MOD_ROWS
    lay = _Layout(bsz, n_lat, n_ctx)

    cc = jnp.zeros((MOD_ROWS, d), F32).at[:bsz].set(c).at[bsz].set(c_ctx)
    mods_all = _adaln(cc, mod_w, mod_b)
    cos, sin = _rope_tables(lay)
    xt = None

    splits = np.cumsum([0, W_A, W_B_RAW, W_C, W_D])
    for layer in range(depth):
        last = layer == depth - 1
        need_ctx = not last
        lat_only = last
        mods = mods_all[layer].reshape(MOD_ROWS * 6, 1, d)
        g = norm_g[layer]
        wl = w_in[layer]
        w_cat = jnp.concatenate(
            [wl[:, splits[0]:splits[1]], wl[:, splits[1]:splits[2]], jnp.zeros((d, W_B - W_B_RAW), F32),
             wl[:, splits[2]:splits[3]], wl[:, splits[3]:splits[4]]], axis=1).astype(BF16)
        if xt is None:
            p_a, p_b, p_c, p_d, xt = _inproj(lay, (x.reshape(bsz * n_lat, d), ctx.reshape(bsz * n_ctx, d)), mods,
                                             g[0:1], w_cat)
        else:
            p_a, p_b, p_c, p_d = _inproj(lay, (xt,), mods, g[0:1], w_cat)

        y_a = _rglru(lay, p_a, conv_w[layer], conv_b[layer], lru_wr[layer], lru_br[layer], lru_wi[layer],
                     lru_bi[layer], lru_lambda[layer])
        w2pad = jnp.zeros((2, LANES, GLA_HEADS * GLA_DK), F32)
        w2pad = w2pad.at[0, :GLA_RANK].set(gla_w2[layer, 0]).at[1, GLA_RANK:2 * GLA_RANK].set(gla_w2[layer, 1])
        y_b = _gla(lay, p_b, w2pad, gla_b2[layer], gla_norm_g[layer])
        y_c = _attention(lay, p_c, cos, sin, qk_norm_g[layer], window=False, need_ctx=need_ctx)
        y_d = _attention(lay, p_d, cos, sin, win_sink[layer], window=True, need_ctx=need_ctx)

        x_mid, h, hp = _outproj(lay, (y_a, y_b, y_c, y_d), w_out[layer].astype(BF16), xt, mods, g[1:2], g[2:3],
                                lat_only)
        top_e, gate, rank, counts = _router(lay, h, router_w[layer].T, router_bias[layer], lat_only)
        n_assign = lay.n_tiles(lat_only) * TM * TOP_K
        n_blocks = -(-(n_assign + N_EXPERTS * (MOE_BLK - 1)) // MOE_BLK)
        pad_start, block_e, block_valid, n_used = _moe_plan(counts, n_blocks)
        owner = top_e[:, :, None] == jnp.arange(N_EXPERTS, dtype=I32)
        dest = jnp.sum(jnp.where(owner, pad_start, 0), axis=-1) + rank
        dest_flat = (dest * PACKED_TILES).T.reshape(lay.rows // TM, 1, TM * TOP_K)
        xs = _dispatch(lay, dest_flat, hp, n_blocks * MOE_BLK, lat_only)
        ys = _experts(block_e, block_valid, n_used, xs, exp_w_gate, exp_w_up, exp_w_down, layer)
        gate_flat = gate.T.reshape(lay.rows // TM, 1, TM * TOP_K)
        xt = _combine(lay, dest_flat, gate_flat, hp, x_mid, mods, g[3:4], sh_w_gate[layer].astype(BF16),
                      sh_w_up[layer].astype(BF16), sh_w_down[layer].astype(BF16), ys, lat_only, compact_out=last)
    return xt.reshape(bsz, n_lat, d)
```

```python
import functools

import numpy as np
import jax
import jax.numpy as jnp
from jax import lax
from jax.experimental import pallas as pl
from jax.experimental.pallas import tpu as pltpu

F32 = jnp.float32
BF16 = jnp.bfloat16
I32 = jnp.int32
U32 = jnp.uint32
HIGHEST = lax.Precision.HIGHEST
NT_DIMS = (((1,), (1,)), ((), ()))
TN_DIMS = (((0,), (0,)), ((), ()))

LANES = 128
SUBLANES = 8
VMEM_LIMIT_BYTES = 56 * 1024 * 1024

D_MODEL = 2048
GRID_W = 64
HEAD_DIM = 128
GROUP_WIDTH = D_MODEL // 4
LRU_BLOCKS = 4
CONV_W = 4
RGLRU_C = 8.0
GLA_HEADS = 4
GLA_DK = 64
GLA_DV = 128
GLA_RANK = 16
GLA_TAU = 16.0
GLA_CHUNK = 64
GLA_HEADS_PER_STEP = 4
KV_HEADS = 2
WINDOW = 128
N_EXPERTS = 64
N_GROUPS = 8
TOPK_GROUPS = 4
TOP_K = 8
EXPERT_HIDDEN = 512
ROUTED_SCALE = 2.5
RMS_EPS = 1e-6
ROPE_THETA = 10000.0
NEG_INF = -1e30

W_A = 2 * GROUP_WIDTH
W_B_RAW = 2 * GLA_HEADS * GLA_DK + 2 * GLA_HEADS * GLA_DV + 2 * GLA_RANK
W_B = 1664
W_C = GROUP_WIDTH + 2 * KV_HEADS * HEAD_DIM
W_D = W_C
Z_COL_BLOCK = (2 * GLA_HEADS * GLA_DK + 2 * GLA_HEADS * GLA_DV) // LANES

TM = 256
MOE_BLK = 512
PACKED_TILES = D_MODEL // 2 // LANES
MOD_ROWS = 16


def _params(sem, vmem=VMEM_LIMIT_BYTES):
    return pltpu.CompilerParams(dimension_semantics=sem, vmem_limit_bytes=vmem)


def _rms(x, g):
    return x * lax.rsqrt(jnp.mean(x * x, axis=-1, keepdims=True) + RMS_EPS) * g


def _silu(x):
    return x * jax.nn.sigmoid(x)


def _pack_bf16_pairs(h):
    n = h.shape[1] // 2
    hb = h.astype(BF16).astype(F32)
    hi = lax.bitcast_convert_type(hb[:, :n], U32)
    lo = lax.bitcast_convert_type(hb[:, n:], U32)
    return hi | (lo >> 16)


def _to_row_tiles(ref, x):
    m = x.shape[0]
    c = x.shape[1] // LANES
    for j in range(c):
        ref[pl.ds(j, m, stride=c), :] = x[:, j * LANES:(j + 1) * LANES]


def _from_row_tiles(ref, c, lead=()):
    m = ref.shape[-2] // c
    return jnp.concatenate([ref[lead + (pl.ds(j, m, stride=c), slice(None))] for j in range(c)], axis=1)


def _unpack_bf16_pairs(p):
    hi = lax.bitcast_convert_type(p & jnp.uint32(0xFFFF0000), F32)
    lo = lax.bitcast_convert_type(p << 16, F32)
    return jnp.concatenate([hi, lo], axis=1).astype(BF16)


def _adaln_kernel(c_ref, w_ref, b_ref, o_ref):
    a = _silu(c_ref[...])
    o_ref[0] = jnp.dot(a, w_ref[0], preferred_element_type=F32, precision=HIGHEST) + b_ref[0]


def _adaln(cc, mod_w, mod_b):
    depth, d, n = mod_w.shape
    tn = 1024
    return pl.pallas_call(
        _adaln_kernel,
        grid=(depth, n // tn),
        in_specs=[pl.BlockSpec((MOD_ROWS, d), lambda l, j: (0, 0)),
                  pl.BlockSpec((1, d, tn), lambda l, j: (l, 0, j)),
                  pl.BlockSpec((1, 1, tn), lambda l, j: (l, 0, j))],
        out_specs=pl.BlockSpec((1, MOD_ROWS, tn), lambda l, j: (l, 0, j)),
        out_shape=jax.ShapeDtypeStruct((depth, MOD_ROWS, n), F32),
        compiler_params=_params(("arbitrary", "arbitrary")),
        name="adaln",
    )(cc, mod_w, mod_b.reshape(depth, 1, n))


class _Layout:
    def __init__(self, bsz, n_lat, n_ctx):
        self.bsz, self.n_lat, self.n_ctx = bsz, n_lat, n_ctx
        self.seq = n_ctx + n_lat
        self.rows = bsz * self.seq
        assert n_ctx % TM == 0 and n_lat % TM == 0
        self.tiles_b = self.seq // TM
        self.ctx_tiles = n_ctx // TM
        self.lat_tiles = n_lat // TM

    def n_tiles(self, lat_only):
        return self.bsz * (self.lat_tiles if lat_only else self.tiles_b)

    def tile(self, i, lat_only):
        if lat_only:
            return (i // self.lat_tiles) * self.tiles_b + self.ctx_tiles + i % self.lat_tiles
        return i

    def mod_row(self, i, lat_only):
        if lat_only:
            return i // self.lat_tiles
        return jnp.where(i % self.tiles_b < self.ctx_tiles, self.bsz, i // self.tiles_b)


def _mod_spec(lay, chunk, lat_only):
    return pl.BlockSpec((1, 1, D_MODEL), lambda i: (lay.mod_row(i, lat_only) * 6 + chunk, 0, 0))


def _inproj_kernel(*refs, tiles_b, ctx_tiles):
    if ctx_tiles:
        x_ref, c_ref, sh_ref, sc_ref, g_ref, w_ref, oa_ref, ob_ref, oc_ref, od_ref, xt_ref = refs
        x = jnp.where(pl.program_id(0) % tiles_b < ctx_tiles, c_ref[...], x_ref[...])
        xt_ref[...] = x
    else:
        x_ref, sh_ref, sc_ref, g_ref, w_ref, oa_ref, ob_ref, oc_ref, od_ref = refs
        x = x_ref[...]
    h = _rms(x, g_ref[...]) * (1.0 + sc_ref[0]) + sh_ref[0]
    hb = h.astype(BF16)
    c0 = 0
    for o_ref in (oa_ref, ob_ref, oc_ref, od_ref):
        width = o_ref.shape[1]
        o_ref[...] = jnp.dot(hb, w_ref[:, c0:c0 + width], preferred_element_type=F32)
        c0 += width


def _inproj(lay, xs, mods, g0, w_cat):
    widths = (W_A, W_B, W_C, W_D)
    row = lambda i: (i, 0)
    split = len(xs) == 2
    x_specs = [pl.BlockSpec((TM, D_MODEL), row)]
    out_specs = [pl.BlockSpec((TM, w), row) for w in widths]
    out_shape = [jax.ShapeDtypeStruct((lay.rows, w), F32) for w in widths]
    if split:
        lat = lambda i: ((i // lay.tiles_b) * lay.lat_tiles + jnp.maximum(i % lay.tiles_b - lay.ctx_tiles, 0), 0)
        ctx = lambda i: ((i // lay.tiles_b) * lay.ctx_tiles + jnp.minimum(i % lay.tiles_b, lay.ctx_tiles - 1), 0)
        x_specs = [pl.BlockSpec((TM, D_MODEL), lat), pl.BlockSpec((TM, D_MODEL), ctx)]
        out_specs.append(pl.BlockSpec((TM, D_MODEL), row))
        out_shape.append(jax.ShapeDtypeStruct((lay.rows, D_MODEL), F32))
    return pl.pallas_call(
        functools.partial(_inproj_kernel, tiles_b=lay.tiles_b, ctx_tiles=lay.ctx_tiles if split else 0),
        grid=(lay.n_tiles(False),),
        in_specs=x_specs + [_mod_spec(lay, 0, False), _mod_spec(lay, 1, False),
                            pl.BlockSpec((1, D_MODEL), lambda i: (0, 0)),
                            pl.BlockSpec((D_MODEL, sum(widths)), lambda i: (0, 0), pipeline_mode=pl.Buffered(1))],
        out_specs=out_specs,
        out_shape=out_shape,
        compiler_params=_params(("arbitrary",)),
        name="inproj",
    )(*xs, mods, mods, g0, w_cat)


def _rglru_kernel(x_ref, g_ref, cw_ref, cb_ref, wr_ref, br_ref, wi_ref, bi_ref, lam_ref, o_ref,
                  xs_ref, a_ref, b_ref, h_ref, *, n_ctx):
    seq = x_ref.shape[0]
    pad = SUBLANES
    n_tiles = seq // SUBLANES
    ctx_tiles = n_ctx // SUBLANES
    row = lax.broadcasted_iota(I32, (seq, LANES), 0)

    xs_ref[0:pad, :] = jnp.zeros((pad, LANES), F32)
    xs_ref[pad + seq:2 * pad + seq, :] = jnp.zeros((pad, LANES), F32)
    xs_ref[pad:pad + seq, :] = x_ref[...]
    u = jnp.zeros((seq, LANES), F32) + cb_ref[...]
    for j in range(CONV_W):
        off = j - 2
        xsh = xs_ref[pad + off:pad + off + seq, :]
        if off < 0:
            xsh = jnp.where((row >= n_ctx) & (row + off < n_ctx), 0.0, xsh)
        elif off > 0:
            xsh = jnp.where((row < n_ctx) & (row + off >= n_ctx), 0.0, xsh)
        u = u + cw_ref[j:j + 1, :] * xsh

    ub = u.astype(BF16)
    tile_shape = (n_tiles, SUBLANES, LANES)
    rmod = lax.broadcasted_iota(I32, tile_shape, 1)
    for d in range(2):
        reverse = d == 1
        r = jax.nn.sigmoid(jnp.dot(ub, wr_ref[d, 0].astype(BF16), preferred_element_type=F32) + br_ref[d:d + 1, :])
        gi = jax.nn.sigmoid(jnp.dot(ub, wi_ref[d, 0].astype(BF16), preferred_element_type=F32) + bi_ref[d:d + 1, :])
        lam = lam_ref[d:d + 1, :]
        softplus_neg = jnp.maximum(-lam, 0.0) + jnp.log(1.0 + jnp.exp(-jnp.abs(lam)))
        a = jnp.exp(-RGLRU_C * r * softplus_neg)
        b = jnp.sqrt(1.0 - a * a) * (gi * u)
        a = a.reshape(tile_shape)
        b = b.reshape(tile_shape)
        for s in (1, 2, 4):
            if reverse:
                ap, bp, ok = pltpu.roll(a, SUBLANES - s, 1), pltpu.roll(b, SUBLANES - s, 1), rmod < SUBLANES - s
            else:
                ap, bp, ok = pltpu.roll(a, s, 1), pltpu.roll(b, s, 1), rmod >= s
            b = jnp.where(ok, a * bp + b, b)
            a = jnp.where(ok, a * ap, a)
        a_ref[...] = a.reshape(seq, LANES)
        b_ref[...] = b.reshape(seq, LANES)

        def step(t, h, reverse=reverse):
            r0 = pl.multiple_of(t * SUBLANES, SUBLANES)
            h8 = a_ref[pl.ds(r0, SUBLANES), :] * h + b_ref[pl.ds(r0, SUBLANES), :]
            if reverse:
                h_ref[pl.ds(r0, SUBLANES), :] = h_ref[pl.ds(r0, SUBLANES), :] + h8
                return jnp.broadcast_to(h8[0:1, :], (SUBLANES, LANES))
            h_ref[pl.ds(r0, SUBLANES), :] = h8
            return jnp.broadcast_to(h8[SUBLANES - 1:SUBLANES, :], (SUBLANES, LANES))

        h0 = jnp.zeros((SUBLANES, LANES), F32)
        if reverse:
            hc = lax.fori_loop(0, ctx_tiles, lambda i, h: step(ctx_tiles - 1 - i, h), h0)
            lax.fori_loop(0, n_tiles - ctx_tiles, lambda i, h: step(n_tiles - 1 - i, h), hc)
        else:
            lax.fori_loop(0, n_tiles, step, h0)

    g = g_ref[...]
    gelu = 0.5 * g * (1.0 + jnp.tanh(np.sqrt(2.0 / np.pi) * (g + 0.044715 * (g * g * g))))
    o_ref[...] = h_ref[...] * gelu


def _rglru(lay, p_a, conv_w, conv_b, wr, br, wi, bi, lam):
    seq = lay.seq
    nb = LRU_BLOCKS
    col = lambda off: (lambda b, j: (b, off + j))
    vec = lambda rows: pl.BlockSpec((rows, LANES), lambda b, j: (0, j))
    wspec = pl.BlockSpec((2, 1, LANES, LANES), lambda b, j: (0, j, 0, 0))
    return pl.pallas_call(
        functools.partial(_rglru_kernel, n_ctx=lay.n_ctx),
        grid=(lay.bsz, nb),
        in_specs=[pl.BlockSpec((seq, LANES), col(0)), pl.BlockSpec((seq, LANES), col(nb)),
                  vec(CONV_W), vec(1), wspec, vec(2), wspec, vec(2), vec(2)],
        out_specs=pl.BlockSpec((seq, LANES), col(0)),
        out_shape=jax.ShapeDtypeStruct((lay.rows, GROUP_WIDTH), F32),
        scratch_shapes=[pltpu.VMEM((seq + 2 * SUBLANES, LANES), F32), pltpu.VMEM((seq, LANES), F32),
                        pltpu.VMEM((seq, LANES), F32), pltpu.VMEM((seq, LANES), F32)],
        compiler_params=_params(("arbitrary", "arbitrary")),
        name="rglru",
    )(p_a, p_a, conv_w, conv_b.reshape(1, -1), wr, br, wi, bi, lam)


def _gla_kernel(q_ref, k_ref, v_ref, g_ref, z_ref, w2_ref, b2_ref, ng_ref, o_ref,
                laf_ref, lab_ref, of_ref, ob_ref, *, n_ctx):
    seq = q_ref.shape[0]
    nh = q_ref.shape[1] // GLA_DK
    qk_w = nh * GLA_DK
    cs = GLA_CHUNK
    n_chunks = seq // cs
    ctx_chunks = n_ctx // cs
    z = z_ref[...]
    for d, la_ref in ((0, laf_ref), (1, lab_ref)):
        pre = jnp.dot(z, w2_ref[d], preferred_element_type=F32, precision=HIGHEST) + b2_ref[d:d + 1, :]
        log_sig = jnp.minimum(pre, 0.0) - jnp.log(1.0 + jnp.exp(-jnp.abs(pre)))
        la_ref[...] = log_sig * (1.0 / GLA_TAU)

    ri = lax.broadcasted_iota(I32, (cs, cs), 0)
    ci = lax.broadcasted_iota(I32, (cs, cs), 1)
    scale = GLA_DK ** -0.5

    def chunk(c, states, reverse):
        r0 = pl.multiple_of(c * cs, cs)
        rows = pl.ds(r0, cs)
        keep = (ri <= ci) if reverse else (ri >= ci)
        la = (lab_ref if reverse else laf_ref)[rows, :]
        la_hi = la.astype(BF16)
        rest = la - la_hi.astype(F32)
        la_mid = rest.astype(BF16)
        la_lo = (rest - la_mid.astype(F32)).astype(BF16)
        parts = jnp.dot(keep.astype(BF16), jnp.concatenate([la_hi, la_mid, la_lo], axis=1),
                        preferred_element_type=F32)
        cum = parts[:, 0:qk_w] + parts[:, qk_w:2 * qk_w] + parts[:, 2 * qk_w:3 * qk_w]
        last = cum[0:1, :] if reverse else cum[cs - 1:cs, :]
        q = q_ref[rows, :] * scale
        k = k_ref[rows, :]
        qe = q * jnp.exp(cum)
        ke = k * jnp.exp(-cum)
        kl = k * jnp.exp(last - cum)
        el = jnp.exp(last)
        out_ref = ob_ref if reverse else of_ref
        new_states = []
        for hh in range(nh):
            dk = slice(hh * GLA_DK, (hh + 1) * GLA_DK)
            dv = slice(hh * GLA_DV, (hh + 1) * GLA_DV)
            qh = qe[:, dk].astype(BF16)
            vh = v_ref[rows, dv].astype(BF16)
            st = states[hh]
            s = lax.dot_general(qh, ke[:, dk].astype(BF16), NT_DIMS, preferred_element_type=F32)
            s = jnp.where(keep, s, 0.0)
            o = (jnp.dot(s.astype(BF16), vh, preferred_element_type=F32)
                 + lax.dot_general(qh, st.astype(BF16), NT_DIMS, preferred_element_type=F32))
            out_ref[rows, dv] = o
            new_states.append(st * el[:, dk]
                              + lax.dot_general(vh, kl[:, dk].astype(BF16), TN_DIMS, preferred_element_type=F32))
        return tuple(new_states)

    def body(i, carry):
        sf, sb = carry
        sf = chunk(i, sf, False)
        cb = jnp.where(i < ctx_chunks, ctx_chunks - 1 - i, n_chunks - 1 + ctx_chunks - i)
        sb = chunk(cb, sb, True)
        return sf, sb

    zero = (jnp.zeros((GLA_DV, GLA_DK), F32),) * nh
    lax.fori_loop(0, n_chunks, body, (zero, zero), unroll=4)

    ng = ng_ref[...]
    for hh in range(nh):
        dv = slice(hh * GLA_DV, (hh + 1) * GLA_DV)
        o = of_ref[:, dv] + ob_ref[:, dv]
        o_ref[:, dv] = _rms(o, ng) * _silu(g_ref[:, dv])


def _gla(lay, p_b, w2pad, b2, norm_g):
    seq = lay.seq
    hp = GLA_HEADS // GLA_HEADS_PER_STEP
    qk_w = GLA_HEADS_PER_STEP * GLA_DK
    v_w = GLA_HEADS_PER_STEP * GLA_DV
    k_off = GLA_HEADS * GLA_DK // qk_w
    v_off = 2 * GLA_HEADS * GLA_DK // v_w
    g_off = v_off + GLA_HEADS * GLA_DV // v_w
    once = pl.Buffered(1)
    blk = lambda w, off, mode=None: pl.BlockSpec((seq, w), lambda b, j: (b, off + j), pipeline_mode=mode)
    return pl.pallas_call(
        functools.partial(_gla_kernel, n_ctx=lay.n_ctx),
        grid=(lay.bsz, hp),
        in_specs=[blk(qk_w, 0, once), blk(qk_w, k_off, once), blk(v_w, v_off, once), blk(v_w, g_off, once),
                  pl.BlockSpec((seq, LANES), lambda b, j: (b, Z_COL_BLOCK), pipeline_mode=once),
                  pl.BlockSpec((2, LANES, qk_w), lambda b, j: (0, 0, j)),
                  pl.BlockSpec((2, qk_w), lambda b, j: (0, j)),
                  pl.BlockSpec((1, GLA_DV), lambda b, j: (0, 0))],
        out_specs=blk(v_w, 0),
        out_shape=jax.ShapeDtypeStruct((lay.rows, GROUP_WIDTH), F32),
        scratch_shapes=[pltpu.VMEM((seq, qk_w), F32), pltpu.VMEM((seq, qk_w), F32),
                        pltpu.VMEM((seq, v_w), F32), pltpu.VMEM((seq, v_w), F32)],
        compiler_params=_params(("arbitrary", "arbitrary")),
        name="gla",
    )(p_b, p_b, p_b, p_b, p_b, w2pad, b2, norm_g.reshape(1, -1))


def _rope(x, cos, sin_signed):
    lane = lax.broadcasted_iota(I32, x.shape, 1)
    partner = jnp.where((lane & 32) != 0, pltpu.roll(x, 32, 1), pltpu.roll(x, HEAD_DIM - 32, 1))
    return x * cos + partner * sin_signed


def _attn_kernel(*refs, window, need_ctx, tq, n_ctx):
    if window:
        sink_ref, q_ref, k_ref, v_ref, cos_ref, sin_ref, o_ref, ks_ref, vs_ref = refs
        qkg_ref = None
    else:
        q_ref, k_ref, v_ref, cos_ref, sin_ref, qkg_ref, o_ref, ks_ref, vs_ref = refs
    seq = k_ref.shape[0]
    n_lat = seq - n_ctx
    g = pl.program_id(1)
    qi = pl.program_id(2)
    jq = qi if need_ctx else qi + n_ctx // tq

    @pl.when(qi == 0)
    def _():
        k = k_ref[...]
        if not window:
            k = _rms(k, qkg_ref[1:2, :])
        ks_ref[...] = _rope(k, cos_ref[...], sin_ref[...]).astype(BF16)
        vs_ref[...] = v_ref[...].astype(BF16)

    r0 = pl.multiple_of(jq * tq, tq)
    cos = cos_ref[pl.ds(r0, tq), :]
    sin = sin_ref[pl.ds(r0, tq), :]
    heads = []
    for hh in range(2):
        qh = q_ref[:, hh * HEAD_DIM:(hh + 1) * HEAD_DIM]
        if not window:
            qh = _rms(qh, qkg_ref[0:1, :])
        heads.append((_rope(qh, cos, sin) * HEAD_DIM ** -0.5).astype(BF16))
    q2 = jnp.concatenate(heads, axis=0)

    def finish(o):
        o_ref[:, 0:HEAD_DIM] = o[0:tq]
        o_ref[:, HEAD_DIM:2 * HEAD_DIM] = o[tq:2 * tq]

    def scores(lo, n):
        return lax.dot_general(q2, ks_ref[pl.ds(lo, n), :], NT_DIMS, preferred_element_type=F32)

    def pv(p, lo, n):
        return jnp.dot(p.astype(BF16), vs_ref[pl.ds(lo, n), :], preferred_element_type=F32)

    if window:
        rowi = lax.broadcasted_iota(I32, (2 * tq, 1), 0)
        sink = jnp.where(rowi < tq, sink_ref[2 * g], sink_ref[2 * g + 1])

    def ctx_tile():
        s = scores(0, n_ctx)
        m = jnp.max(s, axis=-1, keepdims=True)
        if window:
            m = jnp.maximum(m, sink)
        p = jnp.exp(s - m)
        l = jnp.sum(p, axis=-1, keepdims=True)
        if window:
            l = l + jnp.exp(sink - m)
        finish(pv(p, 0, n_ctx) / l)

    def lat_tile():
        if not window:
            s = scores(0, seq)
            m = jnp.max(s, axis=-1, keepdims=True)
            p = jnp.exp(s - m)
            finish(pv(p, 0, seq) / jnp.sum(p, axis=-1, keepdims=True))
            return
        band = tq + 2 * WINDOW
        jl = jq - n_ctx // tq
        start = jnp.clip(jl * tq - WINDOW, 0, n_lat - band)
        lo = pl.multiple_of(n_ctx + start, WINDOW)
        sb = scores(lo, band)
        q_pos = jl * tq + lax.broadcasted_iota(I32, (2 * tq, band), 0) % tq
        k_pos = start + lax.broadcasted_iota(I32, (2 * tq, band), 1)
        sb = jnp.where(jnp.abs(q_pos - k_pos) <= WINDOW, sb, NEG_INF)
        sc = scores(0, n_ctx)
        m = jnp.maximum(jnp.maximum(jnp.max(sb, axis=-1, keepdims=True), jnp.max(sc, axis=-1, keepdims=True)), sink)
        pb = jnp.exp(sb - m)
        pc = jnp.exp(sc - m)
        l = jnp.sum(pb, axis=-1, keepdims=True) + jnp.sum(pc, axis=-1, keepdims=True) + jnp.exp(sink - m)
        finish((pv(pb, lo, band) + pv(pc, 0, n_ctx)) / l)

    if need_ctx:
        pl.when(jq * tq < n_ctx)(ctx_tile)
        pl.when(jq * tq >= n_ctx)(lat_tile)
    else:
        lat_tile()


def _attention(lay, p, cos, sin, extra, *, window, need_ctx):
    seq = lay.seq
    tq = TM
    q_tiles_b = seq // tq
    nq = q_tiles_b if need_ctx else lay.n_lat // tq
    skip = 0 if need_ctx else lay.n_ctx // tq
    q_heads_w = 2 * HEAD_DIM
    k_off = GROUP_WIDTH // HEAD_DIM
    v_off = k_off + KV_HEADS

    def im(f):
        return (lambda b, g, qi, *_: f(b, g, qi))

    in_specs = [pl.BlockSpec((tq, q_heads_w), im(lambda b, g, qi: (b * q_tiles_b + skip + qi, g))),
                pl.BlockSpec((seq, HEAD_DIM), im(lambda b, g, qi: (b, k_off + g))),
                pl.BlockSpec((seq, HEAD_DIM), im(lambda b, g, qi: (b, v_off + g))),
                pl.BlockSpec((seq, HEAD_DIM), im(lambda b, g, qi: (0, 0))),
                pl.BlockSpec((seq, HEAD_DIM), im(lambda b, g, qi: (0, 0)))]
    args = [p, p, p, cos, sin]
    if not window:
        in_specs.append(pl.BlockSpec((2, HEAD_DIM), im(lambda b, g, qi: (0, 0))))
        args.append(extra)
    out_spec = pl.BlockSpec((tq, q_heads_w), im(lambda b, g, qi: (b * q_tiles_b + skip + qi, g)))
    kern = functools.partial(_attn_kernel, window=window, need_ctx=need_ctx, tq=tq, n_ctx=lay.n_ctx)
    scratch = [pltpu.VMEM((seq, HEAD_DIM), BF16), pltpu.VMEM((seq, HEAD_DIM), BF16)]
    out_shape = jax.ShapeDtypeStruct((lay.rows, GROUP_WIDTH), F32)
    sem = ("arbitrary", "arbitrary", "arbitrary")
    grid = (lay.bsz, KV_HEADS, nq)
    if window:
        return pl.pallas_call(
            kern,
            grid_spec=pltpu.PrefetchScalarGridSpec(num_scalar_prefetch=1, grid=grid, in_specs=in_specs,
                                                   out_specs=out_spec, scratch_shapes=scratch),
            out_shape=out_shape, compiler_params=_params(sem), name="win_attn",
        )(extra, *args)
    return pl.pallas_call(kern, grid=grid, in_specs=in_specs, out_specs=out_spec, scratch_shapes=scratch,
                          out_shape=out_shape, compiler_params=_params(sem), name="glob_attn")(*args)


def _outproj_kernel(ya_ref, yb_ref, yc_ref, yd_ref, w_ref, x_ref, gm_ref, sh_ref, sc_ref, g1_ref, g2_ref,
                    xo_ref, h_ref, hp_ref):
    acc = None
    for i, y_ref in enumerate((ya_ref, yb_ref, yc_ref, yd_ref)):
        part = jnp.dot(y_ref[...].astype(BF16), w_ref[i * GROUP_WIDTH:(i + 1) * GROUP_WIDTH, :],
                       preferred_element_type=F32)
        acc = part if acc is None else acc + part
    x_new = x_ref[...] + gm_ref[0] * _rms(acc, g1_ref[...])
    xo_ref[...] = x_new
    h = _rms(x_new, g2_ref[...]) * (1.0 + sc_ref[0]) + sh_ref[0]
    h_ref[...] = h
    _to_row_tiles(hp_ref, _pack_bf16_pairs(h))


def _outproj(lay, ys, w_out, xt, mods, g1, g2, lat_only):
    row = lambda i: (lay.tile(i, lat_only), 0)
    full = lambda shape: pl.BlockSpec(shape, lambda i: (0, 0))
    yspec = pl.BlockSpec((TM, GROUP_WIDTH), row)
    xspec = pl.BlockSpec((TM, D_MODEL), row)
    return pl.pallas_call(
        _outproj_kernel,
        grid=(lay.n_tiles(lat_only),),
        in_specs=[yspec, yspec, yspec, yspec, full((D_MODEL, D_MODEL)), xspec,
                  _mod_spec(lay, 2, lat_only), _mod_spec(lay, 3, lat_only), _mod_spec(lay, 4, lat_only),
                  full((1, D_MODEL)), full((1, D_MODEL))],
        out_specs=[xspec, xspec, pl.BlockSpec((TM * PACKED_TILES, LANES), row)],
        out_shape=[jax.ShapeDtypeStruct((lay.rows, D_MODEL), F32), jax.ShapeDtypeStruct((lay.rows, D_MODEL), F32),
                   jax.ShapeDtypeStruct((lay.rows * PACKED_TILES, LANES), U32)],
        compiler_params=_params(("arbitrary",)),
        name="outproj",
    )(*ys, w_out, xt, mods, mods, mods, g1, g2)


def _route(h, wt_ref, bias_ref, e_ref, gate_ref, rank_ref, cnt_ref, carry_ref):
    i = pl.program_id(0)
    tm = h.shape[0]
    ne = N_EXPERTS
    gsz = ne // N_GROUPS

    @pl.when(i == 0)
    def _():
        carry_ref[...] = jnp.zeros_like(carry_ref)

    logits = lax.dot_general(wt_ref[...], h, NT_DIMS, preferred_element_type=F32, precision=HIGHEST)
    scores = jax.nn.sigmoid(logits)
    biased = scores + bias_ref[...]

    sub = lax.broadcasted_iota(I32, (gsz, tm), 0)
    gs_rows = []
    for grp in range(N_GROUPS):
        blk = biased[grp * gsz:(grp + 1) * gsz, :]
        m1 = jnp.max(blk, axis=0, keepdims=True)
        first = jnp.min(jnp.where(blk == m1, sub, gsz), axis=0, keepdims=True)
        m2 = jnp.max(jnp.where(sub == first, -jnp.inf, blk), axis=0, keepdims=True)
        gs_rows.append(m1 + m2)
    gs = jnp.concatenate(gs_rows, axis=0)

    gidx = lax.broadcasted_iota(I32, (N_GROUPS, tm), 0)
    chosen = jnp.zeros((N_GROUPS, tm), jnp.bool_)
    cur = gs
    for _ in range(TOPK_GROUPS):
        m = jnp.max(cur, axis=0, keepdims=True)
        first = jnp.min(jnp.where(cur == m, gidx, N_GROUPS), axis=0, keepdims=True)
        sel = gidx == first
        chosen = chosen | sel
        cur = jnp.where(sel, -jnp.inf, cur)

    eidx = lax.broadcasted_iota(I32, (ne, tm), 0)
    eligible = jnp.concatenate(
        [jnp.broadcast_to(chosen[grp:grp + 1, :], (gsz, tm)) for grp in range(N_GROUPS)], axis=0)
    cur = jnp.where(eligible, biased, -jnp.inf)
    sels, gates = [], []
    onehot = jnp.zeros((ne, tm), jnp.bool_)
    for k in range(TOP_K):
        m = jnp.max(cur, axis=0, keepdims=True)
        first = jnp.min(jnp.where(cur == m, eidx, ne), axis=0, keepdims=True)
        sel = eidx == first
        e_ref[k:k + 1, :] = first
        gates.append(jnp.sum(jnp.where(sel, scores, 0.0), axis=0, keepdims=True))
        sels.append(sel)
        onehot = onehot | sel
        cur = jnp.where(sel, -jnp.inf, cur)
    total = gates[0]
    for gk in gates[1:]:
        total = total + gk
    for k in range(TOP_K):
        gate_ref[k:k + 1, :] = gates[k] / total * ROUTED_SCALE

    tr = lax.broadcasted_iota(I32, (tm, tm), 0)
    tc = lax.broadcasted_iota(I32, (tm, tm), 1)
    before = (tr < tc).astype(BF16)
    oh = onehot.astype(F32)
    prefix = jnp.dot(oh.astype(BF16), before, preferred_element_type=F32) + carry_ref[...]
    for k in range(TOP_K):
        rank_ref[k:k + 1, :] = jnp.sum(jnp.where(sels[k], prefix, 0.0), axis=0, keepdims=True).astype(I32)
    carry = carry_ref[...] + jnp.sum(oh, axis=1, keepdims=True)
    carry_ref[...] = carry
    cnt_ref[...] = carry.astype(I32)


def _router_kernel(h_ref, wt_ref, bias_ref, e_ref, gate_ref, rank_ref, cnt_ref, carry_ref):
    _route(h_ref[...], wt_ref, bias_ref, e_ref, gate_ref, rank_ref, cnt_ref, carry_ref)


def _router(lay, h, router_wt, bias, lat_only):
    col = lambda i: (0, lay.tile(i, lat_only))
    kt = pl.BlockSpec((TOP_K, TM), col)
    return pl.pallas_call(
        _router_kernel,
        grid=(lay.n_tiles(lat_only),),
        in_specs=[pl.BlockSpec((TM, D_MODEL), lambda i: (lay.tile(i, lat_only), 0)),
                  pl.BlockSpec((N_EXPERTS, D_MODEL), lambda i: (0, 0)),
                  pl.BlockSpec((N_EXPERTS, 1), lambda i: (0, 0))],
        out_specs=[kt, kt, kt, pl.BlockSpec((N_EXPERTS, 1), lambda i: (0, 0))],
        out_shape=[jax.ShapeDtypeStruct((TOP_K, lay.rows), I32), jax.ShapeDtypeStruct((TOP_K, lay.rows), F32),
                   jax.ShapeDtypeStruct((TOP_K, lay.rows), I32), jax.ShapeDtypeStruct((N_EXPERTS, 1), I32)],
        scratch_shapes=[pltpu.VMEM((N_EXPERTS, 1), F32)],
        compiler_params=_params(("arbitrary",)),
        name="router",
    )(h, router_wt, bias.reshape(-1, 1))


def _dispatch_kernel(dest_ref, hp_ref, xs_ref, sem):
    c = PACKED_TILES
    tm = hp_ref.shape[0] // c

    def issue(t, carry):
        src = hp_ref.at[pl.ds(pl.multiple_of(t * c, c), c)]
        for k in range(TOP_K):
            dest = pl.multiple_of(dest_ref[0, 0, t * TOP_K + k], c)
            pltpu.make_async_copy(src, xs_ref.at[pl.ds(dest, c)], sem).start(priority=k % 2)
        return carry

    lax.fori_loop(0, tm, issue, 0, unroll=2)
    for k in range(TOP_K):
        pltpu.make_async_copy(hp_ref, xs_ref.at[pl.ds(0, tm * c)], sem).wait()


def _dispatch(lay, dest_flat, hp, n_rows, lat_only):
    return pl.pallas_call(
        _dispatch_kernel,
        grid=(lay.n_tiles(lat_only),),
        in_specs=[pl.BlockSpec((1, 1, TM * TOP_K), lambda i: (lay.tile(i, lat_only), 0, 0), memory_space=pltpu.SMEM),
                  pl.BlockSpec((TM * PACKED_TILES, LANES), lambda i: (lay.tile(i, lat_only), 0))],
        out_specs=pl.BlockSpec(memory_space=pl.ANY),
        scratch_shapes=[pltpu.SemaphoreType.DMA],
        out_shape=jax.ShapeDtypeStruct((n_rows * PACKED_TILES, LANES), U32),
        compiler_params=_params(("arbitrary",)),
        name="dispatch",
    )(dest_flat, hp)


def _experts_kernel(be_ref, bv_ref, nu_ref, xs_ref, wg_ref, wu_ref, wd_ref, ys_ref, wgb_ref, wub_ref, wdb_ref):
    j = pl.program_id(0)
    live = j < nu_ref[0]
    prev = be_ref[jnp.maximum(j - 1, 0)]

    @pl.when(live & ((j == 0) | (be_ref[j] != prev)))
    def _():
        wgb_ref[...] = wg_ref[0, 0].astype(BF16)
        wub_ref[...] = wu_ref[0, 0].astype(BF16)
        wdb_ref[...] = wd_ref[0, 0].astype(BF16)

    @pl.when(live)
    def _():
        packed = _from_row_tiles(xs_ref, PACKED_TILES)
        rowi = lax.broadcasted_iota(I32, packed.shape, 0)
        packed = jnp.where(rowi < bv_ref[j], packed, jnp.uint32(0))
        x = _unpack_bf16_pairs(packed)
        hid = (_silu(jnp.dot(x, wgb_ref[...], preferred_element_type=F32))
               * jnp.dot(x, wub_ref[...], preferred_element_type=F32))
        y = jnp.dot(hid.astype(BF16), wdb_ref[...], preferred_element_type=F32)
        _to_row_tiles(ys_ref, _pack_bf16_pairs(y))


def _experts(block_e, block_valid, n_used, xs, w_gate, w_up, w_down, layer):
    n_blocks = xs.shape[0] // (MOE_BLK * PACKED_TILES)
    last = lambda j, nu: jnp.minimum(j, nu[0] - 1)
    wmap = lambda j, be, bv, nu: (layer, be[last(j, nu)], 0, 0)
    return pl.pallas_call(
        _experts_kernel,
        grid_spec=pltpu.PrefetchScalarGridSpec(
            num_scalar_prefetch=3, grid=(n_blocks,),
            in_specs=[pl.BlockSpec((MOE_BLK * PACKED_TILES, LANES), lambda j, be, bv, nu: (last(j, nu), 0)),
                      pl.BlockSpec((1, 1, D_MODEL, EXPERT_HIDDEN), wmap),
                      pl.BlockSpec((1, 1, D_MODEL, EXPERT_HIDDEN), wmap),
                      pl.BlockSpec((1, 1, EXPERT_HIDDEN, D_MODEL), wmap)],
            out_specs=pl.BlockSpec((MOE_BLK * PACKED_TILES, LANES), lambda j, be, bv, nu: (last(j, nu), 0)),
            scratch_shapes=[pltpu.VMEM((D_MODEL, EXPERT_HIDDEN), BF16), pltpu.VMEM((D_MODEL, EXPERT_HIDDEN), BF16),
                            pltpu.VMEM((EXPERT_HIDDEN, D_MODEL), BF16)]),
        out_shape=jax.ShapeDtypeStruct((n_blocks * MOE_BLK * PACKED_TILES, LANES), U32),
        compiler_params=_params(("arbitrary",)),
        name="experts",
    )(block_e, block_valid, n_used, xs, w_gate, w_up, w_down)


def _combine_kernel(dest_ref, next_ref, gate_ref, hp_ref, x_ref, gf_ref, g3_ref, sg_ref, su_ref, sd_ref,
                    ys_ref, o_ref, buf_ref, acc_ref, sem):
    c = PACKED_TILES
    tm = hp_ref.shape[0] // c
    i = pl.program_id(0)
    cur = i % 2

    parts = 4
    has_next = i + 1 < pl.num_programs(0)

    def gather(idx_ref, half, part):
        def issue(t, carry):
            slab = pl.ds(pl.multiple_of(t * c, c), c)
            for k in range(TOP_K):
                src = pl.multiple_of(idx_ref[0, 0, t * TOP_K + k], c)
                pltpu.make_async_copy(ys_ref.at[pl.ds(src, c)], buf_ref.at[half, k, slab],
                                      sem.at[half]).start(priority=k % 2)
            return carry
        lax.fori_loop(part * (tm // parts), (part + 1) * (tm // parts), issue, 0)

    def gather_next(part):
        pl.when(has_next)(lambda: gather(next_ref, 1 - cur, part))

    @pl.when(i == 0)
    def _():
        for part in range(parts):
            gather(dest_ref, 0, part)

    gather_next(0)
    hb = _unpack_bf16_pairs(_from_row_tiles(hp_ref, c))
    hid = (_silu(jnp.dot(hb, sg_ref[...], preferred_element_type=F32))
           * jnp.dot(hb, su_ref[...], preferred_element_type=F32))
    f = jnp.dot(hid.astype(BF16), sd_ref[...], preferred_element_type=F32)
    gather_next(1)
    for k in range(TOP_K):
        pltpu.make_async_copy(ys_ref.at[pl.ds(0, tm * c)], buf_ref.at[cur, k], sem.at[cur]).wait()

    def weighted(t, carry):
        slab = pl.ds(pl.multiple_of(t * c, c), c)
        acc_hi = jnp.zeros((c, LANES), F32)
        acc_lo = jnp.zeros((c, LANES), F32)
        for k in range(TOP_K):
            packed = buf_ref[cur, k, slab, :]
            gk = gate_ref[0, 0, t * TOP_K + k]
            acc_hi = acc_hi + gk * lax.bitcast_convert_type(packed & jnp.uint32(0xFFFF0000), F32)
            acc_lo = acc_lo + gk * lax.bitcast_convert_type(packed << 16, F32)
        acc_ref[0, slab, :] = acc_hi
        acc_ref[1, slab, :] = acc_lo
        return carry

    lax.fori_loop(0, tm // 2, weighted, 0, unroll=4)
    gather_next(2)
    lax.fori_loop(tm // 2, tm, weighted, 0, unroll=4)
    gather_next(3)
    f = f + jnp.concatenate([_from_row_tiles(acc_ref, c, lead=(0,)), _from_row_tiles(acc_ref, c, lead=(1,))], axis=1)
    o_ref[...] = x_ref[...] + gf_ref[0] * _rms(f, g3_ref[...])


def _combine(lay, dest_flat, gate_flat, hp, x_mid, mods, g3, s_gate, s_up, s_down, ys, lat_only, compact_out):
    row = lambda i: (lay.tile(i, lat_only), 0)
    full = lambda shape: pl.BlockSpec(shape, lambda i: (0, 0))
    out_rows = lay.n_tiles(lat_only) * TM if compact_out else lay.rows
    out_map = (lambda i: (i, 0)) if compact_out else row
    n_steps = lay.n_tiles(lat_only)
    flat = lambda f: pl.BlockSpec((1, 1, TM * TOP_K), lambda i: (lay.tile(f(i), lat_only), 0, 0),
                                  memory_space=pltpu.SMEM)
    return pl.pallas_call(
        _combine_kernel,
        grid=(n_steps,),
        in_specs=[flat(lambda i: i), flat(lambda i: jnp.minimum(i + 1, n_steps - 1)), flat(lambda i: i),
                  pl.BlockSpec((TM * PACKED_TILES, LANES), row),
                  pl.BlockSpec((TM, D_MODEL), row), _mod_spec(lay, 5, lat_only),
                  full((1, D_MODEL)), full((D_MODEL, EXPERT_HIDDEN)), full((D_MODEL, EXPERT_HIDDEN)),
                  full((EXPERT_HIDDEN, D_MODEL)), pl.BlockSpec(memory_space=pl.ANY)],
        out_specs=pl.BlockSpec((TM, D_MODEL), out_map),
        scratch_shapes=[pltpu.VMEM((2, TOP_K, TM * PACKED_TILES, LANES), U32),
                        pltpu.VMEM((2, TM * PACKED_TILES, LANES), F32), pltpu.SemaphoreType.DMA((2,))],
        out_shape=jax.ShapeDtypeStruct((out_rows, D_MODEL), F32),
        compiler_params=_params(("arbitrary",)),
        name="combine",
    )(dest_flat, dest_flat, gate_flat, hp, x_mid, mods, g3, s_gate, s_up, s_down, ys)


def _rope_tables(lay):
    n_freq = HEAD_DIM // 4
    inv_freq = ROPE_THETA ** (-jnp.arange(n_freq, dtype=F32) / n_freq)
    t = jnp.arange(lay.n_lat)
    pos = jnp.stack([(t // GRID_W).astype(F32), (t % GRID_W).astype(F32)], axis=-1)
    ang = pos[:, :, None, None] * inv_freq
    ang = jnp.broadcast_to(ang, (lay.n_lat, 2, 2, n_freq)).reshape(lay.n_lat, HEAD_DIM)
    sign = jnp.tile(jnp.concatenate([-jnp.ones(n_freq, F32), jnp.ones(n_freq, F32)]), 2)
    cos = jnp.concatenate([jnp.ones((lay.n_ctx, HEAD_DIM), F32), jnp.cos(ang)], axis=0)
    sin = jnp.concatenate([jnp.zeros((lay.n_ctx, HEAD_DIM), F32), jnp.sin(ang) * sign], axis=0)
    return cos, sin


def _moe_plan(counts, n_blocks):
    counts = counts.reshape(-1)
    padded = (counts + MOE_BLK - 1) // MOE_BLK * MOE_BLK
    pad_end = jnp.cumsum(padded)
    pad_start = (pad_end - padded).astype(I32)
    blk_row = jnp.arange(n_blocks, dtype=I32) * MOE_BLK
    block_e = jnp.minimum(jnp.sum(blk_row[:, None] >= pad_end[None, :], axis=1), N_EXPERTS - 1).astype(I32)
    own = block_e[:, None] == jnp.arange(N_EXPERTS, dtype=I32)[None, :]
    count_b = jnp.sum(jnp.where(own, counts[None, :], 0), axis=1)
    start_b = jnp.sum(jnp.where(own, pad_start[None, :], 0), axis=1)
    block_valid = jnp.clip(count_b - (blk_row - start_b), 0, MOE_BLK).astype(I32)
    n_used = jnp.maximum(pad_end[-1:] // MOE_BLK, 1).astype(I32)
    return pad_start, block_e, block_valid, n_used


def kernel(x, c, ctx, c_ctx, mod_w, mod_b, norm_g, w_in, w_out, conv_w, conv_b, lru_wr, lru_br, lru_wi, lru_bi,
           lru_lambda, gla_w2, gla_b2, gla_norm_g, qk_norm_g, win_sink, router_w, router_bias, exp_w_gate,
           exp_w_up, exp_w_down, sh_w_gate, sh_w_up, sh_w_down):
    bsz, n_lat, d = x.shape
    n_ctx = ctx.shape[1]
    depth = mod_w.shape[0]
    assert d == D_MODEL and bsz < MOD_ROWS
    lay = _Layout(bsz, n_lat, n_ctx)

    cc = jnp.zeros((MOD_ROWS, d), F32).at[:bsz].set(c).at[bsz].set(c_ctx)
    mods_all = _adaln(cc, mod_w, mod_b)
    cos, sin = _rope_tables(lay)
    xt = None

    splits = np.cumsum([0, W_A, W_B_RAW, W_C, W_D])
    for layer in range(depth):
        last = layer == depth - 1
        need_ctx = not last
        lat_only = last
        mods = mods_all[layer].reshape(MOD_ROWS * 6, 1, d)
        g = norm_g[layer]
        wl = w_in[layer]
        w_cat = jnp.concatenate(
            [wl[:, splits[0]:splits[1]], wl[:, splits[1]:splits[2]], jnp.zeros((d, W_B - W_B_RAW), F32),
             wl[:, splits[2]:splits[3]], wl[:, splits[3]:splits[4]]], axis=1).astype(BF16)
        if xt is None:
            p_a, p_b, p_c, p_d, xt = _inproj(lay, (x.reshape(bsz * n_lat, d), ctx.reshape(bsz * n_ctx, d)), mods,
                                             g[0:1], w_cat)
        else:
            p_a, p_b, p_c, p_d = _inproj(lay, (xt,), mods, g[0:1], w_cat)

        y_a = _rglru(lay, p_a, conv_w[layer], conv_b[layer], lru_wr[layer], lru_br[layer], lru_wi[layer],
                     lru_bi[layer], lru_lambda[layer])
        w2pad = jnp.zeros((2, LANES, GLA_HEADS * GLA_DK), F32)
        w2pad = w2pad.at[0, :GLA_RANK].set(gla_w2[layer, 0]).at[1, GLA_RANK:2 * GLA_RANK].set(gla_w2[layer, 1])
        y_b = _gla(lay, p_b, w2pad, gla_b2[layer], gla_norm_g[layer])
        y_c = _attention(lay, p_c, cos, sin, qk_norm_g[layer], window=False, need_ctx=need_ctx)
        y_d = _attention(lay, p_d, cos, sin, win_sink[layer], window=True, need_ctx=need_ctx)

        x_mid, h, hp = _outproj(lay, (y_a, y_b, y_c, y_d), w_out[layer].astype(BF16), xt, mods, g[1:2], g[2:3],
                                lat_only)
        top_e, gate, rank, counts = _router(lay, h, router_w[layer].T, router_bias[layer], lat_only)
        n_assign = lay.n_tiles(lat_only) * TM * TOP_K
        n_blocks = -(-(n_assign + N_EXPERTS * (MOE_BLK - 1)) // MOE_BLK)
        pad_start, block_e, block_valid, n_used = _moe_plan(counts, n_blocks)
        owner = top_e[:, :, None] == jnp.arange(N_EXPERTS, dtype=I32)
        dest = jnp.sum(jnp.where(owner, pad_start, 0), axis=-1) + rank
        dest_flat = (dest * PACKED_TILES).T.reshape(lay.rows // TM, 1, TM * TOP_K)
        xs = _dispatch(lay, dest_flat, hp, n_blocks * MOE_BLK, lat_only)
        ys = _experts(block_e, block_valid, n_used, xs, exp_w_gate, exp_w_up, exp_w_down, layer)
        gate_flat = gate.T.reshape(lay.rows // TM, 1, TM * TOP_K)
        xt = _combine(lay, dest_flat, gate_flat, hp, x_mid, mods, g[3:4], sh_w_gate[layer].astype(BF16),
                      sh_w_up[layer].astype(BF16), sh_w_down[layer].astype(BF16), ys, lat_only, compact_out=last)
    return xt.reshape(bsz, n_lat, d)
```

```python
import functools

import numpy as np
import jax
import jax.numpy as jnp
from jax import lax
from jax.experimental import pallas as pl
from jax.experimental.pallas import tpu as pltpu

F32 = jnp.float32
BF16 = jnp.bfloat16
I32 = jnp.int32
U32 = jnp.uint32
HIGHEST = lax.Precision.HIGHEST
NT_DIMS = (((1,), (1,)), ((), ()))
TN_DIMS = (((0,), (0,)), ((), ()))

LANES = 128
SUBLANES = 8
VMEM_LIMIT_BYTES = 56 * 1024 * 1024

D_MODEL = 2048
GRID_W = 64
HEAD_DIM = 128
GROUP_WIDTH = D_MODEL // 4
LRU_BLOCKS = 4
CONV_W = 4
RGLRU_C = 8.0
GLA_HEADS = 4
GLA_DK = 64
GLA_DV = 128
GLA_RANK = 16
GLA_TAU = 16.0
GLA_CHUNK = 64
GLA_HEADS_PER_STEP = 4
KV_HEADS = 2
WINDOW = 128
N_EXPERTS = 64
N_GROUPS = 8
TOPK_GROUPS = 4
TOP_K = 8
EXPERT_HIDDEN = 512
ROUTED_SCALE = 2.5
RMS_EPS = 1e-6
ROPE_THETA = 10000.0
NEG_INF = -1e30

W_A = 2 * GROUP_WIDTH
W_B_RAW = 2 * GLA_HEADS * GLA_DK + 2 * GLA_HEADS * GLA_DV + 2 * GLA_RANK
W_B = 1664
W_C = GROUP_WIDTH + 2 * KV_HEADS * HEAD_DIM
W_D = W_C
Z_COL_BLOCK = (2 * GLA_HEADS * GLA_DK + 2 * GLA_HEADS * GLA_DV) // LANES

TM = 256
MOE_BLK = 512
PACKED_TILES = D_MODEL // 2 // LANES
MOD_ROWS = 16


def _params(sem, vmem=VMEM_LIMIT_BYTES):
    return pltpu.CompilerParams(dimension_semantics=sem, vmem_limit_bytes=vmem)


def _rms(x, g):
    return x * lax.rsqrt(jnp.mean(x * x, axis=-1, keepdims=True) + RMS_EPS) * g


def _silu(x):
    return x * jax.nn.sigmoid(x)


def _pack_bf16_pairs(h):
    n = h.shape[1] // 2
    hb = h.astype(BF16).astype(F32)
    hi = lax.bitcast_convert_type(hb[:, :n], U32)
    lo = lax.bitcast_convert_type(hb[:, n:], U32)
    return hi | (lo >> 16)


def _to_row_tiles(ref, x):
    m = x.shape[0]
    c = x.shape[1] // LANES
    for j in range(c):
        ref[pl.ds(j, m, stride=c), :] = x[:, j * LANES:(j + 1) * LANES]


def _from_row_tiles(ref, c, lead=()):
    m = ref.shape[-2] // c
    return jnp.concatenate([ref[lead + (pl.ds(j, m, stride=c), slice(None))] for j in range(c)], axis=1)


def _unpack_bf16_pairs(p):
    hi = lax.bitcast_convert_type(p & jnp.uint32(0xFFFF0000), F32)
    lo = lax.bitcast_convert_type(p << 16, F32)
    return jnp.concatenate([hi, lo], axis=1).astype(BF16)


def _adaln_kernel(c_ref, w_ref, b_ref, o_ref):
    a = _silu(c_ref[...])
    o_ref[0] = jnp.dot(a, w_ref[0], preferred_element_type=F32, precision=HIGHEST) + b_ref[0]


def _adaln(cc, mod_w, mod_b):
    depth, d, n = mod_w.shape
    tn = 1024
    return pl.pallas_call(
        _adaln_kernel,
        grid=(depth, n // tn),
        in_specs=[pl.BlockSpec((MOD_ROWS, d), lambda l, j: (0, 0)),
                  pl.BlockSpec((1, d, tn), lambda l, j: (l, 0, j)),
                  pl.BlockSpec((1, 1, tn), lambda l, j: (l, 0, j))],
        out_specs=pl.BlockSpec((1, MOD_ROWS, tn), lambda l, j: (l, 0, j)),
        out_shape=jax.ShapeDtypeStruct((depth, MOD_ROWS, n), F32),
        compiler_params=_params(("arbitrary", "arbitrary")),
        name="adaln",
    )(cc, mod_w, mod_b.reshape(depth, 1, n))


class _Layout:
    def __init__(self, bsz, n_lat, n_ctx):
        self.bsz, self.n_lat, self.n_ctx = bsz, n_lat, n_ctx
        self.seq = n_ctx + n_lat
        self.rows = bsz * self.seq
        assert n_ctx % TM == 0 and n_lat % TM == 0
        self.tiles_b = self.seq // TM
        self.ctx_tiles = n_ctx // TM
        self.lat_tiles = n_lat // TM

    def n_tiles(self, lat_only):
        return self.bsz * (self.lat_tiles if lat_only else self.tiles_b)

    def tile(self, i, lat_only):
        if lat_only:
            return (i // self.lat_tiles) * self.tiles_b + self.ctx_tiles + i % self.lat_tiles
        return i

    def mod_row(self, i, lat_only):
        if lat_only:
            return i // self.lat_tiles
        return jnp.where(i % self.tiles_b < self.ctx_tiles, self.bsz, i // self.tiles_b)


def _mod_spec(lay, chunk, lat_only):
    return pl.BlockSpec((1, 1, D_MODEL), lambda i: (lay.mod_row(i, lat_only) * 6 + chunk, 0, 0))


def _inproj_kernel(*refs, tiles_b, ctx_tiles):
    if ctx_tiles:
        x_ref, c_ref, sh_ref, sc_ref, g_ref, w_ref, oa_ref, ob_ref, oc_ref, od_ref, xt_ref = refs
        x = jnp.where(pl.program_id(0) % tiles_b < ctx_tiles, c_ref[...], x_ref[...])
        xt_ref[...] = x
    else:
        x_ref, sh_ref, sc_ref, g_ref, w_ref, oa_ref, ob_ref, oc_ref, od_ref = refs
        x = x_ref[...]
    h = _rms(x, g_ref[...]) * (1.0 + sc_ref[0]) + sh_ref[0]
    hb = h.astype(BF16)
    c0 = 0
    for o_ref in (oa_ref, ob_ref, oc_ref, od_ref):
        width = o_ref.shape[1]
        o_ref[...] = jnp.dot(hb, w_ref[:, c0:c0 + width], preferred_element_type=F32)
        c0 += width


def _inproj(lay, xs, mods, g0, w_cat):
    widths = (W_A, W_B, W_C, W_D)
    row = lambda i: (i, 0)
    split = len(xs) == 2
    x_specs = [pl.BlockSpec((TM, D_MODEL), row)]
    out_specs = [pl.BlockSpec((TM, w), row) for w in widths]
    out_shape = [jax.ShapeDtypeStruct((lay.rows, w), F32) for w in widths]
    if split:
        lat = lambda i: ((i // lay.tiles_b) * lay.lat_tiles + jnp.maximum(i % lay.tiles_b - lay.ctx_tiles, 0), 0)
        ctx = lambda i: ((i // lay.tiles_b) * lay.ctx_tiles + jnp.minimum(i % lay.tiles_b, lay.ctx_tiles - 1), 0)
        x_specs = [pl.BlockSpec((TM, D_MODEL), lat), pl.BlockSpec((TM, D_MODEL), ctx)]
        out_specs.append(pl.BlockSpec((TM, D_MODEL), row))
        out_shape.append(jax.ShapeDtypeStruct((lay.rows, D_MODEL), F32))
    return pl.pallas_call(
        functools.partial(_inproj_kernel, tiles_b=lay.tiles_b, ctx_tiles=lay.ctx_tiles if split else 0),
        grid=(lay.n_tiles(False),),
        in_specs=x_specs + [_mod_spec(lay, 0, False), _mod_spec(lay, 1, False),
                            pl.BlockSpec((1, D_MODEL), lambda i: (0, 0)),
                            pl.BlockSpec((D_MODEL, sum(widths)), lambda i: (0, 0), pipeline_mode=pl.Buffered(1))],
        out_specs=out_specs,
        out_shape=out_shape,
        compiler_params=_params(("arbitrary",)),
        name="inproj",
    )(*xs, mods, mods, g0, w_cat)


def _rglru_kernel(x_ref, g_ref, cw_ref, cb_ref, wr_ref, br_ref, wi_ref, bi_ref, lam_ref, o_ref,
                  xs_ref, a_ref, b_ref, h_ref, *, n_ctx):
    seq = x_ref.shape[0]
    pad = SUBLANES
    n_tiles = seq // SUBLANES
    ctx_tiles = n_ctx // SUBLANES
    row = lax.broadcasted_iota(I32, (seq, LANES), 0)

    xs_ref[0:pad, :] = jnp.zeros((pad, LANES), F32)
    xs_ref[pad + seq:2 * pad + seq, :] = jnp.zeros((pad, LANES), F32)
    xs_ref[pad:pad + seq, :] = x_ref[...]
    u = jnp.zeros((seq, LANES), F32) + cb_ref[...]
    for j in range(CONV_W):
        off = j - 2
        xsh = xs_ref[pad + off:pad + off + seq, :]
        if off < 0:
            xsh = jnp.where((row >= n_ctx) & (row + off < n_ctx), 0.0, xsh)
        elif off > 0:
            xsh = jnp.where((row < n_ctx) & (row + off >= n_ctx), 0.0, xsh)
        u = u + cw_ref[j:j + 1, :] * xsh

    ub = u.astype(BF16)
    tile_shape = (n_tiles, SUBLANES, LANES)
    rmod = lax.broadcasted_iota(I32, tile_shape, 1)
    for d in range(2):
        reverse = d == 1
        r = jax.nn.sigmoid(jnp.dot(ub, wr_ref[d, 0].astype(BF16), preferred_element_type=F32) + br_ref[d:d + 1, :])
        gi = jax.nn.sigmoid(jnp.dot(ub, wi_ref[d, 0].astype(BF16), preferred_element_type=F32) + bi_ref[d:d + 1, :])
        lam = lam_ref[d:d + 1, :]
        softplus_neg = jnp.maximum(-lam, 0.0) + jnp.log(1.0 + jnp.exp(-jnp.abs(lam)))
        a = jnp.exp(-RGLRU_C * r * softplus_neg)
        b = jnp.sqrt(1.0 - a * a) * (gi * u)
        a = a.reshape(tile_shape)
        b = b.reshape(tile_shape)
        for s in (1, 2, 4):
            if reverse:
                ap, bp, ok = pltpu.roll(a, SUBLANES - s, 1), pltpu.roll(b, SUBLANES - s, 1), rmod < SUBLANES - s
            else:
                ap, bp, ok = pltpu.roll(a, s, 1), pltpu.roll(b, s, 1), rmod >= s
            b = jnp.where(ok, a * bp + b, b)
            a = jnp.where(ok, a * ap, a)
        a_ref[...] = a.reshape(seq, LANES)
        b_ref[...] = b.reshape(seq, LANES)

        def step(t, h, reverse=reverse):
            r0 = pl.multiple_of(t * SUBLANES, SUBLANES)
            h8 = a_ref[pl.ds(r0, SUBLANES), :] * h + b_ref[pl.ds(r0, SUBLANES), :]
            if reverse:
                h_ref[pl.ds(r0, SUBLANES), :] = h_ref[pl.ds(r0, SUBLANES), :] + h8
                return jnp.broadcast_to(h8[0:1, :], (SUBLANES, LANES))
            h_ref[pl.ds(r0, SUBLANES), :] = h8
            return jnp.broadcast_to(h8[SUBLANES - 1:SUBLANES, :], (SUBLANES, LANES))

        h0 = jnp.zeros((SUBLANES, LANES), F32)
        if reverse:
            hc = lax.fori_loop(0, ctx_tiles, lambda i, h: step(ctx_tiles - 1 - i, h), h0)
            lax.fori_loop(0, n_tiles - ctx_tiles, lambda i, h: step(n_tiles - 1 - i, h), hc)
        else:
            lax.fori_loop(0, n_tiles, step, h0)

    g = g_ref[...]
    gelu = 0.5 * g * (1.0 + jnp.tanh(np.sqrt(2.0 / np.pi) * (g + 0.044715 * (g * g * g))))
    o_ref[...] = h_ref[...] * gelu


def _rglru(lay, p_a, conv_w, conv_b, wr, br, wi, bi, lam):
    seq = lay.seq
    nb = LRU_BLOCKS
    col = lambda off: (lambda b, j: (b, off + j))
    vec = lambda rows: pl.BlockSpec((rows, LANES), lambda b, j: (0, j))
    wspec = pl.BlockSpec((2, 1, LANES, LANES), lambda b, j: (0, j, 0, 0))
    return pl.pallas_call(
        functools.partial(_rglru_kernel, n_ctx=lay.n_ctx),
        grid=(lay.bsz, nb),
        in_specs=[pl.BlockSpec((seq, LANES), col(0)), pl.BlockSpec((seq, LANES), col(nb)),
                  vec(CONV_W), vec(1), wspec, vec(2), wspec, vec(2), vec(2)],
        out_specs=pl.BlockSpec((seq, LANES), col(0)),
        out_shape=jax.ShapeDtypeStruct((lay.rows, GROUP_WIDTH), F32),
        scratch_shapes=[pltpu.VMEM((seq + 2 * SUBLANES, LANES), F32), pltpu.VMEM((seq, LANES), F32),
                        pltpu.VMEM((seq, LANES), F32), pltpu.VMEM((seq, LANES), F32)],
        compiler_params=_params(("arbitrary", "arbitrary")),
        name="rglru",
    )(p_a, p_a, conv_w, conv_b.reshape(1, -1), wr, br, wi, bi, lam)


def _gla_kernel(q_ref, k_ref, v_ref, g_ref, z_ref, w2_ref, b2_ref, ng_ref, o_ref,
                laf_ref, lab_ref, of_ref, ob_ref, *, n_ctx):
    seq = q_ref.shape[0]
    nh = q_ref.shape[1] // GLA_DK
    qk_w = nh * GLA_DK
    cs = GLA_CHUNK
    n_chunks = seq // cs
    ctx_chunks = n_ctx // cs
    z = z_ref[...]
    for d, la_ref in ((0, laf_ref), (1, lab_ref)):
        pre = jnp.dot(z, w2_ref[d], preferred_element_type=F32, precision=HIGHEST) + b2_ref[d:d + 1, :]
        log_sig = jnp.minimum(pre, 0.0) - jnp.log(1.0 + jnp.exp(-jnp.abs(pre)))
        la_ref[...] = log_sig * (1.0 / GLA_TAU)

    ri = lax.broadcasted_iota(I32, (cs, cs), 0)
    ci = lax.broadcasted_iota(I32, (cs, cs), 1)
    scale = GLA_DK ** -0.5

    def chunk(c, states, reverse):
        r0 = pl.multiple_of(c * cs, cs)
        rows = pl.ds(r0, cs)
        keep = (ri <= ci) if reverse else (ri >= ci)
        la = (lab_ref if reverse else laf_ref)[rows, :]
        la_hi = la.astype(BF16)
        rest = la - la_hi.astype(F32)
        la_mid = rest.astype(BF16)
        la_lo = (rest - la_mid.astype(F32)).astype(BF16)
        parts = jnp.dot(keep.astype(BF16), jnp.concatenate([la_hi, la_mid, la_lo], axis=1),
                        preferred_element_type=F32)
        cum = parts[:, 0:qk_w] + parts[:, qk_w:2 * qk_w] + parts[:, 2 * qk_w:3 * qk_w]
        last = cum[0:1, :] if reverse else cum[cs - 1:cs, :]
        q = q_ref[rows, :] * scale
        k = k_ref[rows, :]
        qe = q * jnp.exp(cum)
        ke = k * jnp.exp(-cum)
        kl = k * jnp.exp(last - cum)
        el = jnp.exp(last)
        out_ref = ob_ref if reverse else of_ref
        new_states = []
        for hh in range(nh):
            dk = slice(hh * GLA_DK, (hh + 1) * GLA_DK)
            dv = slice(hh * GLA_DV, (hh + 1) * GLA_DV)
            qh = qe[:, dk].astype(BF16)
            vh = v_ref[rows, dv].astype(BF16)
            st = states[hh]
            s = lax.dot_general(qh, ke[:, dk].astype(BF16), NT_DIMS, preferred_element_type=F32)
            s = jnp.where(keep, s, 0.0)
            o = (jnp.dot(s.astype(BF16), vh, preferred_element_type=F32)
                 + lax.dot_general(qh, st.astype(BF16), NT_DIMS, preferred_element_type=F32))
            out_ref[rows, dv] = o
            new_states.append(st * el[:, dk]
                              + lax.dot_general(vh, kl[:, dk].astype(BF16), TN_DIMS, preferred_element_type=F32))
        return tuple(new_states)

    def body(i, carry):
        sf, sb = carry
        sf = chunk(i, sf, False)
        cb = jnp.where(i < ctx_chunks, ctx_chunks - 1 - i, n_chunks - 1 + ctx_chunks - i)
        sb = chunk(cb, sb, True)
        return sf, sb

    zero = (jnp.zeros((GLA_DV, GLA_DK), F32),) * nh
    lax.fori_loop(0, n_chunks, body, (zero, zero), unroll=4)

    ng = ng_ref[...]
    for hh in range(nh):
        dv = slice(hh * GLA_DV, (hh + 1) * GLA_DV)
        o = of_ref[:, dv] + ob_ref[:, dv]
        o_ref[:, dv] = _rms(o, ng) * _silu(g_ref[:, dv])


def _gla(lay, p_b, w2pad, b2, norm_g):
    seq = lay.seq
    hp = GLA_HEADS // GLA_HEADS_PER_STEP
    qk_w = GLA_HEADS_PER_STEP * GLA_DK
    v_w = GLA_HEADS_PER_STEP * GLA_DV
    k_off = GLA_HEADS * GLA_DK // qk_w
    v_off = 2 * GLA_HEADS * GLA_DK // v_w
    g_off = v_off + GLA_HEADS * GLA_DV // v_w
    once = pl.Buffered(1)
    blk = lambda w, off, mode=None: pl.BlockSpec((seq, w), lambda b, j: (b, off + j), pipeline_mode=mode)
    return pl.pallas_call(
        functools.partial(_gla_kernel, n_ctx=lay.n_ctx),
        grid=(lay.bsz, hp),
        in_specs=[blk(qk_w, 0, once), blk(qk_w, k_off, once), blk(v_w, v_off, once), blk(v_w, g_off, once),
                  pl.BlockSpec((seq, LANES), lambda b, j: (b, Z_COL_BLOCK), pipeline_mode=once),
                  pl.BlockSpec((2, LANES, qk_w), lambda b, j: (0, 0, j)),
                  pl.BlockSpec((2, qk_w), lambda b, j: (0, j)),
                  pl.BlockSpec((1, GLA_DV), lambda b, j: (0, 0))],
        out_specs=blk(v_w, 0),
        out_shape=jax.ShapeDtypeStruct((lay.rows, GROUP_WIDTH), F32),
        scratch_shapes=[pltpu.VMEM((seq, qk_w), F32), pltpu.VMEM((seq, qk_w), F32),
                        pltpu.VMEM((seq, v_w), F32), pltpu.VMEM((seq, v_w), F32)],
        compiler_params=_params(("arbitrary", "arbitrary")),
        name="gla",
    )(p_b, p_b, p_b, p_b, p_b, w2pad, b2, norm_g.reshape(1, -1))


def _rope(x, cos, sin_signed):
    lane = lax.broadcasted_iota(I32, x.shape, 1)
    partner = jnp.where((lane & 32) != 0, pltpu.roll(x, 32, 1), pltpu.roll(x, HEAD_DIM - 32, 1))
    return x * cos + partner * sin_signed


def _attn_kernel(*refs, window, need_ctx, tq, n_ctx):
    if window:
        sink_ref, q_ref, k_ref, v_ref, cos_ref, sin_ref, o_ref, ks_ref, vs_ref = refs
        qkg_ref = None
    else:
        q_ref, k_ref, v_ref, cos_ref, sin_ref, qkg_ref, o_ref, ks_ref, vs_ref = refs
    seq = k_ref.shape[0]
    n_lat = seq - n_ctx
    g = pl.program_id(1)
    qi = pl.program_id(2)
    jq = qi if need_ctx else qi + n_ctx // tq

    @pl.when(qi == 0)
    def _():
        k = k_ref[...]
        if not window:
            k = _rms(k, qkg_ref[1:2, :])
        ks_ref[...] = _rope(k, cos_ref[...], sin_ref[...]).T.astype(BF16)
        vs_ref[...] = v_ref[...].astype(BF16)

    r0 = pl.multiple_of(jq * tq, tq)
    cos = cos_ref[pl.ds(r0, tq), :]
    sin = sin_ref[pl.ds(r0, tq), :]
    heads = []
    for hh in range(2):
        qh = q_ref[:, hh * HEAD_DIM:(hh + 1) * HEAD_DIM]
        if not window:
            qh = _rms(qh, qkg_ref[0:1, :])
        heads.append((_rope(qh, cos, sin) * HEAD_DIM ** -0.5).astype(BF16))
    q2 = jnp.concatenate(heads, axis=0)

    def finish(o):
        o_ref[:, 0:HEAD_DIM] = o[0:tq]
        o_ref[:, HEAD_DIM:2 * HEAD_DIM] = o[tq:2 * tq]

    def scores(lo, n):
        return jnp.dot(q2, ks_ref[:, pl.ds(lo, n)], preferred_element_type=F32)

    def pv(p, lo, n):
        return jnp.dot(p.astype(BF16), vs_ref[pl.ds(lo, n), :], preferred_element_type=F32)

    if window:
        rowi = lax.broadcasted_iota(I32, (2 * tq, 1), 0)
        sink = jnp.where(rowi < tq, sink_ref[2 * g], sink_ref[2 * g + 1])

    def ctx_tile():
        s = scores(0, n_ctx)
        m = jnp.max(s, axis=-1, keepdims=True)
        if window:
            m = jnp.maximum(m, sink)
        p = jnp.exp(s - m)
        l = jnp.sum(p, axis=-1, keepdims=True)
        if window:
            l = l + jnp.exp(sink - m)
        finish(pv(p, 0, n_ctx) / l)

    def lat_tile():
        if not window:
            s = scores(0, seq)
            m = jnp.max(s, axis=-1, keepdims=True)
            p = jnp.exp(s - m)
            finish(pv(p, 0, seq) / jnp.sum(p, axis=-1, keepdims=True))
            return
        band = tq + 2 * WINDOW
        jl = jq - n_ctx // tq
        start = jnp.clip(jl * tq - WINDOW, 0, n_lat - band)
        lo = pl.multiple_of(n_ctx + start, WINDOW)
        sb = scores(lo, band)
        q_pos = jl * tq + lax.broadcasted_iota(I32, (2 * tq, band), 0) % tq
        k_pos = start + lax.broadcasted_iota(I32, (2 * tq, band), 1)
        sb = jnp.where(jnp.abs(q_pos - k_pos) <= WINDOW, sb, NEG_INF)
        sc = scores(0, n_ctx)
        m = jnp.maximum(jnp.maximum(jnp.max(sb, axis=-1, keepdims=True), jnp.max(sc, axis=-1, keepdims=True)), sink)
        pb = jnp.exp(sb - m)
        pc = jnp.exp(sc - m)
        l = jnp.sum(pb, axis=-1, keepdims=True) + jnp.sum(pc, axis=-1, keepdims=True) + jnp.exp(sink - m)
        finish((pv(pb, lo, band) + pv(pc, 0, n_ctx)) / l)

    if need_ctx:
        pl.when(jq * tq < n_ctx)(ctx_tile)
        pl.when(jq * tq >= n_ctx)(lat_tile)
    else:
        lat_tile()


def _attention(lay, p, cos, sin, extra, *, window, need_ctx):
    seq = lay.seq
    tq = TM
    q_tiles_b = seq // tq
    nq = q_tiles_b if need_ctx else lay.n_lat // tq
    skip = 0 if need_ctx else lay.n_ctx // tq
    q_heads_w = 2 * HEAD_DIM
    k_off = GROUP_WIDTH // HEAD_DIM
    v_off = k_off + KV_HEADS

    def im(f):
        return (lambda b, g, qi, *_: f(b, g, qi))

    in_specs = [pl.BlockSpec((tq, q_heads_w), im(lambda b, g, qi: (b * q_tiles_b + skip + qi, g))),
                pl.BlockSpec((seq, HEAD_DIM), im(lambda b, g, qi: (b, k_off + g))),
                pl.BlockSpec((seq, HEAD_DIM), im(lambda b, g, qi: (b, v_off + g))),
                pl.BlockSpec((seq, HEAD_DIM), im(lambda b, g, qi: (0, 0))),
                pl.BlockSpec((seq, HEAD_DIM), im(lambda b, g, qi: (0, 0)))]
    args = [p, p, p, cos, sin]
    if not window:
        in_specs.append(pl.BlockSpec((2, HEAD_DIM), im(lambda b, g, qi: (0, 0))))
        args.append(extra)
    out_spec = pl.BlockSpec((tq, q_heads_w), im(lambda b, g, qi: (b * q_tiles_b + skip + qi, g)))
    kern = functools.partial(_attn_kernel, window=window, need_ctx=need_ctx, tq=tq, n_ctx=lay.n_ctx)
    scratch = [pltpu.VMEM((HEAD_DIM, seq), BF16), pltpu.VMEM((seq, HEAD_DIM), BF16)]
    out_shape = jax.ShapeDtypeStruct((lay.rows, GROUP_WIDTH), F32)
    sem = ("arbitrary", "arbitrary", "arbitrary")
    grid = (lay.bsz, KV_HEADS, nq)
    if window:
        return pl.pallas_call(
            kern,
            grid_spec=pltpu.PrefetchScalarGridSpec(num_scalar_prefetch=1, grid=grid, in_specs=in_specs,
                                                   out_specs=out_spec, scratch_shapes=scratch),
            out_shape=out_shape, compiler_params=_params(sem), name="win_attn",
        )(extra, *args)
    return pl.pallas_call(kern, grid=grid, in_specs=in_specs, out_specs=out_spec, scratch_shapes=scratch,
                          out_shape=out_shape, compiler_params=_params(sem), name="glob_attn")(*args)


def _outproj_kernel(ya_ref, yb_ref, yc_ref, yd_ref, w_ref, x_ref, gm_ref, sh_ref, sc_ref, g1_ref, g2_ref,
                    xo_ref, h_ref, hp_ref):
    acc = None
    for i, y_ref in enumerate((ya_ref, yb_ref, yc_ref, yd_ref)):
        part = jnp.dot(y_ref[...].astype(BF16), w_ref[i * GROUP_WIDTH:(i + 1) * GROUP_WIDTH, :],
                       preferred_element_type=F32)
        acc = part if acc is None else acc + part
    x_new = x_ref[...] + gm_ref[0] * _rms(acc, g1_ref[...])
    xo_ref[...] = x_new
    h = _rms(x_new, g2_ref[...]) * (1.0 + sc_ref[0]) + sh_ref[0]
    h_ref[...] = h
    _to_row_tiles(hp_ref, _pack_bf16_pairs(h))


def _outproj(lay, ys, w_out, xt, mods, g1, g2, lat_only):
    row = lambda i: (lay.tile(i, lat_only), 0)
    full = lambda shape: pl.BlockSpec(shape, lambda i: (0, 0))
    yspec = pl.BlockSpec((TM, GROUP_WIDTH), row)
    xspec = pl.BlockSpec((TM, D_MODEL), row)
    return pl.pallas_call(
        _outproj_kernel,
        grid=(lay.n_tiles(lat_only),),
        in_specs=[yspec, yspec, yspec, yspec, full((D_MODEL, D_MODEL)), xspec,
                  _mod_spec(lay, 2, lat_only), _mod_spec(lay, 3, lat_only), _mod_spec(lay, 4, lat_only),
                  full((1, D_MODEL)), full((1, D_MODEL))],
        out_specs=[xspec, xspec, pl.BlockSpec((TM * PACKED_TILES, LANES), row)],
        out_shape=[jax.ShapeDtypeStruct((lay.rows, D_MODEL), F32), jax.ShapeDtypeStruct((lay.rows, D_MODEL), F32),
                   jax.ShapeDtypeStruct((lay.rows * PACKED_TILES, LANES), U32)],
        compiler_params=_params(("arbitrary",)),
        name="outproj",
    )(*ys, w_out, xt, mods, mods, mods, g1, g2)


def _route(h, wt_ref, bias_ref, e_ref, gate_ref, rank_ref, cnt_ref, carry_ref):
    i = pl.program_id(0)
    tm = h.shape[0]
    ne = N_EXPERTS
    gsz = ne // N_GROUPS

    @pl.when(i == 0)
    def _():
        carry_ref[...] = jnp.zeros_like(carry_ref)

    logits = lax.dot_general(wt_ref[...], h, NT_DIMS, preferred_element_type=F32, precision=HIGHEST)
    scores = jax.nn.sigmoid(logits)
    biased = scores + bias_ref[...]

    sub = lax.broadcasted_iota(I32, (gsz, tm), 0)
    gs_rows = []
    for grp in range(N_GROUPS):
        blk = biased[grp * gsz:(grp + 1) * gsz, :]
        m1 = jnp.max(blk, axis=0, keepdims=True)
        first = jnp.min(jnp.where(blk == m1, sub, gsz), axis=0, keepdims=True)
        m2 = jnp.max(jnp.where(sub == first, -jnp.inf, blk), axis=0, keepdims=True)
        gs_rows.append(m1 + m2)
    gs = jnp.concatenate(gs_rows, axis=0)

    gidx = lax.broadcasted_iota(I32, (N_GROUPS, tm), 0)
    chosen = jnp.zeros((N_GROUPS, tm), jnp.bool_)
    cur = gs
    for _ in range(TOPK_GROUPS):
        m = jnp.max(cur, axis=0, keepdims=True)
        first = jnp.min(jnp.where(cur == m, gidx, N_GROUPS), axis=0, keepdims=True)
        sel = gidx == first
        chosen = chosen | sel
        cur = jnp.where(sel, -jnp.inf, cur)

    eidx = lax.broadcasted_iota(I32, (ne, tm), 0)
    eligible = jnp.concatenate(
        [jnp.broadcast_to(chosen[grp:grp + 1, :], (gsz, tm)) for grp in range(N_GROUPS)], axis=0)
    cur = jnp.where(eligible, biased, -jnp.inf)
    sels, gates = [], []
    onehot = jnp.zeros((ne, tm), jnp.bool_)
    for k in range(TOP_K):
        m = jnp.max(cur, axis=0, keepdims=True)
        first = jnp.min(jnp.where(cur == m, eidx, ne), axis=0, keepdims=True)
        sel = eidx == first
        e_ref[k:k + 1, :] = first
        gates.append(jnp.sum(jnp.where(sel, scores, 0.0), axis=0, keepdims=True))
        sels.append(sel)
        onehot = onehot | sel
        cur = jnp.where(sel, -jnp.inf, cur)
    total = gates[0]
    for gk in gates[1:]:
        total = total + gk
    for k in range(TOP_K):
        gate_ref[k:k + 1, :] = gates[k] / total * ROUTED_SCALE

    tr = lax.broadcasted_iota(I32, (tm, tm), 0)
    tc = lax.broadcasted_iota(I32, (tm, tm), 1)
    before = (tr < tc).astype(BF16)
    oh = onehot.astype(F32)
    prefix = jnp.dot(oh.astype(BF16), before, preferred_element_type=F32) + carry_ref[...]
    for k in range(TOP_K):
        rank_ref[k:k + 1, :] = jnp.sum(jnp.where(sels[k], prefix, 0.0), axis=0, keepdims=True).astype(I32)
    carry = carry_ref[...] + jnp.sum(oh, axis=1, keepdims=True)
    carry_ref[...] = carry
    cnt_ref[...] = carry.astype(I32)


def _router_kernel(h_ref, wt_ref, bias_ref, e_ref, gate_ref, rank_ref, cnt_ref, carry_ref):
    _route(h_ref[...], wt_ref, bias_ref, e_ref, gate_ref, rank_ref, cnt_ref, carry_ref)


def _router(lay, h, router_wt, bias, lat_only):
    col = lambda i: (0, lay.tile(i, lat_only))
    kt = pl.BlockSpec((TOP_K, TM), col)
    return pl.pallas_call(
        _router_kernel,
        grid=(lay.n_tiles(lat_only),),
        in_specs=[pl.BlockSpec((TM, D_MODEL), lambda i: (lay.tile(i, lat_only), 0)),
                  pl.BlockSpec((N_EXPERTS, D_MODEL), lambda i: (0, 0)),
                  pl.BlockSpec((N_EXPERTS, 1), lambda i: (0, 0))],
        out_specs=[kt, kt, kt, pl.BlockSpec((N_EXPERTS, 1), lambda i: (0, 0))],
        out_shape=[jax.ShapeDtypeStruct((TOP_K, lay.rows), I32), jax.ShapeDtypeStruct((TOP_K, lay.rows), F32),
                   jax.ShapeDtypeStruct((TOP_K, lay.rows), I32), jax.ShapeDtypeStruct((N_EXPERTS, 1), I32)],
        scratch_shapes=[pltpu.VMEM((N_EXPERTS, 1), F32)],
        compiler_params=_params(("arbitrary",)),
        name="router",
    )(h, router_wt, bias.reshape(-1, 1))


def _dispatch_kernel(dest_ref, hp_ref, xs_ref, sem):
    c = PACKED_TILES
    tm = hp_ref.shape[0] // c

    def issue(t, carry):
        src = hp_ref.at[pl.ds(pl.multiple_of(t * c, c), c)]
        for k in range(TOP_K):
            dest = pl.multiple_of(dest_ref[0, 0, t * TOP_K + k], c)
            pltpu.make_async_copy(src, xs_ref.at[pl.ds(dest, c)], sem).start(priority=k % 2)
        return carry

    lax.fori_loop(0, tm, issue, 0, unroll=2)
    for k in range(TOP_K):
        pltpu.make_async_copy(hp_ref, xs_ref.at[pl.ds(0, tm * c)], sem).wait()


def _dispatch(lay, dest_flat, hp, n_rows, lat_only):
    return pl.pallas_call(
        _dispatch_kernel,
        grid=(lay.n_tiles(lat_only),),
        in_specs=[pl.BlockSpec((1, 1, TM * TOP_K), lambda i: (lay.tile(i, lat_only), 0, 0), memory_space=pltpu.SMEM),
                  pl.BlockSpec((TM * PACKED_TILES, LANES), lambda i: (lay.tile(i, lat_only), 0))],
        out_specs=pl.BlockSpec(memory_space=pl.ANY),
        scratch_shapes=[pltpu.SemaphoreType.DMA],
        out_shape=jax.ShapeDtypeStruct((n_rows * PACKED_TILES, LANES), U32),
        compiler_params=_params(("arbitrary",)),
        name="dispatch",
    )(dest_flat, hp)


def _experts_kernel(be_ref, bv_ref, nu_ref, xs_ref, wg_ref, wu_ref, wd_ref, ys_ref, wgb_ref, wub_ref, wdb_ref):
    j = pl.program_id(0)
    live = j < nu_ref[0]
    prev = be_ref[jnp.maximum(j - 1, 0)]

    @pl.when(live & ((j == 0) | (be_ref[j] != prev)))
    def _():
        wgb_ref[...] = wg_ref[0, 0].astype(BF16)
        wub_ref[...] = wu_ref[0, 0].astype(BF16)
        wdb_ref[...] = wd_ref[0, 0].astype(BF16)

    @pl.when(live)
    def _():
        packed = _from_row_tiles(xs_ref, PACKED_TILES)
        rowi = lax.broadcasted_iota(I32, packed.shape, 0)
        packed = jnp.where(rowi < bv_ref[j], packed, jnp.uint32(0))
        x = _unpack_bf16_pairs(packed)
        hid = (_silu(jnp.dot(x, wgb_ref[...], preferred_element_type=F32))
               * jnp.dot(x, wub_ref[...], preferred_element_type=F32))
        y = jnp.dot(hid.astype(BF16), wdb_ref[...], preferred_element_type=F32)
        _to_row_tiles(ys_ref, _pack_bf16_pairs(y))


def _experts(block_e, block_valid, n_used, xs, w_gate, w_up, w_down, layer):
    n_blocks = xs.shape[0] // (MOE_BLK * PACKED_TILES)
    last = lambda j, nu: jnp.minimum(j, nu[0] - 1)
    wmap = lambda j, be, bv, nu: (layer, be[last(j, nu)], 0, 0)
    return pl.pallas_call(
        _experts_kernel,
        grid_spec=pltpu.PrefetchScalarGridSpec(
            num_scalar_prefetch=3, grid=(n_blocks,),
            in_specs=[pl.BlockSpec((MOE_BLK * PACKED_TILES, LANES), lambda j, be, bv, nu: (last(j, nu), 0)),
                      pl.BlockSpec((1, 1, D_MODEL, EXPERT_HIDDEN), wmap),
                      pl.BlockSpec((1, 1, D_MODEL, EXPERT_HIDDEN), wmap),
                      pl.BlockSpec((1, 1, EXPERT_HIDDEN, D_MODEL), wmap)],
            out_specs=pl.BlockSpec((MOE_BLK * PACKED_TILES, LANES), lambda j, be, bv, nu: (last(j, nu), 0)),
            scratch_shapes=[pltpu.VMEM((D_MODEL, EXPERT_HIDDEN), BF16), pltpu.VMEM((D_MODEL, EXPERT_HIDDEN), BF16),
                            pltpu.VMEM((EXPERT_HIDDEN, D_MODEL), BF16)]),
        out_shape=jax.ShapeDtypeStruct((n_blocks * MOE_BLK * PACKED_TILES, LANES), U32),
        compiler_params=_params(("arbitrary",)),
        name="experts",
    )(block_e, block_valid, n_used, xs, w_gate, w_up, w_down)


def _combine_kernel(dest_ref, next_ref, gate_ref, hp_ref, x_ref, gf_ref, g3_ref, sg_ref, su_ref, sd_ref,
                    ys_ref, o_ref, buf_ref, acc_ref, sem):
    c = PACKED_TILES
    tm = hp_ref.shape[0] // c
    i = pl.program_id(0)
    cur = i % 2

    parts = 4
    has_next = i + 1 < pl.num_programs(0)

    def gather(idx_ref, half, part):
        def issue(t, carry):
            slab = pl.ds(pl.multiple_of(t * c, c), c)
            for k in range(TOP_K):
                src = pl.multiple_of(idx_ref[0, 0, t * TOP_K + k], c)
                pltpu.make_async_copy(ys_ref.at[pl.ds(src, c)], buf_ref.at[half, k, slab],
                                      sem.at[half]).start(priority=k % 2)
            return carry
        lax.fori_loop(part * (tm // parts), (part + 1) * (tm // parts), issue, 0)

    def gather_next(part):
        pl.when(has_next)(lambda: gather(next_ref, 1 - cur, part))

    @pl.when(i == 0)
    def _():
        for part in range(parts):
            gather(dest_ref, 0, part)

    gather_next(0)
    hb = _unpack_bf16_pairs(_from_row_tiles(hp_ref, c))
    hid = (_silu(jnp.dot(hb, sg_ref[...], preferred_element_type=F32))
           * jnp.dot(hb, su_ref[...], preferred_element_type=F32))
    f = jnp.dot(hid.astype(BF16), sd_ref[...], preferred_element_type=F32)
    gather_next(1)
    for k in range(TOP_K):
        pltpu.make_async_copy(ys_ref.at[pl.ds(0, tm * c)], buf_ref.at[cur, k], sem.at[cur]).wait()

    def weighted(t, carry):
        slab = pl.ds(pl.multiple_of(t * c, c), c)
        acc_hi = jnp.zeros((c, LANES), F32)
        acc_lo = jnp.zeros((c, LANES), F32)
        for k in range(TOP_K):
            packed = buf_ref[cur, k, slab, :]
            gk = gate_ref[0, 0, t * TOP_K + k]
            acc_hi = acc_hi + gk * lax.bitcast_convert_type(packed & jnp.uint32(0xFFFF0000), F32)
            acc_lo = acc_lo + gk * lax.bitcast_convert_type(packed << 16, F32)
        acc_ref[0, slab, :] = acc_hi
        acc_ref[1, slab, :] = acc_lo
        return carry

    lax.fori_loop(0, tm // 2, weighted, 0, unroll=4)
    gather_next(2)
    lax.fori_loop(tm // 2, tm, weighted, 0, unroll=4)
    gather_next(3)
    f = f + jnp.concatenate([_from_row_tiles(acc_ref, c, lead=(0,)), _from_row_tiles(acc_ref, c, lead=(1,))], axis=1)
    o_ref[...] = x_ref[...] + gf_ref[0] * _rms(f, g3_ref[...])


def _combine(lay, dest_flat, gate_flat, hp, x_mid, mods, g3, s_gate, s_up, s_down, ys, lat_only, compact_out):
    row = lambda i: (lay.tile(i, lat_only), 0)
    full = lambda shape: pl.BlockSpec(shape, lambda i: (0, 0))
    out_rows = lay.n_tiles(lat_only) * TM if compact_out else lay.rows
    out_map = (lambda i: (i, 0)) if compact_out else row
    n_steps = lay.n_tiles(lat_only)
    flat = lambda f: pl.BlockSpec((1, 1, TM * TOP_K), lambda i: (lay.tile(f(i), lat_only), 0, 0),
                                  memory_space=pltpu.SMEM)
    return pl.pallas_call(
        _combine_kernel,
        grid=(n_steps,),
        in_specs=[flat(lambda i: i), flat(lambda i: jnp.minimum(i + 1, n_steps - 1)), flat(lambda i: i),
                  pl.BlockSpec((TM * PACKED_TILES, LANES), row),
                  pl.BlockSpec((TM, D_MODEL), row), _mod_spec(lay, 5, lat_only),
                  full((1, D_MODEL)), full((D_MODEL, EXPERT_HIDDEN)), full((D_MODEL, EXPERT_HIDDEN)),
                  full((EXPERT_HIDDEN, D_MODEL)), pl.BlockSpec(memory_space=pl.ANY)],
        out_specs=pl.BlockSpec((TM, D_MODEL), out_map),
        scratch_shapes=[pltpu.VMEM((2, TOP_K, TM * PACKED_TILES, LANES), U32),
                        pltpu.VMEM((2, TM * PACKED_TILES, LANES), F32), pltpu.SemaphoreType.DMA((2,))],
        out_shape=jax.ShapeDtypeStruct((out_rows, D_MODEL), F32),
        compiler_params=_params(("arbitrary",)),
        name="combine",
    )(dest_flat, dest_flat, gate_flat, hp, x_mid, mods, g3, s_gate, s_up, s_down, ys)


def _rope_tables(lay):
    n_freq = HEAD_DIM // 4
    inv_freq = ROPE_THETA ** (-jnp.arange(n_freq, dtype=F32) / n_freq)
    t = jnp.arange(lay.n_lat)
    pos = jnp.stack([(t // GRID_W).astype(F32), (t % GRID_W).astype(F32)], axis=-1)
    ang = pos[:, :, None, None] * inv_freq
    ang = jnp.broadcast_to(ang, (lay.n_lat, 2, 2, n_freq)).reshape(lay.n_lat, HEAD_DIM)
    sign = jnp.tile(jnp.concatenate([-jnp.ones(n_freq, F32), jnp.ones(n_freq, F32)]), 2)
    cos = jnp.concatenate([jnp.ones((lay.n_ctx, HEAD_DIM), F32), jnp.cos(ang)], axis=0)
    sin = jnp.concatenate([jnp.zeros((lay.n_ctx, HEAD_DIM), F32), jnp.sin(ang) * sign], axis=0)
    return cos, sin


def _moe_plan(counts, n_blocks):
    counts = counts.reshape(-1)
    padded = (counts + MOE_BLK - 1) // MOE_BLK * MOE_BLK
    pad_end = jnp.cumsum(padded)
    pad_start = (pad_end - padded).astype(I32)
    blk_row = jnp.arange(n_blocks, dtype=I32) * MOE_BLK
    block_e = jnp.minimum(jnp.sum(blk_row[:, None] >= pad_end[None, :], axis=1), N_EXPERTS - 1).astype(I32)
    own = block_e[:, None] == jnp.arange(N_EXPERTS, dtype=I32)[None, :]
    count_b = jnp.sum(jnp.where(own, counts[None, :], 0), axis=1)
    start_b = jnp.sum(jnp.where(own, pad_start[None, :], 0), axis=1)
    block_valid = jnp.clip(count_b - (blk_row - start_b), 0, MOE_BLK).astype(I32)
    n_used = jnp.maximum(pad_end[-1:] // MOE_BLK, 1).astype(I32)
    return pad_start, block_e, block_valid, n_used


def kernel(x, c, ctx, c_ctx, mod_w, mod_b, norm_g, w_in, w_out, conv_w, conv_b, lru_wr, lru_br, lru_wi, lru_bi,
           lru_lambda, gla_w2, gla_b2, gla_norm_g, qk_norm_g, win_sink, router_w, router_bias, exp_w_gate,
           exp_w_up, exp_w_down, sh_w_gate, sh_w_up, sh_w_down):
    bsz, n_lat, d = x.shape
    n_ctx = ctx.shape[1]
    depth = mod_w.shape[0]
    assert d == D_MODEL and bsz < MOD_ROWS
    lay = _Layout(bsz, n_lat, n_ctx)

    cc = jnp.zeros((MOD_ROWS, d), F32).at[:bsz].set(c).at[bsz].set(c_ctx)
    mods_all = _adaln(cc, mod_w, mod_b)
    cos, sin = _rope_tables(lay)
    xt = None

    splits = np.cumsum([0, W_A, W_B_RAW, W_C, W_D])
    for layer in range(depth):
        last = layer == depth - 1
        need_ctx = not last
        lat_only = last
        mods = mods_all[layer].reshape(MOD_ROWS * 6, 1, d)
        g = norm_g[layer]
        wl = w_in[layer]
        w_cat = jnp.concatenate(
            [wl[:, splits[0]:splits[1]], wl[:, splits[1]:splits[2]], jnp.zeros((d, W_B - W_B_RAW), F32),
             wl[:, splits[2]:splits[3]], wl[:, splits[3]:splits[4]]], axis=1).astype(BF16)
        if xt is None:
            p_a, p_b, p_c, p_d, xt = _inproj(lay, (x.reshape(bsz * n_lat, d), ctx.reshape(bsz * n_ctx, d)), mods,
                                             g[0:1], w_cat)
        else:
            p_a, p_b, p_c, p_d = _inproj(lay, (xt,), mods, g[0:1], w_cat)

        y_a = _rglru(lay, p_a, conv_w[layer], conv_b[layer], lru_wr[layer], lru_br[layer], lru_wi[layer],
                     lru_bi[layer], lru_lambda[layer])
        w2pad = jnp.zeros((2, LANES, GLA_HEADS * GLA_DK), F32)
        w2pad = w2pad.at[0, :GLA_RANK].set(gla_w2[layer, 0]).at[1, GLA_RANK:2 * GLA_RANK].set(gla_w2[layer, 1])
        y_b = _gla(lay, p_b, w2pad, gla_b2[layer], gla_norm_g[layer])
        y_c = _attention(lay, p_c, cos, sin, qk_norm_g[layer], window=False, need_ctx=need_ctx)
        y_d = _attention(lay, p_d, cos, sin, win_sink[layer], window=True, need_ctx=need_ctx)

        x_mid, h, hp = _outproj(lay, (y_a, y_b, y_c, y_d), w_out[layer].astype(BF16), xt, mods, g[1:2], g[2:3],
                                lat_only)
        top_e, gate, rank, counts = _router(lay, h, router_w[layer].T, router_bias[layer], lat_only)
        n_assign = lay.n_tiles(lat_only) * TM * TOP_K
        n_blocks = -(-(n_assign + N_EXPERTS * (MOE_BLK - 1)) // MOE_BLK)
        pad_start, block_e, block_valid, n_used = _moe_plan(counts, n_blocks)
        owner = top_e[:, :, None] == jnp.arange(N_EXPERTS, dtype=I32)
        dest = jnp.sum(jnp.where(owner, pad_start, 0), axis=-1) + rank
        dest_flat = (dest * PACKED_TILES).T.reshape(lay.rows // TM, 1, TM * TOP_K)
        xs = _dispatch(lay, dest_flat, hp, n_blocks * MOE_BLK, lat_only)
        ys = _experts(block_e, block_valid, n_used, xs, exp_w_gate, exp_w_up, exp_w_down, layer)
        gate_flat = gate.T.reshape(lay.rows // TM, 1, TM * TOP_K)
        xt = _combine(lay, dest_flat, gate_flat, hp, x_mid, mods, g[3:4], sh_w_gate[layer].astype(BF16),
                      sh_w_up[layer].astype(BF16), sh_w_down[layer].astype(BF16), ys, lat_only, compact_out=last)
    return xt.reshape(bsz, n_lat, d)
```

```python
import functools

import numpy as np
import jax
import jax.numpy as jnp
from jax import lax
from jax.experimental import pallas as pl
from jax.experimental.pallas import tpu as pltpu

F32 = jnp.float32
BF16 = jnp.bfloat16
I32 = jnp.int32
U32 = jnp.uint32
HIGHEST = lax.Precision.HIGHEST
NT_DIMS = (((1,), (1,)), ((), ()))
TN_DIMS = (((0,), (0,)), ((), ()))

LANES = 128
SUBLANES = 8
VMEM_LIMIT_BYTES = 56 * 1024 * 1024

D_MODEL = 2048
GRID_W = 64
HEAD_DIM = 128
GROUP_WIDTH = D_MODEL // 4
LRU_BLOCKS = 4
CONV_W = 4
RGLRU_C = 8.0
GLA_HEADS = 4
GLA_DK = 64
GLA_DV = 128
GLA_RANK = 16
GLA_TAU = 16.0
GLA_CHUNK = 64
GLA_HEADS_PER_STEP = 4
KV_HEADS = 2
WINDOW = 128
N_EXPERTS = 64
N_GROUPS = 8
TOPK_GROUPS = 4
TOP_K = 8
EXPERT_HIDDEN = 512
ROUTED_SCALE = 2.5
RMS_EPS = 1e-6
ROPE_THETA = 10000.0
NEG_INF = -1e30

W_A = 2 * GROUP_WIDTH
W_B_RAW = 2 * GLA_HEADS * GLA_DK + 2 * GLA_HEADS * GLA_DV + 2 * GLA_RANK
W_B = 1664
W_C = GROUP_WIDTH + 2 * KV_HEADS * HEAD_DIM
W_D = W_C
Z_COL_BLOCK = (2 * GLA_HEADS * GLA_DK + 2 * GLA_HEADS * GLA_DV) // LANES

TM = 256
MOE_BLK = 512
PACKED_TILES = D_MODEL // 2 // LANES
MOD_ROWS = 16


def _params(sem, vmem=VMEM_LIMIT_BYTES):
    return pltpu.CompilerParams(dimension_semantics=sem, vmem_limit_bytes=vmem)


def _rms(x, g):
    return x * lax.rsqrt(jnp.mean(x * x, axis=-1, keepdims=True) + RMS_EPS) * g


def _silu(x):
    return x * jax.nn.sigmoid(x)


def _pack_bf16_pairs(h):
    n = h.shape[1] // 2
    hb = h.astype(BF16).astype(F32)
    hi = lax.bitcast_convert_type(hb[:, :n], U32)
    lo = lax.bitcast_convert_type(hb[:, n:], U32)
    return hi | (lo >> 16)


def _to_row_tiles(ref, x):
    m = x.shape[0]
    c = x.shape[1] // LANES
    for j in range(c):
        ref[pl.ds(j, m, stride=c), :] = x[:, j * LANES:(j + 1) * LANES]


def _from_row_tiles(ref, c, lead=()):
    m = ref.shape[-2] // c
    return jnp.concatenate([ref[lead + (pl.ds(j, m, stride=c), slice(None))] for j in range(c)], axis=1)


def _unpack_bf16_pairs(p):
    hi = lax.bitcast_convert_type(p & jnp.uint32(0xFFFF0000), F32)
    lo = lax.bitcast_convert_type(p << 16, F32)
    return jnp.concatenate([hi, lo], axis=1).astype(BF16)


def _adaln_kernel(c_ref, w_ref, b_ref, o_ref):
    a = _silu(c_ref[...])
    o_ref[0] = jnp.dot(a, w_ref[0], preferred_element_type=F32, precision=HIGHEST) + b_ref[0]


def _adaln(cc, mod_w, mod_b):
    depth, d, n = mod_w.shape
    tn = 1024
    return pl.pallas_call(
        _adaln_kernel,
        grid=(depth, n // tn),
        in_specs=[pl.BlockSpec((MOD_ROWS, d), lambda l, j: (0, 0)),
                  pl.BlockSpec((1, d, tn), lambda l, j: (l, 0, j)),
                  pl.BlockSpec((1, 1, tn), lambda l, j: (l, 0, j))],
        out_specs=pl.BlockSpec((1, MOD_ROWS, tn), lambda l, j: (l, 0, j)),
        out_shape=jax.ShapeDtypeStruct((depth, MOD_ROWS, n), F32),
        compiler_params=_params(("arbitrary", "arbitrary")),
        name="adaln",
    )(cc, mod_w, mod_b.reshape(depth, 1, n))


class _Layout:
    def __init__(self, bsz, n_lat, n_ctx):
        self.bsz, self.n_lat, self.n_ctx = bsz, n_lat, n_ctx
        self.seq = n_ctx + n_lat
        self.rows = bsz * self.seq
        assert n_ctx % TM == 0 and n_lat % TM == 0
        self.tiles_b = self.seq // TM
        self.ctx_tiles = n_ctx // TM
        self.lat_tiles = n_lat // TM

    def n_tiles(self, lat_only):
        return self.bsz * (self.lat_tiles if lat_only else self.tiles_b)

    def tile(self, i, lat_only):
        if lat_only:
            return (i // self.lat_tiles) * self.tiles_b + self.ctx_tiles + i % self.lat_tiles
        return i

    def mod_row(self, i, lat_only):
        if lat_only:
            return i // self.lat_tiles
        return jnp.where(i % self.tiles_b < self.ctx_tiles, self.bsz, i // self.tiles_b)


def _mod_spec(lay, chunk, lat_only):
    return pl.BlockSpec((1, 1, D_MODEL), lambda i: (lay.mod_row(i, lat_only) * 6 + chunk, 0, 0))


def _inproj_kernel(*refs, tiles_b, ctx_tiles):
    if ctx_tiles:
        x_ref, c_ref, sh_ref, sc_ref, g_ref, w_ref, oa_ref, ob_ref, oc_ref, od_ref, xt_ref = refs
        x = jnp.where(pl.program_id(0) % tiles_b < ctx_tiles, c_ref[...], x_ref[...])
        xt_ref[...] = x
    else:
        x_ref, sh_ref, sc_ref, g_ref, w_ref, oa_ref, ob_ref, oc_ref, od_ref = refs
        x = x_ref[...]
    h = _rms(x, g_ref[...]) * (1.0 + sc_ref[0]) + sh_ref[0]
    hb = h.astype(BF16)
    c0 = 0
    for o_ref in (oa_ref, ob_ref, oc_ref, od_ref):
        width = o_ref.shape[1]
        o_ref[...] = jnp.dot(hb, w_ref[:, c0:c0 + width], preferred_element_type=F32)
        c0 += width


def _inproj(lay, xs, mods, g0, w_cat):
    widths = (W_A, W_B, W_C, W_D)
    row = lambda i: (i, 0)
    split = len(xs) == 2
    x_specs = [pl.BlockSpec((TM, D_MODEL), row)]
    out_specs = [pl.BlockSpec((TM, w), row) for w in widths]
    out_shape = [jax.ShapeDtypeStruct((lay.rows, w), F32) for w in widths]
    if split:
        lat = lambda i: ((i // lay.tiles_b) * lay.lat_tiles + jnp.maximum(i % lay.tiles_b - lay.ctx_tiles, 0), 0)
        ctx = lambda i: ((i // lay.tiles_b) * lay.ctx_tiles + jnp.minimum(i % lay.tiles_b, lay.ctx_tiles - 1), 0)
        x_specs = [pl.BlockSpec((TM, D_MODEL), lat), pl.BlockSpec((TM, D_MODEL), ctx)]
        out_specs.append(pl.BlockSpec((TM, D_MODEL), row))
        out_shape.append(jax.ShapeDtypeStruct((lay.rows, D_MODEL), F32))
    return pl.pallas_call(
        functools.partial(_inproj_kernel, tiles_b=lay.tiles_b, ctx_tiles=lay.ctx_tiles if split else 0),
        grid=(lay.n_tiles(False),),
        in_specs=x_specs + [_mod_spec(lay, 0, False), _mod_spec(lay, 1, False),
                            pl.BlockSpec((1, D_MODEL), lambda i: (0, 0)),
                            pl.BlockSpec((D_MODEL, sum(widths)), lambda i: (0, 0), pipeline_mode=pl.Buffered(1))],
        out_specs=out_specs,
        out_shape=out_shape,
        compiler_params=_params(("arbitrary",)),
        name="inproj",
    )(*xs, mods, mods, g0, w_cat)


def _rglru_kernel(x_ref, g_ref, cw_ref, cb_ref, wr_ref, br_ref, wi_ref, bi_ref, lam_ref, o_ref,
                  xs_ref, a_ref, b_ref, h_ref, *, n_ctx):
    seq = x_ref.shape[0]
    pad = SUBLANES
    n_tiles = seq // SUBLANES
    ctx_tiles = n_ctx // SUBLANES
    row = lax.broadcasted_iota(I32, (seq, LANES), 0)

    xs_ref[0:pad, :] = jnp.zeros((pad, LANES), F32)
    xs_ref[pad + seq:2 * pad + seq, :] = jnp.zeros((pad, LANES), F32)
    xs_ref[pad:pad + seq, :] = x_ref[...]
    u = jnp.zeros((seq, LANES), F32) + cb_ref[...]
    for j in range(CONV_W):
        off = j - 2
        xsh = xs_ref[pad + off:pad + off + seq, :]
        if off < 0:
            xsh = jnp.where((row >= n_ctx) & (row + off < n_ctx), 0.0, xsh)
        elif off > 0:
            xsh = jnp.where((row < n_ctx) & (row + off >= n_ctx), 0.0, xsh)
        u = u + cw_ref[j:j + 1, :] * xsh

    ub = u.astype(BF16)
    tile_shape = (n_tiles, SUBLANES, LANES)
    rmod = lax.broadcasted_iota(I32, tile_shape, 1)
    for d in range(2):
        reverse = d == 1
        r = jax.nn.sigmoid(jnp.dot(ub, wr_ref[d, 0].astype(BF16), preferred_element_type=F32) + br_ref[d:d + 1, :])
        gi = jax.nn.sigmoid(jnp.dot(ub, wi_ref[d, 0].astype(BF16), preferred_element_type=F32) + bi_ref[d:d + 1, :])
        lam = lam_ref[d:d + 1, :]
        softplus_neg = jnp.maximum(-lam, 0.0) + jnp.log(1.0 + jnp.exp(-jnp.abs(lam)))
        a = jnp.exp(-RGLRU_C * r * softplus_neg)
        b = jnp.sqrt(1.0 - a * a) * (gi * u)
        a = a.reshape(tile_shape)
        b = b.reshape(tile_shape)
        for s in (1, 2, 4):
            if reverse:
                ap, bp, ok = pltpu.roll(a, SUBLANES - s, 1), pltpu.roll(b, SUBLANES - s, 1), rmod < SUBLANES - s
            else:
                ap, bp, ok = pltpu.roll(a, s, 1), pltpu.roll(b, s, 1), rmod >= s
            b = jnp.where(ok, a * bp + b, b)
            a = jnp.where(ok, a * ap, a)
        a_ref[...] = a.reshape(seq, LANES)
        b_ref[...] = b.reshape(seq, LANES)

        def step(t, h, reverse=reverse):
            r0 = pl.multiple_of(t * SUBLANES, SUBLANES)
            h8 = a_ref[pl.ds(r0, SUBLANES), :] * h + b_ref[pl.ds(r0, SUBLANES), :]
            if reverse:
                h_ref[pl.ds(r0, SUBLANES), :] = h_ref[pl.ds(r0, SUBLANES), :] + h8
                return jnp.broadcast_to(h8[0:1, :], (SUBLANES, LANES))
            h_ref[pl.ds(r0, SUBLANES), :] = h8
            return jnp.broadcast_to(h8[SUBLANES - 1:SUBLANES, :], (SUBLANES, LANES))

        h0 = jnp.zeros((SUBLANES, LANES), F32)
        if reverse:
            hc = lax.fori_loop(0, ctx_tiles, lambda i, h: step(ctx_tiles - 1 - i, h), h0)
            lax.fori_loop(0, n_tiles - ctx_tiles, lambda i, h: step(n_tiles - 1 - i, h), hc)
        else:
            lax.fori_loop(0, n_tiles, step, h0)

    g = g_ref[...]
    gelu = 0.5 * g * (1.0 + jnp.tanh(np.sqrt(2.0 / np.pi) * (g + 0.044715 * (g * g * g))))
    o_ref[...] = h_ref[...] * gelu


def _rglru(lay, p_a, conv_w, conv_b, wr, br, wi, bi, lam):
    seq = lay.seq
    nb = LRU_BLOCKS
    col = lambda off: (lambda b, j: (b, off + j))
    vec = lambda rows: pl.BlockSpec((rows, LANES), lambda b, j: (0, j))
    wspec = pl.BlockSpec((2, 1, LANES, LANES), lambda b, j: (0, j, 0, 0))
    return pl.pallas_call(
        functools.partial(_rglru_kernel, n_ctx=lay.n_ctx),
        grid=(lay.bsz, nb),
        in_specs=[pl.BlockSpec((seq, LANES), col(0)), pl.BlockSpec((seq, LANES), col(nb)),
                  vec(CONV_W), vec(1), wspec, vec(2), wspec, vec(2), vec(2)],
        out_specs=pl.BlockSpec((seq, LANES), col(0)),
        out_shape=jax.ShapeDtypeStruct((lay.rows, GROUP_WIDTH), F32),
        scratch_shapes=[pltpu.VMEM((seq + 2 * SUBLANES, LANES), F32), pltpu.VMEM((seq, LANES), F32),
                        pltpu.VMEM((seq, LANES), F32), pltpu.VMEM((seq, LANES), F32)],
        compiler_params=_params(("arbitrary", "arbitrary")),
        name="rglru",
    )(p_a, p_a, conv_w, conv_b.reshape(1, -1), wr, br, wi, bi, lam)


def _gla_kernel(q_ref, k_ref, v_ref, g_ref, z_ref, w2_ref, b2_ref, ng_ref, o_ref,
                laf_ref, lab_ref, of_ref, ob_ref, *, n_ctx):
    seq = q_ref.shape[0]
    nh = q_ref.shape[1] // GLA_DK
    qk_w = nh * GLA_DK
    cs = GLA_CHUNK
    n_chunks = seq // cs
    ctx_chunks = n_ctx // cs
    z = z_ref[...]
    for d, la_ref in ((0, laf_ref), (1, lab_ref)):
        pre = jnp.dot(z, w2_ref[d], preferred_element_type=F32, precision=HIGHEST) + b2_ref[d:d + 1, :]
        log_sig = jnp.minimum(pre, 0.0) - jnp.log(1.0 + jnp.exp(-jnp.abs(pre)))
        la_ref[...] = log_sig * (1.0 / GLA_TAU)

    ri = lax.broadcasted_iota(I32, (cs, cs), 0)
    ci = lax.broadcasted_iota(I32, (cs, cs), 1)
    scale = GLA_DK ** -0.5

    def chunk(c, states, reverse):
        r0 = pl.multiple_of(c * cs, cs)
        rows = pl.ds(r0, cs)
        keep = (ri <= ci) if reverse else (ri >= ci)
        la = (lab_ref if reverse else laf_ref)[rows, :]
        la_hi = la.astype(BF16)
        rest = la - la_hi.astype(F32)
        la_mid = rest.astype(BF16)
        la_lo = (rest - la_mid.astype(F32)).astype(BF16)
        parts = jnp.dot(keep.astype(BF16), jnp.concatenate([la_hi, la_mid, la_lo], axis=1),
                        preferred_element_type=F32)
        cum = parts[:, 0:qk_w] + parts[:, qk_w:2 * qk_w] + parts[:, 2 * qk_w:3 * qk_w]
        last = cum[0:1, :] if reverse else cum[cs - 1:cs, :]
        q = q_ref[rows, :] * scale
        k = k_ref[rows, :]
        qe = q * jnp.exp(cum)
        ke = k * jnp.exp(-cum)
        kl = k * jnp.exp(last - cum)
        el = jnp.exp(last)
        out_ref = ob_ref if reverse else of_ref
        new_states = []
        for hh in range(nh):
            dk = slice(hh * GLA_DK, (hh + 1) * GLA_DK)
            dv = slice(hh * GLA_DV, (hh + 1) * GLA_DV)
            qh = qe[:, dk].astype(BF16)
            vh = v_ref[rows, dv].astype(BF16)
            st = states[hh]
            s = lax.dot_general(qh, ke[:, dk].astype(BF16), NT_DIMS, preferred_element_type=F32)
            s = jnp.where(keep, s, 0.0)
            o = (jnp.dot(s.astype(BF16), vh, preferred_element_type=F32)
                 + lax.dot_general(qh, st.astype(BF16), NT_DIMS, preferred_element_type=F32))
            out_ref[rows, dv] = o
            new_states.append(st * el[:, dk]
                              + lax.dot_general(vh, kl[:, dk].astype(BF16), TN_DIMS, preferred_element_type=F32))
        return tuple(new_states)

    def body(i, carry):
        sf, sb = carry
        sf = chunk(i, sf, False)
        cb = jnp.where(i < ctx_chunks, ctx_chunks - 1 - i, n_chunks - 1 + ctx_chunks - i)
        sb = chunk(cb, sb, True)
        return sf, sb

    zero = (jnp.zeros((GLA_DV, GLA_DK), F32),) * nh
    lax.fori_loop(0, n_chunks, body, (zero, zero), unroll=4)

    ng = ng_ref[...]
    for hh in range(nh):
        dv = slice(hh * GLA_DV, (hh + 1) * GLA_DV)
        o = of_ref[:, dv] + ob_ref[:, dv]
        o_ref[:, dv] = _rms(o, ng) * _silu(g_ref[:, dv])


def _gla(lay, p_b, w2pad, b2, norm_g):
    seq = lay.seq
    hp = GLA_HEADS // GLA_HEADS_PER_STEP
    qk_w = GLA_HEADS_PER_STEP * GLA_DK
    v_w = GLA_HEADS_PER_STEP * GLA_DV
    k_off = GLA_HEADS * GLA_DK // qk_w
    v_off = 2 * GLA_HEADS * GLA_DK // v_w
    g_off = v_off + GLA_HEADS * GLA_DV // v_w
    once = pl.Buffered(1)
    blk = lambda w, off, mode=None: pl.BlockSpec((seq, w), lambda b, j: (b, off + j), pipeline_mode=mode)
    return pl.pallas_call(
        functools.partial(_gla_kernel, n_ctx=lay.n_ctx),
        grid=(lay.bsz, hp),
        in_specs=[blk(qk_w, 0, once), blk(qk_w, k_off, once), blk(v_w, v_off, once), blk(v_w, g_off, once),
                  pl.BlockSpec((seq, LANES), lambda b, j: (b, Z_COL_BLOCK), pipeline_mode=once),
                  pl.BlockSpec((2, LANES, qk_w), lambda b, j: (0, 0, j)),
                  pl.BlockSpec((2, qk_w), lambda b, j: (0, j)),
                  pl.BlockSpec((1, GLA_DV), lambda b, j: (0, 0))],
        out_specs=blk(v_w, 0),
        out_shape=jax.ShapeDtypeStruct((lay.rows, GROUP_WIDTH), F32),
        scratch_shapes=[pltpu.VMEM((seq, qk_w), F32), pltpu.VMEM((seq, qk_w), F32),
                        pltpu.VMEM((seq, v_w), F32), pltpu.VMEM((seq, v_w), F32)],
        compiler_params=_params(("arbitrary", "arbitrary")),
        name="gla",
    )(p_b, p_b, p_b, p_b, p_b, w2pad, b2, norm_g.reshape(1, -1))


def _rope(x, cos, sin_signed):
    lane = lax.broadcasted_iota(I32, x.shape, 1)
    partner = jnp.where((lane & 32) != 0, pltpu.roll(x, 32, 1), pltpu.roll(x, HEAD_DIM - 32, 1))
    return x * cos + partner * sin_signed


def _attn_kernel(*refs, window, need_ctx, tq, n_ctx):
    if window:
        sink_ref, q_ref, k_ref, v_ref, cos_ref, sin_ref, o_ref, ks_ref, vs_ref = refs
        qkg_ref = None
    else:
        q_ref, k_ref, v_ref, cos_ref, sin_ref, qkg_ref, o_ref, ks_ref, vs_ref = refs
    seq = k_ref.shape[0]
    n_lat = seq - n_ctx
    g = pl.program_id(1)
    qi = pl.program_id(2)
    jq = qi if need_ctx else qi + n_ctx // tq

    @pl.when(qi == 0)
    def _():
        k = k_ref[...]
        if not window:
            k = _rms(k, qkg_ref[1:2, :])
        ks_ref[...] = _rope(k, cos_ref[...], sin_ref[...]).T.astype(BF16)
        vs_ref[...] = v_ref[...].astype(BF16)

    r0 = pl.multiple_of(jq * tq, tq)
    cos = cos_ref[pl.ds(r0, tq), :]
    sin = sin_ref[pl.ds(r0, tq), :]
    heads = []
    for hh in range(2):
        qh = q_ref[:, hh * HEAD_DIM:(hh + 1) * HEAD_DIM]
        if not window:
            qh = _rms(qh, qkg_ref[0:1, :])
        heads.append((_rope(qh, cos, sin) * HEAD_DIM ** -0.5).astype(BF16))
    q2 = jnp.concatenate(heads, axis=0)

    def finish(o):
        o_ref[:, 0:HEAD_DIM] = o[0:tq]
        o_ref[:, HEAD_DIM:2 * HEAD_DIM] = o[tq:2 * tq]

    def scores(lo, n):
        return jnp.dot(q2, ks_ref[:, pl.ds(lo, n)], preferred_element_type=F32)

    def pv(p, lo, n):
        return jnp.dot(p.astype(BF16), vs_ref[pl.ds(lo, n), :], preferred_element_type=F32)

    if window:
        rowi = lax.broadcasted_iota(I32, (2 * tq, 1), 0)
        sink = jnp.where(rowi < tq, sink_ref[2 * g], sink_ref[2 * g + 1])

    def ctx_tile():
        s = scores(0, n_ctx)
        m = jnp.max(s, axis=-1, keepdims=True)
        if window:
            m = jnp.maximum(m, sink)
        p = jnp.exp(s - m)
        l = jnp.sum(p, axis=-1, keepdims=True)
        if window:
            l = l + jnp.exp(sink - m)
        finish(pv(p, 0, n_ctx) / l)

    def lat_tile():
        if not window:
            s = scores(0, seq)
            m = jnp.max(s, axis=-1, keepdims=True)
            p = jnp.exp(s - m)
            finish(pv(p, 0, seq) / jnp.sum(p, axis=-1, keepdims=True))
            return
        band = tq + 2 * WINDOW
        jl = jq - n_ctx // tq
        start = jnp.clip(jl * tq - WINDOW, 0, n_lat - band)
        lo = pl.multiple_of(n_ctx + start, WINDOW)
        sb = scores(lo, band)
        q_pos = jl * tq + lax.broadcasted_iota(I32, (2 * tq, band), 0) % tq
        k_pos = start + lax.broadcasted_iota(I32, (2 * tq, band), 1)
        sb = jnp.where(jnp.abs(q_pos - k_pos) <= WINDOW, sb, NEG_INF)
        sc = scores(0, n_ctx)
        m = jnp.maximum(jnp.maximum(jnp.max(sb, axis=-1, keepdims=True), jnp.max(sc, axis=-1, keepdims=True)), sink)
        pb = jnp.exp(sb - m)
        pc = jnp.exp(sc - m)
        l = jnp.sum(pb, axis=-1, keepdims=True) + jnp.sum(pc, axis=-1, keepdims=True) + jnp.exp(sink - m)
        finish((pv(pb, lo, band) + pv(pc, 0, n_ctx)) / l)

    if need_ctx:
        pl.when(jq * tq < n_ctx)(ctx_tile)
        pl.when(jq * tq >= n_ctx)(lat_tile)
    else:
        lat_tile()


def _attention(lay, p, cos, sin, extra, *, window, need_ctx):
    seq = lay.seq
    tq = TM
    q_tiles_b = seq // tq
    nq = q_tiles_b if need_ctx else lay.n_lat // tq
    skip = 0 if need_ctx else lay.n_ctx // tq
    q_heads_w = 2 * HEAD_DIM
    k_off = GROUP_WIDTH // HEAD_DIM
    v_off = k_off + KV_HEADS

    def im(f):
        return (lambda b, g, qi, *_: f(b, g, qi))

    in_specs = [pl.BlockSpec((tq, q_heads_w), im(lambda b, g, qi: (b * q_tiles_b + skip + qi, g))),
                pl.BlockSpec((seq, HEAD_DIM), im(lambda b, g, qi: (b, k_off + g))),
                pl.BlockSpec((seq, HEAD_DIM), im(lambda b, g, qi: (b, v_off + g))),
                pl.BlockSpec((seq, HEAD_DIM), im(lambda b, g, qi: (0, 0))),
                pl.BlockSpec((seq, HEAD_DIM), im(lambda b, g, qi: (0, 0)))]
    args = [p, p, p, cos, sin]
    if not window:
        in_specs.append(pl.BlockSpec((2, HEAD_DIM), im(lambda b, g, qi: (0, 0))))
        args.append(extra)
    out_spec = pl.BlockSpec((tq, q_heads_w), im(lambda b, g, qi: (b * q_tiles_b + skip + qi, g)))
    kern = functools.partial(_attn_kernel, window=window, need_ctx=need_ctx, tq=tq, n_ctx=lay.n_ctx)
    scratch = [pltpu.VMEM((HEAD_DIM, seq), BF16), pltpu.VMEM((seq, HEAD_DIM), BF16)]
    out_shape = jax.ShapeDtypeStruct((lay.rows, GROUP_WIDTH), F32)
    sem = ("arbitrary", "arbitrary", "arbitrary")
    grid = (lay.bsz, KV_HEADS, nq)
    if window:
        return pl.pallas_call(
            kern,
            grid_spec=pltpu.PrefetchScalarGridSpec(num_scalar_prefetch=1, grid=grid, in_specs=in_specs,
                                                   out_specs=out_spec, scratch_shapes=scratch),
            out_shape=out_shape, compiler_params=_params(sem), name="win_attn",
        )(extra, *args)
    return pl.pallas_call(kern, grid=grid, in_specs=in_specs, out_specs=out_spec, scratch_shapes=scratch,
                          out_shape=out_shape, compiler_params=_params(sem), name="glob_attn")(*args)


def _outproj_kernel(ya_ref, yb_ref, yc_ref, yd_ref, w_ref, x_ref, gm_ref, sh_ref, sc_ref, g1_ref, g2_ref,
                    xo_ref, h_ref, hp_ref):
    acc = None
    for i, y_ref in enumerate((ya_ref, yb_ref, yc_ref, yd_ref)):
        part = jnp.dot(y_ref[...].astype(BF16), w_ref[i * GROUP_WIDTH:(i + 1) * GROUP_WIDTH, :],
                       preferred_element_type=F32)
        acc = part if acc is None else acc + part
    x_new = x_ref[...] + gm_ref[0] * _rms(acc, g1_ref[...])
    xo_ref[...] = x_new
    h = _rms(x_new, g2_ref[...]) * (1.0 + sc_ref[0]) + sh_ref[0]
    h_ref[...] = h
    _to_row_tiles(hp_ref, _pack_bf16_pairs(h))


def _outproj(lay, ys, w_out, xt, mods, g1, g2, lat_only):
    row = lambda i: (lay.tile(i, lat_only), 0)
    full = lambda shape: pl.BlockSpec(shape, lambda i: (0, 0))
    yspec = pl.BlockSpec((TM, GROUP_WIDTH), row)
    xspec = pl.BlockSpec((TM, D_MODEL), row)
    return pl.pallas_call(
        _outproj_kernel,
        grid=(lay.n_tiles(lat_only),),
        in_specs=[yspec, yspec, yspec, yspec, full((D_MODEL, D_MODEL)), xspec,
                  _mod_spec(lay, 2, lat_only), _mod_spec(lay, 3, lat_only), _mod_spec(lay, 4, lat_only),
                  full((1, D_MODEL)), full((1, D_MODEL))],
        out_specs=[xspec, xspec, pl.BlockSpec((TM * PACKED_TILES, LANES), row)],
        out_shape=[jax.ShapeDtypeStruct((lay.rows, D_MODEL), F32), jax.ShapeDtypeStruct((lay.rows, D_MODEL), F32),
                   jax.ShapeDtypeStruct((lay.rows * PACKED_TILES, LANES), U32)],
        compiler_params=_params(("arbitrary",)),
        name="outproj",
    )(*ys, w_out, xt, mods, mods, mods, g1, g2)


def _route(h, wt_ref, bias_ref, e_ref, gate_ref, rank_ref, cnt_ref, carry_ref):
    i = pl.program_id(0)
    tm = h.shape[0]
    ne = N_EXPERTS
    gsz = ne // N_GROUPS

    @pl.when(i == 0)
    def _():
        carry_ref[...] = jnp.zeros_like(carry_ref)

    wt = wt_ref[...]
    w_hi, h_hi = wt.astype(BF16), h.astype(BF16)
    w_lo, h_lo = (wt - w_hi.astype(F32)).astype(BF16), (h - h_hi.astype(F32)).astype(BF16)
    nt = lambda a, b: lax.dot_general(a, b, NT_DIMS, preferred_element_type=F32)
    logits = nt(w_hi, h_hi) + (nt(w_hi, h_lo) + nt(w_lo, h_hi))
    scores = jax.nn.sigmoid(logits)
    biased = scores + bias_ref[...]

    sub = lax.broadcasted_iota(I32, (gsz, tm), 0)
    gs_rows = []
    for grp in range(N_GROUPS):
        blk = biased[grp * gsz:(grp + 1) * gsz, :]
        m1 = jnp.max(blk, axis=0, keepdims=True)
        first = jnp.min(jnp.where(blk == m1, sub, gsz), axis=0, keepdims=True)
        m2 = jnp.max(jnp.where(sub == first, -jnp.inf, blk), axis=0, keepdims=True)
        gs_rows.append(m1 + m2)
    gs = jnp.concatenate(gs_rows, axis=0)

    gidx = lax.broadcasted_iota(I32, (N_GROUPS, tm), 0)
    chosen = jnp.zeros((N_GROUPS, tm), jnp.bool_)
    cur = gs
    for _ in range(TOPK_GROUPS):
        m = jnp.max(cur, axis=0, keepdims=True)
        first = jnp.min(jnp.where(cur == m, gidx, N_GROUPS), axis=0, keepdims=True)
        sel = gidx == first
        chosen = chosen | sel
        cur = jnp.where(sel, -jnp.inf, cur)

    eidx = lax.broadcasted_iota(I32, (ne, tm), 0)
    eligible = jnp.concatenate(
        [jnp.broadcast_to(chosen[grp:grp + 1, :], (gsz, tm)) for grp in range(N_GROUPS)], axis=0)
    cur = jnp.where(eligible, biased, -jnp.inf)
    sels, gates = [], []
    onehot = jnp.zeros((ne, tm), jnp.bool_)
    for k in range(TOP_K):
        m = jnp.max(cur, axis=0, keepdims=True)
        first = jnp.min(jnp.where(cur == m, eidx, ne), axis=0, keepdims=True)
        sel = eidx == first
        e_ref[k:k + 1, :] = first
        gates.append(jnp.sum(jnp.where(sel, scores, 0.0), axis=0, keepdims=True))
        sels.append(sel)
        onehot = onehot | sel
        cur = jnp.where(sel, -jnp.inf, cur)
    total = gates[0]
    for gk in gates[1:]:
        total = total + gk
    for k in range(TOP_K):
        gate_ref[k:k + 1, :] = gates[k] / total * ROUTED_SCALE

    tr = lax.broadcasted_iota(I32, (tm, tm), 0)
    tc = lax.broadcasted_iota(I32, (tm, tm), 1)
    before = (tr < tc).astype(BF16)
    oh = onehot.astype(F32)
    prefix = jnp.dot(oh.astype(BF16), before, preferred_element_type=F32) + carry_ref[...]
    for k in range(TOP_K):
        rank_ref[k:k + 1, :] = jnp.sum(jnp.where(sels[k], prefix, 0.0), axis=0, keepdims=True).astype(I32)
    carry = carry_ref[...] + jnp.sum(oh, axis=1, keepdims=True)
    carry_ref[...] = carry
    cnt_ref[...] = carry.astype(I32)


def _router_kernel(h_ref, wt_ref, bias_ref, e_ref, gate_ref, rank_ref, cnt_ref, carry_ref):
    _route(h_ref[...], wt_ref, bias_ref, e_ref, gate_ref, rank_ref, cnt_ref, carry_ref)


def _router(lay, h, router_wt, bias, lat_only):
    col = lambda i: (0, lay.tile(i, lat_only))
    kt = pl.BlockSpec((TOP_K, TM), col)
    return pl.pallas_call(
        _router_kernel,
        grid=(lay.n_tiles(lat_only),),
        in_specs=[pl.BlockSpec((TM, D_MODEL), lambda i: (lay.tile(i, lat_only), 0)),
                  pl.BlockSpec((N_EXPERTS, D_MODEL), lambda i: (0, 0)),
                  pl.BlockSpec((N_EXPERTS, 1), lambda i: (0, 0))],
        out_specs=[kt, kt, kt, pl.BlockSpec((N_EXPERTS, 1), lambda i: (0, 0))],
        out_shape=[jax.ShapeDtypeStruct((TOP_K, lay.rows), I32), jax.ShapeDtypeStruct((TOP_K, lay.rows), F32),
                   jax.ShapeDtypeStruct((TOP_K, lay.rows), I32), jax.ShapeDtypeStruct((N_EXPERTS, 1), I32)],
        scratch_shapes=[pltpu.VMEM((N_EXPERTS, 1), F32)],
        compiler_params=_params(("arbitrary",)),
        name="router",
    )(h, router_wt, bias.reshape(-1, 1))


def _dispatch_kernel(dest_ref, hp_ref, xs_ref, sem):
    c = PACKED_TILES
    tm = hp_ref.shape[0] // c

    def issue(t, carry):
        src = hp_ref.at[pl.ds(pl.multiple_of(t * c, c), c)]
        for k in range(TOP_K):
            dest = pl.multiple_of(dest_ref[0, 0, t * TOP_K + k], c)
            pltpu.make_async_copy(src, xs_ref.at[pl.ds(dest, c)], sem).start(priority=k % 2)
        return carry

    lax.fori_loop(0, tm, issue, 0, unroll=2)
    for k in range(TOP_K):
        pltpu.make_async_copy(hp_ref, xs_ref.at[pl.ds(0, tm * c)], sem).wait()


def _dispatch(lay, dest_flat, hp, n_rows, lat_only):
    return pl.pallas_call(
        _dispatch_kernel,
        grid=(lay.n_tiles(lat_only),),
        in_specs=[pl.BlockSpec((1, 1, TM * TOP_K), lambda i: (lay.tile(i, lat_only), 0, 0), memory_space=pltpu.SMEM),
                  pl.BlockSpec((TM * PACKED_TILES, LANES), lambda i: (lay.tile(i, lat_only), 0))],
        out_specs=pl.BlockSpec(memory_space=pl.ANY),
        scratch_shapes=[pltpu.SemaphoreType.DMA],
        out_shape=jax.ShapeDtypeStruct((n_rows * PACKED_TILES, LANES), U32),
        compiler_params=_params(("arbitrary",)),
        name="dispatch",
    )(dest_flat, hp)


def _experts_kernel(be_ref, bv_ref, nu_ref, xs_ref, wg_ref, wu_ref, wd_ref, ys_ref, wgb_ref, wub_ref, wdb_ref):
    j = pl.program_id(0)
    live = j < nu_ref[0]
    prev = be_ref[jnp.maximum(j - 1, 0)]

    @pl.when(live & ((j == 0) | (be_ref[j] != prev)))
    def _():
        wgb_ref[...] = wg_ref[0, 0].astype(BF16)
        wub_ref[...] = wu_ref[0, 0].astype(BF16)
        wdb_ref[...] = wd_ref[0, 0].astype(BF16)

    @pl.when(live)
    def _():
        packed = _from_row_tiles(xs_ref, PACKED_TILES)
        rowi = lax.broadcasted_iota(I32, packed.shape, 0)
        packed = jnp.where(rowi < bv_ref[j], packed, jnp.uint32(0))
        x = _unpack_bf16_pairs(packed)
        hid = (_silu(jnp.dot(x, wgb_ref[...], preferred_element_type=F32))
               * jnp.dot(x, wub_ref[...], preferred_element_type=F32))
        y = jnp.dot(hid.astype(BF16), wdb_ref[...], preferred_element_type=F32)
        _to_row_tiles(ys_ref, _pack_bf16_pairs(y))


def _experts(block_e, block_valid, n_used, xs, w_gate, w_up, w_down, layer):
    n_blocks = xs.shape[0] // (MOE_BLK * PACKED_TILES)
    last = lambda j, nu: jnp.minimum(j, nu[0] - 1)
    wmap = lambda j, be, bv, nu: (layer, be[last(j, nu)], 0, 0)
    return pl.pallas_call(
        _experts_kernel,
        grid_spec=pltpu.PrefetchScalarGridSpec(
            num_scalar_prefetch=3, grid=(n_blocks,),
            in_specs=[pl.BlockSpec((MOE_BLK * PACKED_TILES, LANES), lambda j, be, bv, nu: (last(j, nu), 0)),
                      pl.BlockSpec((1, 1, D_MODEL, EXPERT_HIDDEN), wmap),
                      pl.BlockSpec((1, 1, D_MODEL, EXPERT_HIDDEN), wmap),
                      pl.BlockSpec((1, 1, EXPERT_HIDDEN, D_MODEL), wmap)],
            out_specs=pl.BlockSpec((MOE_BLK * PACKED_TILES, LANES), lambda j, be, bv, nu: (last(j, nu), 0)),
            scratch_shapes=[pltpu.VMEM((D_MODEL, EXPERT_HIDDEN), BF16), pltpu.VMEM((D_MODEL, EXPERT_HIDDEN), BF16),
                            pltpu.VMEM((EXPERT_HIDDEN, D_MODEL), BF16)]),
        out_shape=jax.ShapeDtypeStruct((n_blocks * MOE_BLK * PACKED_TILES, LANES), U32),
        compiler_params=_params(("arbitrary",)),
        name="experts",
    )(block_e, block_valid, n_used, xs, w_gate, w_up, w_down)


def _combine_kernel(dest_ref, next_ref, gate_ref, hp_ref, x_ref, gf_ref, g3_ref, sg_ref, su_ref, sd_ref,
                    ys_ref, o_ref, buf_ref, acc_ref, sem):
    c = PACKED_TILES
    tm = hp_ref.shape[0] // c
    i = pl.program_id(0)
    cur = i % 2

    parts = 4
    has_next = i + 1 < pl.num_programs(0)

    def gather(idx_ref, half, part):
        def issue(t, carry):
            slab = pl.ds(pl.multiple_of(t * c, c), c)
            for k in range(TOP_K):
                src = pl.multiple_of(idx_ref[0, 0, t * TOP_K + k], c)
                pltpu.make_async_copy(ys_ref.at[pl.ds(src, c)], buf_ref.at[half, k, slab],
                                      sem.at[half]).start(priority=k % 2)
            return carry
        lax.fori_loop(part * (tm // parts), (part + 1) * (tm // parts), issue, 0)

    def gather_next(part):
        pl.when(has_next)(lambda: gather(next_ref, 1 - cur, part))

    @pl.when(i == 0)
    def _():
        for part in range(parts):
            gather(dest_ref, 0, part)

    gather_next(0)
    hb = _unpack_bf16_pairs(_from_row_tiles(hp_ref, c))
    hid = (_silu(jnp.dot(hb, sg_ref[...], preferred_element_type=F32))
           * jnp.dot(hb, su_ref[...], preferred_element_type=F32))
    f = jnp.dot(hid.astype(BF16), sd_ref[...], preferred_element_type=F32)
    gather_next(1)
    for k in range(TOP_K):
        pltpu.make_async_copy(ys_ref.at[pl.ds(0, tm * c)], buf_ref.at[cur, k], sem.at[cur]).wait()

    def weighted(t, carry):
        slab = pl.ds(pl.multiple_of(t * c, c), c)
        acc_hi = jnp.zeros((c, LANES), F32)
        acc_lo = jnp.zeros((c, LANES), F32)
        for k in range(TOP_K):
            packed = buf_ref[cur, k, slab, :]
            gk = gate_ref[0, 0, t * TOP_K + k]
            acc_hi = acc_hi + gk * lax.bitcast_convert_type(packed & jnp.uint32(0xFFFF0000), F32)
            acc_lo = acc_lo + gk * lax.bitcast_convert_type(packed << 16, F32)
        acc_ref[0, slab, :] = acc_hi
        acc_ref[1, slab, :] = acc_lo
        return carry

    lax.fori_loop(0, tm // 2, weighted, 0, unroll=4)
    gather_next(2)
    lax.fori_loop(tm // 2, tm, weighted, 0, unroll=4)
    gather_next(3)
    f = f + jnp.concatenate([_from_row_tiles(acc_ref, c, lead=(0,)), _from_row_tiles(acc_ref, c, lead=(1,))], axis=1)
    o_ref[...] = x_ref[...] + gf_ref[0] * _rms(f, g3_ref[...])


def _combine(lay, dest_flat, gate_flat, hp, x_mid, mods, g3, s_gate, s_up, s_down, ys, lat_only, compact_out):
    row = lambda i: (lay.tile(i, lat_only), 0)
    full = lambda shape: pl.BlockSpec(shape, lambda i: (0, 0))
    out_rows = lay.n_tiles(lat_only) * TM if compact_out else lay.rows
    out_map = (lambda i: (i, 0)) if compact_out else row
    n_steps = lay.n_tiles(lat_only)
    flat = lambda f: pl.BlockSpec((1, 1, TM * TOP_K), lambda i: (lay.tile(f(i), lat_only), 0, 0),
                                  memory_space=pltpu.SMEM)
    return pl.pallas_call(
        _combine_kernel,
        grid=(n_steps,),
        in_specs=[flat(lambda i: i), flat(lambda i: jnp.minimum(i + 1, n_steps - 1)), flat(lambda i: i),
                  pl.BlockSpec((TM * PACKED_TILES, LANES), row),
                  pl.BlockSpec((TM, D_MODEL), row), _mod_spec(lay, 5, lat_only),
                  full((1, D_MODEL)), full((D_MODEL, EXPERT_HIDDEN)), full((D_MODEL, EXPERT_HIDDEN)),
                  full((EXPERT_HIDDEN, D_MODEL)), pl.BlockSpec(memory_space=pl.ANY)],
        out_specs=pl.BlockSpec((TM, D_MODEL), out_map),
        scratch_shapes=[pltpu.VMEM((2, TOP_K, TM * PACKED_TILES, LANES), U32),
                        pltpu.VMEM((2, TM * PACKED_TILES, LANES), F32), pltpu.SemaphoreType.DMA((2,))],
        out_shape=jax.ShapeDtypeStruct((out_rows, D_MODEL), F32),
        compiler_params=_params(("arbitrary",)),
        name="combine",
    )(dest_flat, dest_flat, gate_flat, hp, x_mid, mods, g3, s_gate, s_up, s_down, ys)


def _rope_tables(lay):
    n_freq = HEAD_DIM // 4
    inv_freq = ROPE_THETA ** (-jnp.arange(n_freq, dtype=F32) / n_freq)
    t = jnp.arange(lay.n_lat)
    pos = jnp.stack([(t // GRID_W).astype(F32), (t % GRID_W).astype(F32)], axis=-1)
    ang = pos[:, :, None, None] * inv_freq
    ang = jnp.broadcast_to(ang, (lay.n_lat, 2, 2, n_freq)).reshape(lay.n_lat, HEAD_DIM)
    sign = jnp.tile(jnp.concatenate([-jnp.ones(n_freq, F32), jnp.ones(n_freq, F32)]), 2)
    cos = jnp.concatenate([jnp.ones((lay.n_ctx, HEAD_DIM), F32), jnp.cos(ang)], axis=0)
    sin = jnp.concatenate([jnp.zeros((lay.n_ctx, HEAD_DIM), F32), jnp.sin(ang) * sign], axis=0)
    return cos, sin


def _moe_plan(counts, n_blocks):
    counts = counts.reshape(-1)
    padded = (counts + MOE_BLK - 1) // MOE_BLK * MOE_BLK
    pad_end = jnp.cumsum(padded)
    pad_start = (pad_end - padded).astype(I32)
    blk_row = jnp.arange(n_blocks, dtype=I32) * MOE_BLK
    block_e = jnp.minimum(jnp.sum(blk_row[:, None] >= pad_end[None, :], axis=1), N_EXPERTS - 1).astype(I32)
    own = block_e[:, None] == jnp.arange(N_EXPERTS, dtype=I32)[None, :]
    count_b = jnp.sum(jnp.where(own, counts[None, :], 0), axis=1)
    start_b = jnp.sum(jnp.where(own, pad_start[None, :], 0), axis=1)
    block_valid = jnp.clip(count_b - (blk_row - start_b), 0, MOE_BLK).astype(I32)
    n_used = jnp.maximum(pad_end[-1:] // MOE_BLK, 1).astype(I32)
    return pad_start, block_e, block_valid, n_used


def kernel(x, c, ctx, c_ctx, mod_w, mod_b, norm_g, w_in, w_out, conv_w, conv_b, lru_wr, lru_br, lru_wi, lru_bi,
           lru_lambda, gla_w2, gla_b2, gla_norm_g, qk_norm_g, win_sink, router_w, router_bias, exp_w_gate,
           exp_w_up, exp_w_down, sh_w_gate, sh_w_up, sh_w_down):
    bsz, n_lat, d = x.shape
    n_ctx = ctx.shape[1]
    depth = mod_w.shape[0]
    assert d == D_MODEL and bsz < MOD_ROWS
    lay = _Layout(bsz, n_lat, n_ctx)

    cc = jnp.zeros((MOD_ROWS, d), F32).at[:bsz].set(c).at[bsz].set(c_ctx)
    mods_all = _adaln(cc, mod_w, mod_b)
    cos, sin = _rope_tables(lay)
    xt = None

    splits = np.cumsum([0, W_A, W_B_RAW, W_C, W_D])
    for layer in range(depth):
        last = layer == depth - 1
        need_ctx = not last
        lat_only = last
        mods = mods_all[layer].reshape(MOD_ROWS * 6, 1, d)
        g = norm_g[layer]
        wl = w_in[layer]
        w_cat = jnp.concatenate(
            [wl[:, splits[0]:splits[1]], wl[:, splits[1]:splits[2]], jnp.zeros((d, W_B - W_B_RAW), F32),
             wl[:, splits[2]:splits[3]], wl[:, splits[3]:splits[4]]], axis=1).astype(BF16)
        if xt is None:
            p_a, p_b, p_c, p_d, xt = _inproj(lay, (x.reshape(bsz * n_lat, d), ctx.reshape(bsz * n_ctx, d)), mods,
                                             g[0:1], w_cat)
        else:
            p_a, p_b, p_c, p_d = _inproj(lay, (xt,), mods, g[0:1], w_cat)

        y_a = _rglru(lay, p_a, conv_w[layer], conv_b[layer], lru_wr[layer], lru_br[layer], lru_wi[layer],
                     lru_bi[layer], lru_lambda[layer])
        w2pad = jnp.zeros((2, LANES, GLA_HEADS * GLA_DK), F32)
        w2pad = w2pad.at[0, :GLA_RANK].set(gla_w2[layer, 0]).at[1, GLA_RANK:2 * GLA_RANK].set(gla_w2[layer, 1])
        y_b = _gla(lay, p_b, w2pad, gla_b2[layer], gla_norm_g[layer])
        y_c = _attention(lay, p_c, cos, sin, qk_norm_g[layer], window=False, need_ctx=need_ctx)
        y_d = _attention(lay, p_d, cos, sin, win_sink[layer], window=True, need_ctx=need_ctx)

        x_mid, h, hp = _outproj(lay, (y_a, y_b, y_c, y_d), w_out[layer].astype(BF16), xt, mods, g[1:2], g[2:3],
                                lat_only)
        top_e, gate, rank, counts = _router(lay, h, router_w[layer].T, router_bias[layer], lat_only)
        n_assign = lay.n_tiles(lat_only) * TM * TOP_K
        n_blocks = -(-(n_assign + N_EXPERTS * (MOE_BLK - 1)) // MOE_BLK)
        pad_start, block_e, block_valid, n_used = _moe_plan(counts, n_blocks)
        owner = top_e[:, :, None] == jnp.arange(N_EXPERTS, dtype=I32)
        dest = jnp.sum(jnp.where(owner, pad_start, 0), axis=-1) + rank
        dest_flat = (dest * PACKED_TILES).T.reshape(lay.rows // TM, 1, TM * TOP_K)
        xs = _dispatch(lay, dest_flat, hp, n_blocks * MOE_BLK, lat_only)
        ys = _experts(block_e, block_valid, n_used, xs, exp_w_gate, exp_w_up, exp_w_down, layer)
        gate_flat = gate.T.reshape(lay.rows // TM, 1, TM * TOP_K)
        xt = _combine(lay, dest_flat, gate_flat, hp, x_mid, mods, g[3:4], sh_w_gate[layer].astype(BF16),
                      sh_w_up[layer].astype(BF16), sh_w_down[layer].astype(BF16), ys, lat_only, compact_out=last)
    return xt.reshape(bsz, n_lat, d)
```
